```python
import math
import jax, jax.numpy as jnp
from jax import lax
import numpy as np

D_MODEL = 1024
BATCH = 4
SEQ = 8192
DEPTH = 1

CHUNK = 64
Q_BLOCK = 128
EPS = 1e-6
D_FF = 2816
N_MOD = 9

GDN_HEADS = 4
GDN_DK = 128
GDN_DV = 128
CONV_K = 4

MLA_HEADS = 4
MLA_NOPE = 128
MLA_ROPE = 64
MLA_V = 128
MLA_Q_LORA = 384
MLA_KV_LORA = 256
ROPE_BASE = 10000.0

GDN_WIDTH = GDN_HEADS * GDN_DV
MLA_WIDTH = MLA_HEADS * MLA_V
MIX_WIDTH = GDN_WIDTH + MLA_WIDTH

IN_SPLITS = (GDN_HEADS * GDN_DK,
             GDN_HEADS * GDN_DK,
             GDN_WIDTH,
             GDN_WIDTH,
             GDN_HEADS,
             GDN_HEADS,
             MLA_Q_LORA,
             MLA_KV_LORA,
             MLA_ROPE)
N_IN = sum(IN_SPLITS)
IN_OFFSETS = tuple(int(o) for o in np.cumsum(IN_SPLITS)[:-1])

kernel_name = "hybrid_gdn_mla_macaron_adaln_block"


def _rms(x, w=None):
    xf = x.astype(jnp.float32)
    y = xf * lax.rsqrt(jnp.mean(xf * xf, axis=-1, keepdims=True) + EPS)
    if w is not None:
        y = y * w.astype(jnp.float32)
    return y.astype(x.dtype)


def _l2n(x):
    return x * lax.rsqrt(jnp.sum(x * x, axis=-1, keepdims=True) + EPS)


def _modulate(x, shift, scale):
    return _rms(x) * (1.0 + scale[:, None, :]) + shift[:, None, :]


def _swiglu(h, w_in, w_out):
    gate, up = jnp.split(h @ w_in, 2, axis=-1)
    return (jax.nn.silu(gate) * up) @ w_out


def _rope(x, cos, sin):
    x1, x2 = jnp.split(x, 2, axis=-1)
    return jnp.concatenate([x1 * cos - x2 * sin, x2 * cos + x1 * sin], axis=-1)


def _causal_conv(x, w):
    return lax.conv_general_dilated(
        x, w[:, None, :].astype(x.dtype), window_strides=(1,),
        padding=[(CONV_K - 1, 0)], dimension_numbers=("NWC", "WIO", "NWC"),
        feature_group_count=x.shape[-1])


def _gated_delta_rule(q, k, v, g, beta):
    B, S, H, _ = q.shape
    nc = S // CHUNK

    def to_chunks(t):
        return t.reshape(B, nc, CHUNK, H, t.shape[-1]).transpose(0, 3, 1, 2, 4)

    q, k, v = to_chunks(q), to_chunks(k), to_chunks(v)
    g = g.reshape(B, nc, CHUNK, H).transpose(0, 3, 1, 2)
    beta = beta.reshape(B, nc, CHUNK, H).transpose(0, 3, 1, 2)

    G = jnp.cumsum(g, axis=-1)
    idx = jnp.arange(CHUNK)
    incl = idx[:, None] >= idx[None, :]
    strict = idx[:, None] > idx[None, :]
    decay = jnp.exp(jnp.where(incl, G[..., :, None] - G[..., None, :], -jnp.inf))

    kk = jnp.einsum('bhncd,bhnsd->bhncs', k, k)
    A = jnp.where(strict, beta[..., :, None] * kk * decay, 0.0)
    M = A + jnp.eye(CHUNK, dtype=A.dtype)
    rhs = jnp.concatenate([v * beta[..., None], k * (beta * jnp.exp(G))[..., None]], axis=-1)
    W = lax.linalg.triangular_solve(M, rhs, left_side=True, lower=True, unit_diagonal=True)
    u, wk = W[..., :GDN_DV], W[..., GDN_DV:]

    qk = jnp.einsum('bhncd,bhnsd->bhncs', q, k) * decay
    q_dec = q * jnp.exp(G)[..., None]
    k_dec = k * jnp.exp(G[..., -1:] - G)[..., None]
    g_last = jnp.exp(G[..., -1])

    xs = tuple(jnp.moveaxis(t, 2, 0) for t in (u, wk, q_dec, k_dec, qk, g_last))

    def step(state, inp):
        u_c, wk_c, qd_c, kd_c, qk_c, gl_c = inp
        v_new = u_c - jnp.einsum('bhck,bhkv->bhcv', wk_c, state)
        o_c = jnp.einsum('bhck,bhkv->bhcv', qd_c, state) + jnp.einsum('bhcs,bhsv->bhcv', qk_c, v_new)
        state = state * gl_c[..., None, None] + jnp.einsum('bhck,bhcv->bhkv', kd_c, v_new)
        return state, o_c

    s0 = jnp.zeros((B, H, GDN_DK, GDN_DV), jnp.float32)
    _, o = lax.scan(step, s0, xs)
    return o.transpose(1, 0, 3, 2, 4).reshape(B, S, H, GDN_DV)


def _hybrid_mixer(h, cos, sin, w_in, gdn_conv_w, gdn_a_log, gdn_dt_bias, gdn_norm_w,
                  mla_q_norm_w, mla_w_uq, mla_kv_norm_w, mla_w_ukv,
                  qkn_q_nope, qkn_q_rope, qkn_k_nope, qkn_k_rope, mla_out_norm_w, w_out):
    B, S, _ = h.shape
    nb = S // Q_BLOCK
    proj = h @ w_in
    gq, gk, gv, gz, ga, gb, cq, ckv, kr = jnp.split(proj, IN_OFFSETS, axis=-1)

    qkv = jax.nn.silu(_causal_conv(jnp.concatenate([gq, gk, gv], axis=-1), gdn_conv_w))
    q_a = qkv[..., :GDN_HEADS * GDN_DK].reshape(B, S, GDN_HEADS, GDN_DK).astype(jnp.float32)
    k_a = qkv[..., GDN_HEADS * GDN_DK:2 * GDN_HEADS * GDN_DK].reshape(B, S, GDN_HEADS, GDN_DK).astype(jnp.float32)
    v_a = qkv[..., 2 * GDN_HEADS * GDN_DK:].reshape(B, S, GDN_HEADS, GDN_DV).astype(jnp.float32)
    q_a = _l2n(q_a) * (GDN_DK ** -0.5)
    k_a = _l2n(k_a)
    beta = jax.nn.sigmoid(gb.astype(jnp.float32))
    g = -jnp.exp(gdn_a_log.astype(jnp.float32)) * jax.nn.softplus(
        ga.astype(jnp.float32) + gdn_dt_bias.astype(jnp.float32))
    o_a = _gated_delta_rule(q_a, k_a, v_a, g, beta).astype(h.dtype)
    o_a = _rms(o_a, gdn_norm_w) * jax.nn.silu(gz.reshape(B, S, GDN_HEADS, GDN_DV))

    qf = (_rms(cq, mla_q_norm_w) @ mla_w_uq).reshape(B, S, MLA_HEADS, MLA_NOPE + MLA_ROPE)
    kvf = (_rms(ckv, mla_kv_norm_w) @ mla_w_ukv).reshape(B, S, MLA_HEADS, MLA_NOPE + MLA_V)
    q_nope, q_rope = qf[..., :MLA_NOPE], qf[..., MLA_NOPE:]
    k_nope, v_b = kvf[..., :MLA_NOPE], kvf[..., MLA_NOPE:]
    scale = (MLA_NOPE + MLA_ROPE) ** -0.5
    q_nope = _rms(q_nope, qkn_q_nope) * scale
    q_rope = _rope(_rms(q_rope, qkn_q_rope), cos[:, :, None], sin[:, :, None]) * scale
    k_nope = _rms(k_nope, qkn_k_nope)
    k_rope = _rope(_rms(kr, qkn_k_rope), cos, sin)

    k_chunk = jnp.arange(S) // CHUNK
    q_chunk_b = k_chunk.reshape(nb, Q_BLOCK)
    qn_b = q_nope.reshape(B, nb, Q_BLOCK, MLA_HEADS, MLA_NOPE).transpose(1, 0, 2, 3, 4)
    qr_b = q_rope.reshape(B, nb, Q_BLOCK, MLA_HEADS, MLA_ROPE).transpose(1, 0, 2, 3, 4)

    def attend(blk):
        qn, qr, qc = blk
        s = (jnp.einsum('bqhd,bkhd->bhqk', qn, k_nope)
             + jnp.einsum('bqhd,bkd->bhqk', qr, k_rope)).astype(jnp.float32)
        s = jnp.where(qc[:, None] >= k_chunk[None, :], s, -jnp.inf)
        p = jax.nn.softmax(s, axis=-1).astype(v_b.dtype)
        return jnp.einsum('bhqk,bkhd->bqhd', p, v_b)

    o_b = lax.map(attend, (qn_b, qr_b, q_chunk_b))
    o_b = o_b.transpose(1, 0, 2, 3, 4).reshape(B, S, MLA_HEADS, MLA_V)
    o_b = _rms(o_b, mla_out_norm_w)

    mixed = jnp.concatenate([o_a.reshape(B, S, GDN_WIDTH), o_b.reshape(B, S, MLA_WIDTH)], axis=-1)
    return mixed @ w_out


def setup_inputs(seed: int = 0) -> dict:
    key = jax.random.key(seed)
    ks = jax.random.split(key, 32)
    f32 = jnp.float32
    L = DEPTH

    def nrm(k, shape, fan_in, mult=1.0):
        return jax.random.normal(k, shape, f32) * (mult * fan_in ** -0.5)

    def gain(k, shape):
        return 1.0 + 0.02 * jax.random.normal(k, shape, f32)

    x = jax.random.normal(ks[0], (BATCH, SEQ, D_MODEL), f32)
    c = jax.random.normal(ks[1], (BATCH, D_MODEL), f32)
    offset = jax.random.randint(ks[2], (BATCH, 1), 0, 4096, dtype=jnp.int32)
    positions = (offset + jnp.arange(SEQ, dtype=jnp.int32)[None, :]).astype(jnp.int32)

    dt = jnp.exp(jax.random.uniform(ks[10], (L, GDN_HEADS), f32, math.log(1e-3), math.log(1e-1)))
    return {
        "x": x,
        "c": c,
        "positions": positions,
        "w_ada": nrm(ks[3], (L, D_MODEL, N_MOD * D_MODEL), D_MODEL, 0.5),
        "b_ada": 0.02 * jax.random.normal(ks[4], (L, N_MOD * D_MODEL), f32),
        "ffn1_w_in": nrm(ks[5], (L, D_MODEL, 2 * D_FF), D_MODEL),
        "ffn1_w_out": nrm(ks[6], (L, D_FF, D_MODEL), D_FF),
        "w_in": nrm(ks[7], (L, D_MODEL, N_IN), D_MODEL),
        "gdn_conv_w": nrm(ks[8], (L, CONV_K, 3 * GDN_WIDTH), CONV_K),
        "gdn_a_log": jnp.log(jax.random.uniform(ks[9], (L, GDN_HEADS), f32, 1.0, 16.0)),
        "gdn_dt_bias": dt + jnp.log(-jnp.expm1(-dt)),
        "gdn_norm_w": gain(ks[11], (L, GDN_DV)),
        "mla_q_norm_w": gain(ks[12], (L, MLA_Q_LORA)),
        "mla_w_uq": nrm(ks[13], (L, MLA_Q_LORA, MLA_HEADS * (MLA_NOPE + MLA_ROPE)), MLA_Q_LORA),
        "mla_kv_norm_w": gain(ks[14], (L, MLA_KV_LORA)),
        "mla_w_ukv": nrm(ks[15], (L, MLA_KV_LORA, MLA_HEADS * (MLA_NOPE + MLA_V)), MLA_KV_LORA),
        "qkn_q_nope": gain(ks[16], (L, MLA_NOPE)),
        "qkn_q_rope": gain(ks[17], (L, MLA_ROPE)),
        "qkn_k_nope": gain(ks[18], (L, MLA_NOPE)),
        "qkn_k_rope": gain(ks[19], (L, MLA_ROPE)),
        "mla_out_norm_w": gain(ks[20], (L, MLA_V)),
        "w_out": nrm(ks[21], (L, MIX_WIDTH, D_MODEL), MIX_WIDTH),
        "ffn2_w_in": nrm(ks[22], (L, D_MODEL, 2 * D_FF), D_MODEL),
        "ffn2_w_out": nrm(ks[23], (L, D_FF, D_MODEL), D_FF),
    }


def reference(x, c, positions, w_ada, b_ada, ffn1_w_in, ffn1_w_out, w_in, gdn_conv_w,
              gdn_a_log, gdn_dt_bias, gdn_norm_w, mla_q_norm_w, mla_w_uq, mla_kv_norm_w,
              mla_w_ukv, qkn_q_nope, qkn_q_rope, qkn_k_nope, qkn_k_rope, mla_out_norm_w,
              w_out, ffn2_w_in, ffn2_w_out):
    half = MLA_ROPE // 2
    inv_freq = ROPE_BASE ** (-jnp.arange(half, dtype=jnp.float32) / half)
    ang = positions.astype(jnp.float32)[..., None] * inv_freq
    cos = jnp.cos(ang).astype(x.dtype)
    sin = jnp.sin(ang).astype(x.dtype)
    sc = jax.nn.silu(c)

    for l in range(DEPTH):
        mod = sc @ w_ada[l] + b_ada[l]
        sh1, s1, g1, sh2, s2, g2, sh3, s3, g3 = jnp.split(mod, N_MOD, axis=-1)
        h = _modulate(x, sh1, s1)
        x = x + 0.5 * g1[:, None, :] * _swiglu(h, ffn1_w_in[l], ffn1_w_out[l])
        h = _modulate(x, sh2, s2)
        y = _hybrid_mixer(h, cos, sin, w_in[l], gdn_conv_w[l], gdn_a_log[l], gdn_dt_bias[l],
                          gdn_norm_w[l], mla_q_norm_w[l], mla_w_uq[l], mla_kv_norm_w[l],
                          mla_w_ukv[l], qkn_q_nope[l], qkn_q_rope[l], qkn_k_nope[l],
                          qkn_k_rope[l], mla_out_norm_w[l], w_out[l])
        x = x + g2[:, None, :] * y
        h = _modulate(x, sh3, s3)
        x = x + 0.5 * g3[:, None, :] * _swiglu(h, ffn2_w_in[l], ffn2_w_out[l])
    return x
```

```python
import functools

import jax
import jax.numpy as jnp
from jax import lax
from jax.experimental import pallas as pl
from jax.experimental.pallas import tpu as pltpu

F32 = jnp.float32
BF16 = jnp.bfloat16

EPS = 1e-6
ATTN_CHUNK = 64
GDN_HEADS = 4
GDN_DK = 128
GDN_DV = 128
CONV_K = 4
MLA_HEADS = 4
MLA_NOPE = 128
MLA_ROPE = 64
MLA_V = 128
MLA_Q_LORA = 384
MLA_KV_LORA = 256
ROPE_BASE = 10000.0
N_MOD = 9

LANES = 128
SUBLANES = 8
VMEM_LIMIT = 56 * 1024 * 1024

GDN_CHUNK = 128


def _dot(a, b):
    return jnp.dot(a, b, preferred_element_type=F32)


def _dot_nt(a, b):
    return lax.dot_general(a, b, (((1,), (1,)), ((), ())), preferred_element_type=F32)


def _silu(x):
    return x * jax.nn.sigmoid(x)


def _rms_scale(x, n):
    return lax.rsqrt(jnp.sum(x * x, axis=-1, keepdims=True) * (1.0 / n) + EPS)


def _mod_kernel(c_ref, w_ref, b_ref, o_ref):
    sc = _silu(c_ref[...])
    o_ref[...] = jnp.dot(sc, w_ref[...], precision=lax.Precision.HIGHEST,
                         preferred_element_type=F32) + b_ref[...]


def _mod(c, w_ada, b_ada):
    bsz, d = c.shape
    n = w_ada.shape[1]
    tn = 1024
    return pl.pallas_call(
        _mod_kernel,
        grid=(n // tn,),
        in_specs=[pl.BlockSpec((bsz, d), lambda j: (0, 0)),
                  pl.BlockSpec((d, tn), lambda j: (0, j)),
                  pl.BlockSpec((1, tn), lambda j: (0, j))],
        out_specs=pl.BlockSpec((bsz, tn), lambda j: (0, j)),
        out_shape=jax.ShapeDtypeStruct((bsz, n), F32),
        compiler_params=pltpu.CompilerParams(dimension_semantics=("arbitrary",),
                                             vmem_limit_bytes=VMEM_LIMIT),
        name="mod",
    )(c, w_ada, b_ada.reshape(1, n))


def _ffn_kernel(*refs, has_mix, emit_next, d_model):
    it = iter(refs)
    x_ref = next(it)
    if has_mix:
        oa_ref, ob_ref, wmix_ref, gmix_ref = next(it), next(it), next(it), next(it)
    shift_ref, scale_ref, gate_ref = next(it), next(it), next(it)
    wg_ref, wu_ref, wo_ref = next(it), next(it), next(it)
    if emit_next:
        nshift_ref, nscale_ref = next(it), next(it)
    out_ref = next(it)
    if emit_next:
        hn_ref = next(it)
    h_sc, acc_sc = next(it), next(it)
    if has_mix:
        x_sc = next(it)

    f = pl.program_id(2)
    n_f = pl.num_programs(2)

    @pl.when(f == 0)
    def _():
        x = x_ref[0]
        if has_mix:
            half = oa_ref.shape[-1]
            y = _dot(oa_ref[0], wmix_ref[:half, :]) + _dot(ob_ref[0], wmix_ref[half:, :])
            x = x + gmix_ref[0] * y
            x_sc[...] = x
        h = x * _rms_scale(x, d_model) * (1.0 + scale_ref[0]) + shift_ref[0]
        h_sc[...] = h.astype(BF16)
        acc_sc[...] = jnp.zeros_like(acc_sc)

    h = h_sc[...]
    g = _dot(h, wg_ref[...])
    u = _dot(h, wu_ref[...])
    a = (_silu(g) * u).astype(BF16)
    acc_sc[...] += _dot(a, wo_ref[...])

    @pl.when(f == n_f - 1)
    def _():
        x = x_sc[...] if has_mix else x_ref[0]
        out = x + 0.5 * gate_ref[0] * acc_sc[...]
        out_ref[0] = out
        if emit_next:
            hn = out * _rms_scale(out, d_model) * (1.0 + nscale_ref[0]) + nshift_ref[0]
            hn_ref[0] = hn.astype(BF16)


def _ffn(x, shift, scale, gate, w_in_bf, w_out_bf, *, mix=None, nxt=None, tm=1024, tf=256):
    bsz, s, d = x.shape
    ff = w_out_bf.shape[0]
    tm = min(tm, s)
    n_f = ff // tf
    has_mix = mix is not None
    emit_next = nxt is not None

    row = lambda b, i, f: (b, i, 0)
    vec = lambda b, i, f: (b, 0, 0)
    args = [x]
    in_specs = [pl.BlockSpec((1, tm, d), row)]
    if has_mix:
        o_a, o_b, w_mix, g_mix = mix
        args += [o_a, o_b, w_mix, g_mix]
        in_specs += [pl.BlockSpec((1, tm, o_a.shape[-1]), row),
                     pl.BlockSpec((1, tm, o_b.shape[-1]), row),
                     pl.BlockSpec(w_mix.shape, lambda b, i, f: (0, 0)),
                     pl.BlockSpec((1, 1, d), vec)]
    args += [shift, scale, gate, w_in_bf, w_in_bf, w_out_bf]
    in_specs += [pl.BlockSpec((1, 1, d), vec)] * 3
    in_specs += [pl.BlockSpec((d, tf), lambda b, i, f: (0, f)),
                 pl.BlockSpec((d, tf), lambda b, i, f: (0, f + n_f)),
                 pl.BlockSpec((tf, d), lambda b, i, f: (f, 0))]
    if emit_next:
        args += list(nxt)
        in_specs += [pl.BlockSpec((1, 1, d), vec)] * 2
    out_shape = [jax.ShapeDtypeStruct((bsz, s, d), F32)]
    out_specs = [pl.BlockSpec((1, tm, d), row)]
    if emit_next:
        out_shape.append(jax.ShapeDtypeStruct((bsz, s, d), BF16))
        out_specs.append(pl.BlockSpec((1, tm, d), row))
    scratch = [pltpu.VMEM((tm, d), BF16), pltpu.VMEM((tm, d), F32)]
    if has_mix:
        scratch.append(pltpu.VMEM((tm, d), F32))

    res = pl.pallas_call(
        functools.partial(_ffn_kernel, has_mix=has_mix, emit_next=emit_next, d_model=d),
        grid=(bsz, s // tm, n_f),
        in_specs=in_specs,
        out_specs=out_specs,
        out_shape=out_shape,
        scratch_shapes=scratch,
        compiler_params=pltpu.CompilerParams(
            dimension_semantics=("parallel", "parallel", "arbitrary"),
            vmem_limit_bytes=VMEM_LIMIT),
        name="ffn_mix" if has_mix else "ffn",
    )(*args)
    return res if emit_next else res[0]


def _unit_lower_inverse(a_mat, ii, jj):
    n = a_mat.shape[0]
    eye = (ii == jj).astype(F32)

    def same_block(size):
        return (ii // size) == (jj // size)

    d1 = jnp.where(same_block(SUBLANES), a_mat, 0.0)
    d1b = d1.astype(BF16)
    d2 = _dot(d1b, d1b)
    d2b = d2.astype(BF16)
    d3 = _dot(d1b, d2b)
    d4 = _dot(d2b, d2b)
    p1 = eye - d1 + d2 - d3
    t = p1 + _dot(p1.astype(BF16), d4.astype(BF16))
    size = SUBLANES
    while size < n:
        off = jnp.where(same_block(2 * size) & jnp.logical_not(same_block(size)), a_mat, 0.0)
        x = _dot(off.astype(BF16), t.astype(BF16))
        t = t - _dot(t.astype(BF16), x.astype(BF16))
        size *= 2
    return t


def _split3(x):
    hi = x.astype(BF16)
    r1 = x - hi.astype(F32)
    mid = r1.astype(BF16)
    lo = (r1 - mid.astype(F32)).astype(BF16)
    return hi, mid, lo


def _gdn_kernel(h_ref, wqkvz_ref, wab_ref, convw_ref, alog_ref, dtb_ref, normw_ref,
                o_ref, xbuf, state, *, t_blk, chunk):
    nh, dk, dv = GDN_HEADS, GDN_DK, GDN_DV
    qk_w = nh * dk
    conv_w = 2 * qk_w + nh * dv
    pad = SUBLANES

    @pl.when(pl.program_id(1) == 0)
    def _():
        state[...] = jnp.zeros_like(state)
        xbuf[0:pad, :] = jnp.zeros((pad, conv_w), F32)

    h = h_ref[0]
    proj = _dot(h, wqkvz_ref[...])
    ab = _dot(h, wab_ref[...])

    xbuf[pad:pad + t_blk, :] = proj[:, :conv_w]
    cw = convw_ref[...]
    conv = xbuf[pad:pad + t_blk, :] * cw[CONV_K - 1:CONV_K, :]
    for tap in range(1, CONV_K):
        conv = conv + xbuf[pad - tap:pad - tap + t_blk, :] * cw[CONV_K - 1 - tap:CONV_K - tap, :]
    xbuf[0:pad, :] = xbuf[t_blk:t_blk + pad, :]
    qkv = _silu(conv)

    g_all = -jnp.exp(alog_ref[...]) * jax.nn.softplus(ab + dtb_ref[...])
    beta_all = jax.nn.sigmoid(ab)

    ii = lax.broadcasted_iota(jnp.int32, (chunk, chunk), 0)
    jj = lax.broadcasted_iota(jnp.int32, (chunk, chunk), 1)
    incl = ii >= jj
    strict = ii > jj
    tri = incl.astype(BF16)

    for c in range(t_blk // chunk):
        rows = slice(c * chunk, (c + 1) * chunk)
        g_hi, g_mid, g_lo = _split3(g_all[rows, :])
        gcum = _dot(tri, g_hi) + _dot(tri, g_mid) + _dot(tri, g_lo)
        gcum_t = gcum.T
        beta_c = beta_all[rows, :]
        for hh in range(nh):
            q = qkv[rows, hh * dk:(hh + 1) * dk]
            k = qkv[rows, qk_w + hh * dk:qk_w + (hh + 1) * dk]
            v = qkv[rows, 2 * qk_w + hh * dv:2 * qk_w + (hh + 1) * dv]
            z = proj[rows, conv_w + hh * dv:conv_w + (hh + 1) * dv]
            q = q * lax.rsqrt(jnp.sum(q * q, axis=-1, keepdims=True) + EPS) * (dk ** -0.5)
            k = k * lax.rsqrt(jnp.sum(k * k, axis=-1, keepdims=True) + EPS)

            g_col = gcum[:, hh:hh + 1]
            g_row = gcum_t[hh:hh + 1, :]
            b_col = beta_c[:, nh + hh:nh + hh + 1]
            decay = jnp.exp(jnp.where(incl, g_col - g_row, -jnp.inf))
            kb = k.astype(BF16)
            kk = _dot_nt(kb, kb)
            a_mat = jnp.where(strict, b_col * kk * decay, 0.0)
            t_inv = _unit_lower_inverse(a_mat, ii, jj)

            e_g = jnp.exp(g_col)
            rhs = jnp.concatenate([v * b_col, k * (b_col * e_g)], axis=1).astype(BF16)
            w_all = _dot(t_inv.astype(BF16), rhs)
            u = w_all[:, :dv]
            wk = w_all[:, dv:]
            qk = _dot_nt(q.astype(BF16), kb) * decay
            g_last = gcum[chunk - 1:chunk, hh:hh + 1]
            q_dec = (q * e_g).astype(BF16)
            k_dec = (k * jnp.exp(g_last - g_col)).astype(BF16)

            s_old = state[hh]
            s_b = s_old.astype(BF16)
            v_new = u - _dot(wk.astype(BF16), s_b)
            v_nb = v_new.astype(BF16)
            o = _dot(q_dec, s_b) + _dot(qk.astype(BF16), v_nb)
            state[hh] = s_old * jnp.exp(g_last) + _dot(k_dec.T, v_nb)

            o = o * _rms_scale(o, dv) * normw_ref[...] * _silu(z)
            o_ref[0, rows, hh * dv:(hh + 1) * dv] = o.astype(o_ref.dtype)


def _gdn(h2, w_qkvz, w_ab, conv_w, a_log, dt_bias, norm_w, *, t_blk=512):
    bsz, s, d = h2.shape
    t_blk = min(t_blk, s)
    nh, dv = GDN_HEADS, GDN_DV
    cw = conv_w.shape[1]
    full = lambda b, j: (0, 0)
    return pl.pallas_call(
        functools.partial(_gdn_kernel, t_blk=t_blk, chunk=GDN_CHUNK),
        grid=(bsz, s // t_blk),
        in_specs=[pl.BlockSpec((1, t_blk, d), lambda b, j: (b, j, 0)),
                  pl.BlockSpec(w_qkvz.shape, full),
                  pl.BlockSpec(w_ab.shape, full),
                  pl.BlockSpec(conv_w.shape, full),
                  pl.BlockSpec((1, LANES), full),
                  pl.BlockSpec((1, LANES), full),
                  pl.BlockSpec((1, dv), full)],
        out_specs=pl.BlockSpec((1, t_blk, nh * dv), lambda b, j: (b, j, 0)),
        out_shape=jax.ShapeDtypeStruct((bsz, s, nh * dv), BF16),
        scratch_shapes=[pltpu.VMEM((t_blk + 2 * SUBLANES, cw), F32),
                        pltpu.VMEM((nh, GDN_DK, dv), F32)],
        compiler_params=pltpu.CompilerParams(
            dimension_semantics=("parallel", "arbitrary"),
            vmem_limit_bytes=VMEM_LIMIT),
        name="gdn",
    )(h2, w_qkvz, w_ab, conv_w, a_log, dt_bias, norm_w)


def _mla_prep_kernel(h_ref, pos_ref, wm_ref, wuq_ref, wukv_ref, qnw_ref, kvnw_ref,
                     wqn_ref, wqr_ref, wqrs_ref, wkn_ref, wkr_ref, wkrs_ref,
                     invf_ref, sgn_ref, qt_ref, k_ref, vt_ref):
    nh = MLA_HEADS
    scale = (MLA_NOPE + MLA_ROPE) ** -0.5
    h = h_ref[0]
    lat = _dot(h, wm_ref[...])
    o1 = MLA_Q_LORA
    o2 = o1 + MLA_KV_LORA
    cq = lat[:, :o1]
    ckv = lat[:, o1:o2]
    kr = lat[:, o2:o2 + LANES]
    krs = lat[:, o2 + LANES:o2 + 2 * LANES]
    cqn = (cq * _rms_scale(cq, MLA_Q_LORA) * qnw_ref[...]).astype(BF16)
    ckvn = (ckv * _rms_scale(ckv, MLA_KV_LORA) * kvnw_ref[...]).astype(BF16)
    qf = _dot(cqn, wuq_ref[...])
    kvf = _dot(ckvn, wukv_ref[...])

    ang = pos_ref[0].astype(F32) * invf_ref[...]
    cos_p = jnp.cos(ang)
    sin_p = jnp.sin(ang) * sgn_ref[...]

    k_rope = (kr * wkr_ref[...] * cos_p + krs * wkrs_ref[...] * sin_p) * _rms_scale(kr, MLA_ROPE)
    qw = MLA_NOPE + 2 * LANES
    kw = MLA_NOPE + MLA_V
    for hh in range(nh):
        qn = qf[:, hh * qw:hh * qw + MLA_NOPE]
        qr = qf[:, hh * qw + MLA_NOPE:hh * qw + MLA_NOPE + LANES]
        qrs = qf[:, hh * qw + MLA_NOPE + LANES:(hh + 1) * qw]
        qn = qn * _rms_scale(qn, MLA_NOPE) * (wqn_ref[...] * scale)
        q_rope = ((qr * wqr_ref[...] * cos_p + qrs * wqrs_ref[...] * sin_p)
                  * (_rms_scale(qr, MLA_ROPE) * scale))
        qt_ref[0, hh] = jnp.concatenate([qn, q_rope], axis=1).T.astype(BF16)
        kn = kvf[:, hh * kw:hh * kw + MLA_NOPE]
        v = kvf[:, hh * kw + MLA_NOPE:(hh + 1) * kw]
        kn = kn * _rms_scale(kn, MLA_NOPE) * wkn_ref[...]
        k_ref[0, hh] = jnp.concatenate([kn, k_rope], axis=1).astype(BF16)
        vt_ref[0, hh] = v.T.astype(BF16)


def _mla_prep(h2, pos3, w_m, w_uq, w_ukv, vecs, *, tm=512):
    bsz, s, d = h2.shape
    tm = min(tm, s)
    nh = MLA_HEADS
    qk_dim = MLA_NOPE + LANES
    full = lambda b, i: (0, 0)
    vec_specs = [pl.BlockSpec(v.shape, full) for v in vecs]
    return pl.pallas_call(
        _mla_prep_kernel,
        grid=(bsz, s // tm),
        in_specs=[pl.BlockSpec((1, tm, d), lambda b, i: (b, i, 0)),
                  pl.BlockSpec((1, tm, 1), lambda b, i: (b, i, 0)),
                  pl.BlockSpec(w_m.shape, full),
                  pl.BlockSpec(w_uq.shape, full),
                  pl.BlockSpec(w_ukv.shape, full)] + vec_specs,
        out_specs=[pl.BlockSpec((1, nh, qk_dim, tm), lambda b, i: (b, 0, 0, i)),
                   pl.BlockSpec((1, nh, tm, qk_dim), lambda b, i: (b, 0, i, 0)),
                   pl.BlockSpec((1, nh, MLA_V, tm), lambda b, i: (b, 0, 0, i))],
        out_shape=[jax.ShapeDtypeStruct((bsz, nh, qk_dim, s), BF16),
                   jax.ShapeDtypeStruct((bsz, nh, s, qk_dim), BF16),
                   jax.ShapeDtypeStruct((bsz, nh, MLA_V, s), BF16)],
        compiler_params=pltpu.CompilerParams(
            dimension_semantics=("parallel", "parallel"),
            vmem_limit_bytes=VMEM_LIMIT),
        name="mla_prep",
    )(h2, pos3, w_m, w_uq, w_ukv, *vecs)


def _attn_kernel(qt_ref, k_ref, vt_ref, nw_ref, o_ref, *, tq, tk):
    i = pl.program_id(2)
    qt = qt_ref[0, 0]

    def step(j, carry, masked):
        m, l, acc = carry
        start = pl.multiple_of(j * tk, tk)
        kj = k_ref[0, 0, pl.ds(start, tk), :]
        st = _dot(kj, qt)
        if masked:
            kc = (start + lax.broadcasted_iota(jnp.int32, (tk, tq), 0)) // ATTN_CHUNK
            qc = (i * tq + lax.broadcasted_iota(jnp.int32, (tk, tq), 1)) // ATTN_CHUNK
            st = jnp.where(qc >= kc, st, -jnp.inf)
        m_new = jnp.maximum(m, jnp.max(st, axis=0, keepdims=True))
        alpha = jnp.exp(m - m_new)
        p = jnp.exp(st - m_new)
        l_new = alpha * l + jnp.sum(p, axis=0, keepdims=True)
        vj = vt_ref[0, 0, :, pl.ds(start, tk)]
        acc_new = alpha * acc + _dot(vj, p.astype(BF16))
        return m_new, l_new, acc_new

    init = (jnp.full((1, tq), -jnp.inf, F32), jnp.zeros((1, tq), F32),
            jnp.zeros((MLA_V, tq), F32))
    per_q = tq // tk
    carry = lax.fori_loop(0, i * per_q, lambda j, c: step(j, c, False), init)
    for r in range(per_q):
        carry = step(i * per_q + r, carry, True)
    _, l, acc = carry
    o = (acc / l).T
    o = o * _rms_scale(o, MLA_V) * nw_ref[...]
    o_ref[0] = o.astype(o_ref.dtype)


def _attn(qt, k, vt, out_norm_w, *, tq=512, tk=512):
    bsz, nh, qk_dim, s = qt.shape
    tq = min(tq, s)
    tk = min(tk, tq)
    return pl.pallas_call(
        functools.partial(_attn_kernel, tq=tq, tk=tk),
        grid=(bsz, nh, s // tq),
        in_specs=[pl.BlockSpec((1, 1, qk_dim, tq), lambda b, h, i: (b, h, 0, i)),
                  pl.BlockSpec((1, 1, s, qk_dim), lambda b, h, i: (b, h, 0, 0)),
                  pl.BlockSpec((1, 1, MLA_V, s), lambda b, h, i: (b, h, 0, 0)),
                  pl.BlockSpec((1, MLA_V), lambda b, h, i: (0, 0))],
        out_specs=pl.BlockSpec((1, tq, MLA_V), lambda b, h, i: (b, i, h)),
        out_shape=jax.ShapeDtypeStruct((bsz, s, nh * MLA_V), BF16),
        compiler_params=pltpu.CompilerParams(
            dimension_semantics=("parallel", "parallel", "arbitrary"),
            vmem_limit_bytes=VMEM_LIMIT),
        name="attn",
    )(qt, k, vt, out_norm_w)


def _pad_lanes(w, width=LANES):
    return jnp.pad(w, [(0, 0)] * (w.ndim - 1) + [(0, width - w.shape[-1])])


def _swap_halves(w):
    half = w.shape[-1] // 2
    return jnp.concatenate([w[..., half:], w[..., :half]], axis=-1)


def _layer(x, mods, pos3, w_ffn1_in, w_ffn1_out, w_in, conv_w, a_log, dt_bias, gdn_norm_w,
           q_norm_w, w_uq, kv_norm_w, w_ukv, qn_q_nope, qn_q_rope, qn_k_nope, qn_k_rope,
           out_norm_w, w_out, w_ffn2_in, w_ffn2_out):
    sh1, s1, g1, sh2, s2, g2, sh3, s3, g3 = mods
    nh = MLA_HEADS

    o_gz = 2 * GDN_HEADS * GDN_DK + 2 * GDN_HEADS * GDN_DV
    o_ab = o_gz + 2 * GDN_HEADS
    o_cq = o_ab + MLA_Q_LORA
    o_ckv = o_cq + MLA_KV_LORA
    w_qkvz = w_in[:, :o_gz].astype(BF16)
    w_ab = _pad_lanes(w_in[:, o_gz:o_ab]).astype(BF16)
    w_kr = w_in[:, o_ckv:]
    w_m = jnp.concatenate([w_in[:, o_ab:o_ckv], _pad_lanes(w_kr), _pad_lanes(_swap_halves(w_kr))],
                          axis=1).astype(BF16)
    per_q = MLA_NOPE + MLA_ROPE
    uq_parts = []
    for hh in range(nh):
        wn = w_uq[:, hh * per_q:hh * per_q + MLA_NOPE]
        wr = w_uq[:, hh * per_q + MLA_NOPE:(hh + 1) * per_q]
        uq_parts += [wn, _pad_lanes(wr), _pad_lanes(_swap_halves(wr))]
    w_uq_p = jnp.concatenate(uq_parts, axis=1).astype(BF16)
    w_ukv_b = w_ukv.astype(BF16)

    half = MLA_ROPE // 2
    inv_freq = ROPE_BASE ** (-jnp.arange(half, dtype=F32) / half)
    invf = _pad_lanes(jnp.concatenate([inv_freq, inv_freq])).reshape(1, LANES)
    sgn = _pad_lanes(jnp.concatenate([-jnp.ones((half,), F32), jnp.ones((half,), F32)])).reshape(1, LANES)
    row = lambda v: v.reshape(1, -1)
    vecs = [row(q_norm_w), row(kv_norm_w),
            row(qn_q_nope), row(_pad_lanes(qn_q_rope)), row(_pad_lanes(_swap_halves(qn_q_rope))),
            row(qn_k_nope), row(_pad_lanes(qn_k_rope)), row(_pad_lanes(_swap_halves(qn_k_rope))),
            invf, sgn]

    x1, h2 = _ffn(x, sh1, s1, g1, w_ffn1_in.astype(BF16), w_ffn1_out.astype(BF16), nxt=(sh2, s2))

    o_a = _gdn(h2, w_qkvz, w_ab, conv_w, row(_pad_lanes(a_log)), row(_pad_lanes(dt_bias)),
               row(gdn_norm_w))
    qt, k, vt = _mla_prep(h2, pos3, w_m, w_uq_p, w_ukv_b, vecs)
    o_b = _attn(qt, k, vt, row(out_norm_w))

    return _ffn(x1, sh3, s3, g3, w_ffn2_in.astype(BF16), w_ffn2_out.astype(BF16),
                mix=(o_a, o_b, w_out.astype(BF16), g2))


def kernel(x, c, positions, w_ada, b_ada, ffn1_w_in, ffn1_w_out, w_in, gdn_conv_w, gdn_a_log, gdn_dt_bias, gdn_norm_w, mla_q_norm_w, mla_w_uq, mla_kv_norm_w, mla_w_ukv, qkn_q_nope, qkn_q_rope, qkn_k_nope, qkn_k_rope, mla_out_norm_w, w_out, ffn2_w_in, ffn2_w_out):
    bsz, s, d = x.shape
    pos3 = positions.reshape(bsz, s, 1)
    for l in range(w_ada.shape[0]):
        mod = _mod(c, w_ada[l], b_ada[l])
        mods = [m.reshape(bsz, 1, d) for m in jnp.split(mod, N_MOD, axis=-1)]
        x = _layer(x, mods, pos3, ffn1_w_in[l], ffn1_w_out[l], w_in[l], gdn_conv_w[l],
                   gdn_a_log[l], gdn_dt_bias[l], gdn_norm_w[l], mla_q_norm_w[l], mla_w_uq[l],
                   mla_kv_norm_w[l], mla_w_ukv[l], qkn_q_nope[l], qkn_q_rope[l], qkn_k_nope[l],
                   qkn_k_rope[l], mla_out_norm_w[l], w_out[l], ffn2_w_in[l], ffn2_w_out[l])
    return x
```

```python
import functools

import jax
import jax.numpy as jnp
from jax import lax
from jax.experimental import pallas as pl
from jax.experimental.pallas import tpu as pltpu

F32 = jnp.float32
BF16 = jnp.bfloat16

EPS = 1e-6
ATTN_CHUNK = 64
GDN_HEADS = 4
GDN_DK = 128
GDN_DV = 128
CONV_K = 4
MLA_HEADS = 4
MLA_NOPE = 128
MLA_ROPE = 64
MLA_V = 128
MLA_Q_LORA = 384
MLA_KV_LORA = 256
ROPE_BASE = 10000.0
N_MOD = 9
LOG2_E = 1.4426950408889634

LANES = 128
SUBLANES = 8
VMEM_LIMIT = 56 * 1024 * 1024

GDN_CHUNK = 128


def _dot(a, b):
    return jnp.dot(a, b, preferred_element_type=F32)


def _dot_nt(a, b):
    return lax.dot_general(a, b, (((1,), (1,)), ((), ())), preferred_element_type=F32)


def _silu(x):
    return x * jax.nn.sigmoid(x)


def _rms_scale(x, n):
    return lax.rsqrt(jnp.sum(x * x, axis=-1, keepdims=True) * (1.0 / n) + EPS)


def _mod_kernel(c_ref, w_ref, b_ref, o_ref):
    sc = _silu(c_ref[...])
    o_ref[...] = jnp.dot(sc, w_ref[...], precision=lax.Precision.HIGHEST,
                         preferred_element_type=F32) + b_ref[...]


def _mod(c, w_ada, b_ada):
    bsz, d = c.shape
    n = w_ada.shape[1]
    tn = 1024
    return pl.pallas_call(
        _mod_kernel,
        grid=(n // tn,),
        in_specs=[pl.BlockSpec((bsz, d), lambda j: (0, 0)),
                  pl.BlockSpec((d, tn), lambda j: (0, j)),
                  pl.BlockSpec((1, tn), lambda j: (0, j))],
        out_specs=pl.BlockSpec((bsz, tn), lambda j: (0, j)),
        out_shape=jax.ShapeDtypeStruct((bsz, n), F32),
        compiler_params=pltpu.CompilerParams(dimension_semantics=("arbitrary",),
                                             vmem_limit_bytes=VMEM_LIMIT),
        name="mod",
    )(c, w_ada, b_ada.reshape(1, n))


def _ffn_kernel(*refs, has_mix, emit_next, d_model, d_ff, tf):
    it = iter(refs)
    x_ref = next(it)
    if has_mix:
        oa_ref, ob_ref, wmix_ref, gmix_ref = next(it), next(it), next(it), next(it)
    shift_ref, scale_ref, gate_ref = next(it), next(it), next(it)
    win_ref, wout_ref = next(it), next(it)
    if emit_next:
        nshift_ref, nscale_ref = next(it), next(it)
    out_ref = next(it)
    if emit_next:
        hn_ref = next(it)
    a_sc = next(it)

    x = x_ref[0]
    if has_mix:
        half = oa_ref.shape[-1]
        y = _dot(oa_ref[0], wmix_ref[:half, :]) + _dot(ob_ref[0], wmix_ref[half:, :])
        x = x + gmix_ref[0] * y
    h = (x * _rms_scale(x, d_model) * (1.0 + scale_ref[0]) + shift_ref[0]).astype(BF16)
    for c in range(d_ff // tf):
        g = _dot(h, win_ref[:, c * tf:(c + 1) * tf])
        u = _dot(h, win_ref[:, d_ff + c * tf:d_ff + (c + 1) * tf])
        a_sc[:, c * tf:(c + 1) * tf] = (_silu(g) * u).astype(BF16)
    out = x + 0.5 * gate_ref[0] * _dot(a_sc[...], wout_ref[...])
    out_ref[0] = out
    if emit_next:
        hn = out * _rms_scale(out, d_model) * (1.0 + nscale_ref[0]) + nshift_ref[0]
        hn_ref[0] = hn.astype(BF16)


def _resident(shape, index_map):
    return pl.BlockSpec(shape, index_map, pipeline_mode=pl.Buffered(1))


def _ffn(x, shift, scale, gate, w_in_bf, w_out_bf, *, mix=None, nxt=None, tm=512, tf=256):
    bsz, s, d = x.shape
    ff = w_out_bf.shape[0]
    tm = min(tm, s)
    assert ff % tf == 0 and s % tm == 0
    has_mix = mix is not None
    emit_next = nxt is not None

    row = lambda b, i: (b, i, 0)
    vec = lambda b, i: (b, 0, 0)
    const = lambda b, i: (0, 0)
    args = [x]
    in_specs = [pl.BlockSpec((1, tm, d), row)]
    if has_mix:
        o_a, o_b, w_mix, g_mix = mix
        args += [o_a, o_b, w_mix, g_mix]
        in_specs += [pl.BlockSpec((1, tm, o_a.shape[-1]), row),
                     pl.BlockSpec((1, tm, o_b.shape[-1]), row),
                     _resident(w_mix.shape, const),
                     pl.BlockSpec((1, 1, d), vec)]
    args += [shift, scale, gate, w_in_bf, w_out_bf]
    in_specs += [pl.BlockSpec((1, 1, d), vec)] * 3
    in_specs += [_resident(w_in_bf.shape, const), _resident(w_out_bf.shape, const)]
    if emit_next:
        args += list(nxt)
        in_specs += [pl.BlockSpec((1, 1, d), vec)] * 2
    out_shape = [jax.ShapeDtypeStruct((bsz, s, d), F32)]
    out_specs = [pl.BlockSpec((1, tm, d), row)]
    if emit_next:
        out_shape.append(jax.ShapeDtypeStruct((bsz, s, d), BF16))
        out_specs.append(pl.BlockSpec((1, tm, d), row))

    res = pl.pallas_call(
        functools.partial(_ffn_kernel, has_mix=has_mix, emit_next=emit_next, d_model=d,
                          d_ff=ff, tf=tf),
        grid=(bsz, s // tm),
        in_specs=in_specs,
        out_specs=out_specs,
        out_shape=out_shape,
        scratch_shapes=[pltpu.VMEM((tm, ff), BF16)],
        compiler_params=pltpu.CompilerParams(
            dimension_semantics=("parallel", "parallel"),
            vmem_limit_bytes=VMEM_LIMIT),
        name="ffn_mix" if has_mix else "ffn",
    )(*args)
    return res if emit_next else res[0]


def _unit_lower_inverses(a_mats, ii, jj):
    n = a_mats[0].shape[0]
    eye = (ii == jj).astype(F32)

    def same_block(size):
        return (ii // size) == (jj // size)

    base = same_block(SUBLANES)
    d1 = [jnp.where(base, a, 0.0) for a in a_mats]
    d1b = [d.astype(BF16) for d in d1]
    d2 = [_dot(d, d) for d in d1b]
    d2b = [d.astype(BF16) for d in d2]
    d3 = [_dot(a, b) for a, b in zip(d1b, d2b)]
    d4b = [_dot(d, d).astype(BF16) for d in d2b]
    p1 = [eye - a + b - c for a, b, c in zip(d1, d2, d3)]
    ts = [p + _dot(p.astype(BF16), d) for p, d in zip(p1, d4b)]
    size = SUBLANES
    while size < n:
        below = same_block(2 * size) & jnp.logical_not(same_block(size))
        offs = [jnp.where(below, a, 0.0).astype(BF16) for a in a_mats]
        tbs = [t.astype(BF16) for t in ts]
        xs = [_dot(o, t).astype(BF16) for o, t in zip(offs, tbs)]
        ts = [t - _dot(tb, x) for t, tb, x in zip(ts, tbs, xs)]
        size *= 2
    return ts


def _split3(x):
    hi = x.astype(BF16)
    r1 = x - hi.astype(F32)
    mid = r1.astype(BF16)
    lo = (r1 - mid.astype(F32)).astype(BF16)
    return hi, mid, lo


def _gdn_kernel(h_ref, wqkvz_ref, wab_ref, convw_ref, alog_ref, dtb_ref, normw_ref,
                o_ref, xbuf, state, *, chunk):
    nb, _, d = h_ref.shape
    nh, dk, dv = GDN_HEADS, GDN_DK, GDN_DV
    qk_w = nh * dk
    conv_w = 2 * qk_w + nh * dv
    pad = SUBLANES
    probs = [(b, hh) for b in range(nb) for hh in range(nh)]

    @pl.when(pl.program_id(0) == 0)
    def _():
        state[...] = jnp.zeros_like(state)
        xbuf[:, 0:pad, :] = jnp.zeros((nb, pad, conv_w), F32)

    h = h_ref[...].reshape(nb * chunk, d)
    proj = _dot(h, wqkvz_ref[...])
    ab = _dot(h, wab_ref[...])
    g_all = -jnp.exp(alog_ref[...]) * jax.nn.softplus(ab + dtb_ref[...])
    beta_all = jax.nn.sigmoid(ab)

    ii = lax.broadcasted_iota(jnp.int32, (chunk, chunk), 0)
    jj = lax.broadcasted_iota(jnp.int32, (chunk, chunk), 1)
    incl = ii >= jj
    strict = ii > jj
    tri = incl.astype(BF16)
    cw = convw_ref[...]

    qkv, gcum, gcum_t = [], [], []
    for b in range(nb):
        rows = slice(b * chunk, (b + 1) * chunk)
        xbuf[b, pad:pad + chunk, :] = proj[rows, :conv_w]
        conv = xbuf[b, pad:pad + chunk, :] * cw[CONV_K - 1:CONV_K, :]
        for tap in range(1, CONV_K):
            conv = conv + (xbuf[b, pad - tap:pad - tap + chunk, :]
                           * cw[CONV_K - 1 - tap:CONV_K - tap, :])
        xbuf[b, 0:pad, :] = xbuf[b, chunk:chunk + pad, :]
        qkv.append(_silu(conv))
        g_hi, g_mid, g_lo = _split3(g_all[rows, :])
        gc = _dot(tri, g_hi) + _dot(tri, g_mid) + _dot(tri, g_lo)
        gcum.append(gc)
        gcum_t.append(gc.T)

    qs, ks, vs, kbs, g_cols, b_cols, decays = [], [], [], [], [], [], []
    for b, hh in probs:
        q = qkv[b][:, hh * dk:(hh + 1) * dk]
        k = qkv[b][:, qk_w + hh * dk:qk_w + (hh + 1) * dk]
        vs.append(qkv[b][:, 2 * qk_w + hh * dv:2 * qk_w + (hh + 1) * dv])
        q = q * (lax.rsqrt(jnp.sum(q * q, axis=-1, keepdims=True) + EPS) * (dk ** -0.5))
        k = k * lax.rsqrt(jnp.sum(k * k, axis=-1, keepdims=True) + EPS)
        qs.append(q)
        ks.append(k)
        kbs.append(k.astype(BF16))
        g_col = gcum[b][:, hh:hh + 1]
        g_row = gcum_t[b][hh:hh + 1, :]
        g_cols.append(g_col)
        b_cols.append(beta_all[b * chunk:(b + 1) * chunk, nh + hh:nh + hh + 1])
        decays.append(jnp.exp(jnp.where(incl, g_col - g_row, -jnp.inf)))

    kks = [_dot_nt(kb, kb) for kb in kbs]
    qks = [(_dot_nt(q.astype(BF16), kb) * dec).astype(BF16) for q, kb, dec in zip(qs, kbs, decays)]
    a_mats = [jnp.where(strict, bc * kk * dec, 0.0) for bc, kk, dec in zip(b_cols, kks, decays)]
    t_invs = _unit_lower_inverses(a_mats, ii, jj)

    e_gs = [jnp.exp(g) for g in g_cols]
    rhss = [jnp.concatenate([v * bc, k * (bc * eg)], axis=1).astype(BF16)
            for v, k, bc, eg in zip(vs, ks, b_cols, e_gs)]
    w_alls = [_dot(t.astype(BF16), r) for t, r in zip(t_invs, rhss)]
    g_lasts = [g[chunk - 1:chunk, :] for g in g_cols]
    q_decs = [(q * eg).astype(BF16) for q, eg in zip(qs, e_gs)]
    k_dec_ts = [(k * jnp.exp(gl - g)).T.astype(BF16) for k, gl, g in zip(ks, g_lasts, g_cols)]

    s_olds = [state[b, hh] for b, hh in probs]
    s_bs = [s.astype(BF16) for s in s_olds]
    v_nbs = [(w[:, :dv] - _dot(w[:, dv:].astype(BF16), sb)).astype(BF16)
             for w, sb in zip(w_alls, s_bs)]
    outs = [_dot(qd, sb) + _dot(qk, vn) for qd, sb, qk, vn in zip(q_decs, s_bs, qks, v_nbs)]
    for (b, hh), s_old, gl, kdt, vn in zip(probs, s_olds, g_lasts, k_dec_ts, v_nbs):
        state[b, hh] = s_old * jnp.exp(gl) + _dot(kdt, vn)
    for (b, hh), o in zip(probs, outs):
        z = proj[b * chunk:(b + 1) * chunk, conv_w + hh * dv:conv_w + (hh + 1) * dv]
        o = o * _rms_scale(o, dv) * normw_ref[...] * _silu(z)
        o_ref[b, :, hh * dv:(hh + 1) * dv] = o.astype(o_ref.dtype)


def _gdn(h2, w_qkvz, w_ab, conv_w, a_log, dt_bias, norm_w):
    bsz, s, d = h2.shape
    chunk = min(GDN_CHUNK, s)
    nh, dv = GDN_HEADS, GDN_DV
    cw = conv_w.shape[1]
    full = lambda j: (0, 0)
    return pl.pallas_call(
        functools.partial(_gdn_kernel, chunk=chunk),
        grid=(s // chunk,),
        in_specs=[pl.BlockSpec((bsz, chunk, d), lambda j: (0, j, 0)),
                  pl.BlockSpec(w_qkvz.shape, full),
                  pl.BlockSpec(w_ab.shape, full),
                  pl.BlockSpec(conv_w.shape, full),
                  pl.BlockSpec((1, LANES), full),
                  pl.BlockSpec((1, LANES), full),
                  pl.BlockSpec((1, dv), full)],
        out_specs=pl.BlockSpec((bsz, chunk, nh * dv), lambda j: (0, j, 0)),
        out_shape=jax.ShapeDtypeStruct((bsz, s, nh * dv), BF16),
        scratch_shapes=[pltpu.VMEM((bsz, chunk + SUBLANES, cw), F32),
                        pltpu.VMEM((bsz, nh, GDN_DK, dv), F32)],
        compiler_params=pltpu.CompilerParams(
            dimension_semantics=("arbitrary",),
            vmem_limit_bytes=VMEM_LIMIT),
        name="gdn",
    )(h2, w_qkvz, w_ab, conv_w, a_log, dt_bias, norm_w)


def _mla_prep_kernel(h_ref, pos_ref, wm_ref, wuq_ref, wukv_ref, qnw_ref, kvnw_ref,
                     wqn_ref, wqr_ref, wqrs_ref, wkn_ref, wkr_ref, wkrs_ref,
                     invf_ref, sgn_ref, qt_ref, k_ref, vt_ref):
    nh = MLA_HEADS
    scale = (MLA_NOPE + MLA_ROPE) ** -0.5 * LOG2_E
    h = h_ref[0]
    lat = _dot(h, wm_ref[...])
    o1 = MLA_Q_LORA
    o2 = o1 + MLA_KV_LORA
    cq = lat[:, :o1]
    ckv = lat[:, o1:o2]
    kr = lat[:, o2:o2 + LANES]
    krs = lat[:, o2 + LANES:o2 + 2 * LANES]
    cqn = (cq * _rms_scale(cq, MLA_Q_LORA) * qnw_ref[...]).astype(BF16)
    ckvn = (ckv * _rms_scale(ckv, MLA_KV_LORA) * kvnw_ref[...]).astype(BF16)
    qf = _dot(cqn, wuq_ref[...])
    kvf = _dot(ckvn, wukv_ref[...])

    ang = pos_ref[0].astype(F32) * invf_ref[...]
    cos_p = jnp.cos(ang)
    sin_p = jnp.sin(ang) * sgn_ref[...]

    k_rope = (kr * wkr_ref[...] * cos_p + krs * wkrs_ref[...] * sin_p) * _rms_scale(kr, MLA_ROPE)
    qw = MLA_NOPE + 2 * LANES
    kw = MLA_NOPE + MLA_V
    for hh in range(nh):
        qn = qf[:, hh * qw:hh * qw + MLA_NOPE]
        qr = qf[:, hh * qw + MLA_NOPE:hh * qw + MLA_NOPE + LANES]
        qrs = qf[:, hh * qw + MLA_NOPE + LANES:(hh + 1) * qw]
        qn = qn * _rms_scale(qn, MLA_NOPE) * (wqn_ref[...] * scale)
        q_rope = ((qr * wqr_ref[...] * cos_p + qrs * wqrs_ref[...] * sin_p)
                  * (_rms_scale(qr, MLA_ROPE) * scale))
        qt_ref[0, hh] = jnp.concatenate([qn, q_rope], axis=1).T.astype(BF16)
        kn = kvf[:, hh * kw:hh * kw + MLA_NOPE]
        v = kvf[:, hh * kw + MLA_NOPE:(hh + 1) * kw]
        kn = kn * _rms_scale(kn, MLA_NOPE) * wkn_ref[...]
        k_ref[0, hh] = jnp.concatenate([kn, k_rope], axis=1).astype(BF16)
        vt_ref[0, hh] = v.T.astype(BF16)


def _mla_prep(h2, pos3, w_m, w_uq, w_ukv, vecs, *, tm=512):
    bsz, s, d = h2.shape
    tm = min(tm, s)
    nh = MLA_HEADS
    qk_dim = MLA_NOPE + LANES
    full = lambda b, i: (0, 0)
    vec_specs = [pl.BlockSpec(v.shape, full) for v in vecs]
    return pl.pallas_call(
        _mla_prep_kernel,
        grid=(bsz, s // tm),
        in_specs=[pl.BlockSpec((1, tm, d), lambda b, i: (b, i, 0)),
                  pl.BlockSpec((1, tm, 1), lambda b, i: (b, i, 0)),
                  pl.BlockSpec(w_m.shape, full),
                  pl.BlockSpec(w_uq.shape, full),
                  pl.BlockSpec(w_ukv.shape, full)] + vec_specs,
        out_specs=[pl.BlockSpec((1, nh, qk_dim, tm), lambda b, i: (b, 0, 0, i)),
                   pl.BlockSpec((1, nh, tm, qk_dim), lambda b, i: (b, 0, i, 0)),
                   pl.BlockSpec((1, nh, MLA_V, tm), lambda b, i: (b, 0, 0, i))],
        out_shape=[jax.ShapeDtypeStruct((bsz, nh, qk_dim, s), BF16),
                   jax.ShapeDtypeStruct((bsz, nh, s, qk_dim), BF16),
                   jax.ShapeDtypeStruct((bsz, nh, MLA_V, s), BF16)],
        compiler_params=pltpu.CompilerParams(
            dimension_semantics=("parallel", "parallel"),
            vmem_limit_bytes=VMEM_LIMIT),
        name="mla_prep",
    )(h2, pos3, w_m, w_uq, w_ukv, *vecs)


def _attn_kernel(qt_ref, k_ref, vt_ref, nw_ref, o_ref, s_scr, m_scr, l_scr, acc_scr, *,
                 tq, tk):
    i = pl.program_id(2)
    per_q = tq // tk

    def scores(hh, j):
        start = pl.multiple_of(j * tk, tk)
        return _dot(k_ref[0, hh, pl.ds(start, tk), :], qt_ref[0, hh])

    def accumulate(hh, j, visible=None):
        if visible is not None:
            s_scr[hh] = jnp.where(visible, s_scr[hh], -jnp.inf)
        m = m_scr[hh]
        m_new = jnp.maximum(m, jnp.max(s_scr[hh], axis=0, keepdims=True))
        alpha = jnp.exp2(m - m_new)
        p = jnp.exp2(s_scr[hh] - m_new)
        m_scr[hh] = m_new
        l_scr[hh] = alpha * l_scr[hh] + jnp.sum(p, axis=0, keepdims=True)
        start = pl.multiple_of(j * tk, tk)
        vj = vt_ref[0, hh, :, pl.ds(start, tk)]
        acc_scr[hh] = alpha * acc_scr[hh] + _dot(vj, p.astype(BF16))

    m_scr[...] = jnp.full(m_scr.shape, -jnp.inf, F32)
    l_scr[...] = jnp.zeros(l_scr.shape, F32)
    acc_scr[...] = jnp.zeros(acc_scr.shape, F32)
    s_scr[0] = scores(0, 0)

    def body(j, _):
        s_scr[1] = scores(1, j)
        accumulate(0, j)
        s_scr[0] = scores(0, j + 1)
        accumulate(1, j)
        return 0

    n_full = i * per_q
    lax.fori_loop(0, n_full, body, 0)

    kc = lax.broadcasted_iota(jnp.int32, (tk, tq), 0) // ATTN_CHUNK
    qc = lax.broadcasted_iota(jnp.int32, (tk, tq), 1) // ATTN_CHUNK
    for r in range(per_q):
        visible = qc >= kc + (r * tk) // ATTN_CHUNK
        j = n_full + r
        s_scr[1] = scores(1, j)
        accumulate(0, j, visible)
        if r < per_q - 1:
            s_scr[0] = scores(0, j + 1)
        accumulate(1, j, visible)
    for hh in range(2):
        o = (acc_scr[hh] / l_scr[hh]).T
        o = o * _rms_scale(o, MLA_V) * nw_ref[...]
        o_ref[0, :, hh * MLA_V:(hh + 1) * MLA_V] = o.astype(o_ref.dtype)


def _attn(qt, k, vt, out_norm_w, *, tq=1024, tk=512):
    bsz, nh, qk_dim, s = qt.shape
    tq = min(tq, s)
    tk = min(tk, tq)
    heads = 2
    return pl.pallas_call(
        functools.partial(_attn_kernel, tq=tq, tk=tk),
        grid=(bsz, nh // heads, s // tq),
        in_specs=[pl.BlockSpec((1, heads, qk_dim, tq), lambda b, h, i: (b, h, 0, i)),
                  pl.BlockSpec((1, heads, s, qk_dim), lambda b, h, i: (b, h, 0, 0)),
                  pl.BlockSpec((1, heads, MLA_V, s), lambda b, h, i: (b, h, 0, 0)),
                  pl.BlockSpec((1, MLA_V), lambda b, h, i: (0, 0))],
        out_specs=pl.BlockSpec((1, tq, heads * MLA_V), lambda b, h, i: (b, i, h)),
        out_shape=jax.ShapeDtypeStruct((bsz, s, nh * MLA_V), BF16),
        scratch_shapes=[pltpu.VMEM((heads, tk, tq), F32),
                        pltpu.VMEM((heads, 1, tq), F32),
                        pltpu.VMEM((heads, 1, tq), F32),
                        pltpu.VMEM((heads, MLA_V, tq), F32)],
        compiler_params=pltpu.CompilerParams(
            dimension_semantics=("parallel", "parallel", "arbitrary"),
            vmem_limit_bytes=VMEM_LIMIT),
        name="attn",
    )(qt, k, vt, out_norm_w)


def _pad_lanes(w, width=LANES):
    return jnp.pad(w, [(0, 0)] * (w.ndim - 1) + [(0, width - w.shape[-1])])


def _swap_halves(w):
    half = w.shape[-1] // 2
    return jnp.concatenate([w[..., half:], w[..., :half]], axis=-1)


def _layer(x, mods, pos3, w_ffn1_in, w_ffn1_out, w_in, conv_w, a_log, dt_bias, gdn_norm_w,
           q_norm_w, w_uq, kv_norm_w, w_ukv, qn_q_nope, qn_q_rope, qn_k_nope, qn_k_rope,
           out_norm_w, w_out, w_ffn2_in, w_ffn2_out):
    sh1, s1, g1, sh2, s2, g2, sh3, s3, g3 = mods
    nh = MLA_HEADS

    o_gz = 2 * GDN_HEADS * GDN_DK + 2 * GDN_HEADS * GDN_DV
    o_ab = o_gz + 2 * GDN_HEADS
    o_cq = o_ab + MLA_Q_LORA
    o_ckv = o_cq + MLA_KV_LORA
    w_qkvz = w_in[:, :o_gz].astype(BF16)
    w_ab = _pad_lanes(w_in[:, o_gz:o_ab]).astype(BF16)
    w_kr = w_in[:, o_ckv:]
    w_m = jnp.concatenate([w_in[:, o_ab:o_ckv], _pad_lanes(w_kr), _pad_lanes(_swap_halves(w_kr))],
                          axis=1).astype(BF16)
    per_q = MLA_NOPE + MLA_ROPE
    uq_parts = []
    for hh in range(nh):
        wn = w_uq[:, hh * per_q:hh * per_q + MLA_NOPE]
        wr = w_uq[:, hh * per_q + MLA_NOPE:(hh + 1) * per_q]
        uq_parts += [wn, _pad_lanes(wr), _pad_lanes(_swap_halves(wr))]
    w_uq_p = jnp.concatenate(uq_parts, axis=1).astype(BF16)
    w_ukv_b = w_ukv.astype(BF16)

    half = MLA_ROPE // 2
    inv_freq = ROPE_BASE ** (-jnp.arange(half, dtype=F32) / half)
    invf = _pad_lanes(jnp.concatenate([inv_freq, inv_freq])).reshape(1, LANES)
    sgn = _pad_lanes(jnp.concatenate([-jnp.ones((half,), F32), jnp.ones((half,), F32)])).reshape(1, LANES)
    row = lambda v: v.reshape(1, -1)
    vecs = [row(q_norm_w), row(kv_norm_w),
            row(qn_q_nope), row(_pad_lanes(qn_q_rope)), row(_pad_lanes(_swap_halves(qn_q_rope))),
            row(qn_k_nope), row(_pad_lanes(qn_k_rope)), row(_pad_lanes(_swap_halves(qn_k_rope))),
            invf, sgn]

    x1, h2 = _ffn(x, sh1, s1, g1, w_ffn1_in.astype(BF16), w_ffn1_out.astype(BF16), nxt=(sh2, s2))

    o_a = _gdn(h2, w_qkvz, w_ab, conv_w, row(_pad_lanes(a_log)), row(_pad_lanes(dt_bias)),
               row(gdn_norm_w))
    qt, k, vt = _mla_prep(h2, pos3, w_m, w_uq_p, w_ukv_b, vecs)
    o_b = _attn(qt, k, vt, row(out_norm_w))

    return _ffn(x1, sh3, s3, g3, w_ffn2_in.astype(BF16), w_ffn2_out.astype(BF16),
                mix=(o_a, o_b, w_out.astype(BF16), g2))


def kernel(x, c, positions, w_ada, b_ada, ffn1_w_in, ffn1_w_out, w_in, gdn_conv_w, gdn_a_log, gdn_dt_bias, gdn_norm_w, mla_q_norm_w, mla_w_uq, mla_kv_norm_w, mla_w_ukv, qkn_q_nope, qkn_q_rope, qkn_k_nope, qkn_k_rope, mla_out_norm_w, w_out, ffn2_w_in, ffn2_w_out):
    bsz, s, d = x.shape
    pos3 = positions.reshape(bsz, s, 1)
    for l in range(w_ada.shape[0]):
        mod = _mod(c, w_ada[l], b_ada[l])
        mods = [m.reshape(bsz, 1, d) for m in jnp.split(mod, N_MOD, axis=-1)]
        x = _layer(x, mods, pos3, ffn1_w_in[l], ffn1_w_out[l], w_in[l], gdn_conv_w[l],
                   gdn_a_log[l], gdn_dt_bias[l], gdn_norm_w[l], mla_q_norm_w[l], mla_w_uq[l],
                   mla_kv_norm_w[l], mla_w_ukv[l], qkn_q_nope[l], qkn_q_rope[l], qkn_k_nope[l],
                   qkn_k_rope[l], mla_out_norm_w[l], w_out[l], ffn2_w_in[l], ffn2_w_out[l])
    return x
```

```python
import functools

import jax
import jax.numpy as jnp
from jax import lax
from jax.experimental import pallas as pl
from jax.experimental.pallas import tpu as pltpu

F32 = jnp.float32
BF16 = jnp.bfloat16

EPS = 1e-6
ATTN_CHUNK = 64
GDN_HEADS = 4
GDN_DK = 128
GDN_DV = 128
CONV_K = 4
MLA_HEADS = 4
MLA_NOPE = 128
MLA_ROPE = 64
MLA_V = 128
MLA_Q_LORA = 384
MLA_KV_LORA = 256
ROPE_BASE = 10000.0
N_MOD = 9
LOG2_E = 1.4426950408889634

LANES = 128
SUBLANES = 8
VMEM_LIMIT = 56 * 1024 * 1024

GDN_CHUNK = 128


def _dot(a, b):
    return jnp.dot(a, b, preferred_element_type=F32)


def _dot_nt(a, b):
    return lax.dot_general(a, b, (((1,), (1,)), ((), ())), preferred_element_type=F32)


def _dot_tn(a, b):
    return lax.dot_general(a, b, (((0,), (0,)), ((), ())), preferred_element_type=F32)


def _silu(x):
    return x * jax.nn.sigmoid(x)


def _rms_scale(x, n):
    return lax.rsqrt(jnp.sum(x * x, axis=-1, keepdims=True) * (1.0 / n) + EPS)


def _mod_kernel(c_ref, w_ref, b_ref, o_ref):
    sc = _silu(c_ref[...])
    o_ref[...] = jnp.dot(sc, w_ref[...], precision=lax.Precision.HIGHEST,
                         preferred_element_type=F32) + b_ref[...]


def _mod(c, w_ada, b_ada):
    bsz, d = c.shape
    n = w_ada.shape[1]
    tn = 1024
    return pl.pallas_call(
        _mod_kernel,
        grid=(n // tn,),
        in_specs=[pl.BlockSpec((bsz, d), lambda j: (0, 0)),
                  pl.BlockSpec((d, tn), lambda j: (0, j)),
                  pl.BlockSpec((1, tn), lambda j: (0, j))],
        out_specs=pl.BlockSpec((bsz, tn), lambda j: (0, j)),
        out_shape=jax.ShapeDtypeStruct((bsz, n), F32),
        compiler_params=pltpu.CompilerParams(dimension_semantics=("arbitrary",),
                                             vmem_limit_bytes=VMEM_LIMIT),
        name="mod",
    )(c, w_ada, b_ada.reshape(1, n))


def _ffn_kernel(*refs, has_mix, emit_next, d_model, d_ff, tf):
    it = iter(refs)
    x_ref = next(it)
    if has_mix:
        oa_ref, ob_ref, wmix_ref, gmix_ref = next(it), next(it), next(it), next(it)
    shift_ref, scale_ref, gate_ref = next(it), next(it), next(it)
    win_ref, wout_ref = next(it), next(it)
    if emit_next:
        nshift_ref, nscale_ref = next(it), next(it)
    out_ref = next(it)
    if emit_next:
        hn_ref = next(it)
    a_sc = next(it)

    x = x_ref[0]
    if has_mix:
        half = oa_ref.shape[-1]
        y = _dot(oa_ref[0], wmix_ref[:half, :]) + _dot(ob_ref[0], wmix_ref[half:, :])
        x = x + gmix_ref[0] * y
    h = (x * _rms_scale(x, d_model) * (1.0 + scale_ref[0]) + shift_ref[0]).astype(BF16)
    for c in range(d_ff // tf):
        g = _dot(h, win_ref[:, c * tf:(c + 1) * tf])
        u = _dot(h, win_ref[:, d_ff + c * tf:d_ff + (c + 1) * tf])
        a_sc[:, c * tf:(c + 1) * tf] = (_silu(g) * u).astype(BF16)
    out = x + 0.5 * gate_ref[0] * _dot(a_sc[...], wout_ref[...])
    out_ref[0] = out
    if emit_next:
        hn = out * _rms_scale(out, d_model) * (1.0 + nscale_ref[0]) + nshift_ref[0]
        hn_ref[0] = hn.astype(BF16)


def _resident(shape, index_map):
    return pl.BlockSpec(shape, index_map, pipeline_mode=pl.Buffered(1))


def _ffn(x, shift, scale, gate, w_in_bf, w_out_bf, *, mix=None, nxt=None, tm=512, tf=256):
    bsz, s, d = x.shape
    ff = w_out_bf.shape[0]
    tm = min(tm, s)
    assert ff % tf == 0 and s % tm == 0
    has_mix = mix is not None
    emit_next = nxt is not None

    row = lambda b, i: (b, i, 0)
    vec = lambda b, i: (b, 0, 0)
    const = lambda b, i: (0, 0)
    args = [x]
    in_specs = [pl.BlockSpec((1, tm, d), row)]
    if has_mix:
        o_a, o_b, w_mix, g_mix = mix
        args += [o_a, o_b, w_mix, g_mix]
        in_specs += [pl.BlockSpec((1, tm, o_a.shape[-1]), row),
                     pl.BlockSpec((1, tm, o_b.shape[-1]), row),
                     _resident(w_mix.shape, const),
                     pl.BlockSpec((1, 1, d), vec)]
    args += [shift, scale, gate, w_in_bf, w_out_bf]
    in_specs += [pl.BlockSpec((1, 1, d), vec)] * 3
    in_specs += [_resident(w_in_bf.shape, const), _resident(w_out_bf.shape, const)]
    if emit_next:
        args += list(nxt)
        in_specs += [pl.BlockSpec((1, 1, d), vec)] * 2
    out_shape = [jax.ShapeDtypeStruct((bsz, s, d), F32)]
    out_specs = [pl.BlockSpec((1, tm, d), row)]
    if emit_next:
        out_shape.append(jax.ShapeDtypeStruct((bsz, s, d), BF16))
        out_specs.append(pl.BlockSpec((1, tm, d), row))

    res = pl.pallas_call(
        functools.partial(_ffn_kernel, has_mix=has_mix, emit_next=emit_next, d_model=d,
                          d_ff=ff, tf=tf),
        grid=(bsz, s // tm),
        in_specs=in_specs,
        out_specs=out_specs,
        out_shape=out_shape,
        scratch_shapes=[pltpu.VMEM((tm, ff), BF16)],
        compiler_params=pltpu.CompilerParams(
            dimension_semantics=("parallel", "parallel"),
            vmem_limit_bytes=VMEM_LIMIT),
        name="ffn_mix" if has_mix else "ffn",
    )(*args)
    return res if emit_next else res[0]


def _unit_lower_inverses(a_mats, ii, jj):
    n = a_mats[0].shape[0]
    eye = (ii == jj).astype(F32)

    def same_block(size):
        return (ii // size) == (jj // size)

    base = same_block(SUBLANES)
    d1 = [jnp.where(base, a, 0.0) for a in a_mats]
    d1b = [d.astype(BF16) for d in d1]
    d2 = [_dot(d, d) for d in d1b]
    d2b = [d.astype(BF16) for d in d2]
    d3 = [_dot(a, b) for a, b in zip(d1b, d2b)]
    d4b = [_dot(d, d).astype(BF16) for d in d2b]
    p1 = [eye - a + b - c for a, b, c in zip(d1, d2, d3)]
    ts = [p + _dot(p.astype(BF16), d) for p, d in zip(p1, d4b)]
    size = SUBLANES
    while size < n:
        below = same_block(2 * size) & jnp.logical_not(same_block(size))
        offs = [jnp.where(below, a, 0.0).astype(BF16) for a in a_mats]
        tbs = [t.astype(BF16) for t in ts]
        xs = [_dot(o, t).astype(BF16) for o, t in zip(offs, tbs)]
        ts = [t - _dot(tb, x) for t, tb, x in zip(ts, tbs, xs)]
        size *= 2
    return ts


def _split3(x):
    hi = x.astype(BF16)
    r1 = x - hi.astype(F32)
    mid = r1.astype(BF16)
    lo = (r1 - mid.astype(F32)).astype(BF16)
    return hi, mid, lo


def _gdn_kernel(h_ref, wqkvz_ref, wab_ref, convw_ref, alog_ref, dtb_ref, normw_ref,
                o_ref, xbuf, state, *, chunk):
    nb, _, d = h_ref.shape
    nh, dk, dv = GDN_HEADS, GDN_DK, GDN_DV
    qk_w = nh * dk
    conv_w = 2 * qk_w + nh * dv
    pad = SUBLANES
    probs = [(b, hh) for b in range(nb) for hh in range(nh)]

    @pl.when(pl.program_id(0) == 0)
    def _():
        state[...] = jnp.zeros_like(state)
        xbuf[:, 0:pad, :] = jnp.zeros((nb, pad, conv_w), F32)

    h = h_ref[...].reshape(nb * chunk, d)
    proj = _dot(h, wqkvz_ref[...])
    ab = _dot(h, wab_ref[...])
    g_all = -jnp.exp(alog_ref[...]) * jax.nn.softplus(ab + dtb_ref[...])
    beta_all = jax.nn.sigmoid(ab)

    ii = lax.broadcasted_iota(jnp.int32, (chunk, chunk), 0)
    jj = lax.broadcasted_iota(jnp.int32, (chunk, chunk), 1)
    incl = ii >= jj
    strict = ii > jj
    tri = incl.astype(BF16)
    cw = convw_ref[...]

    qkv, gcum, gcum_t = [], [], []
    for b in range(nb):
        rows = slice(b * chunk, (b + 1) * chunk)
        xbuf[b, pad:pad + chunk, :] = proj[rows, :conv_w]
        conv = xbuf[b, pad:pad + chunk, :] * cw[CONV_K - 1:CONV_K, :]
        for tap in range(1, CONV_K):
            conv = conv + (xbuf[b, pad - tap:pad - tap + chunk, :]
                           * cw[CONV_K - 1 - tap:CONV_K - tap, :])
        xbuf[b, 0:pad, :] = xbuf[b, chunk:chunk + pad, :]
        qkv.append(_silu(conv))
        g_hi, g_mid, g_lo = _split3(g_all[rows, :])
        gc = _dot(tri, g_hi) + _dot(tri, g_mid) + _dot(tri, g_lo)
        gcum.append(gc)
        gcum_t.append(gc.T)

    qs, ks, vs, kbs, g_cols, b_cols, decays = [], [], [], [], [], [], []
    for b, hh in probs:
        q = qkv[b][:, hh * dk:(hh + 1) * dk]
        k = qkv[b][:, qk_w + hh * dk:qk_w + (hh + 1) * dk]
        vs.append(qkv[b][:, 2 * qk_w + hh * dv:2 * qk_w + (hh + 1) * dv])
        q = q * (lax.rsqrt(jnp.sum(q * q, axis=-1, keepdims=True) + EPS) * (dk ** -0.5))
        k = k * lax.rsqrt(jnp.sum(k * k, axis=-1, keepdims=True) + EPS)
        qs.append(q)
        ks.append(k)
        kbs.append(k.astype(BF16))
        g_col = gcum[b][:, hh:hh + 1]
        g_row = gcum_t[b][hh:hh + 1, :]
        g_cols.append(g_col)
        b_cols.append(beta_all[b * chunk:(b + 1) * chunk, nh + hh:nh + hh + 1])
        decays.append(jnp.exp(jnp.where(incl, g_col - g_row, -jnp.inf)))

    kks = [_dot_nt(kb, kb) for kb in kbs]
    qks = [(_dot_nt(q.astype(BF16), kb) * dec).astype(BF16) for q, kb, dec in zip(qs, kbs, decays)]
    a_mats = [jnp.where(strict, bc * kk * dec, 0.0) for bc, kk, dec in zip(b_cols, kks, decays)]
    t_invs = _unit_lower_inverses(a_mats, ii, jj)

    e_gs = [jnp.exp(g) for g in g_cols]
    rhss = [jnp.concatenate([v * bc, k * (bc * eg)], axis=1).astype(BF16)
            for v, k, bc, eg in zip(vs, ks, b_cols, e_gs)]
    w_alls = [_dot(t.astype(BF16), r) for t, r in zip(t_invs, rhss)]
    g_lasts = [g[chunk - 1:chunk, :] for g in g_cols]
    q_decs = [(q * eg).astype(BF16) for q, eg in zip(qs, e_gs)]
    k_dec_ts = [(k * jnp.exp(gl - g)).T.astype(BF16) for k, gl, g in zip(ks, g_lasts, g_cols)]

    s_olds = [state[b, hh] for b, hh in probs]
    s_bs = [s.astype(BF16) for s in s_olds]
    v_nbs = [(w[:, :dv] - _dot(w[:, dv:].astype(BF16), sb)).astype(BF16)
             for w, sb in zip(w_alls, s_bs)]
    outs = [_dot(qd, sb) + _dot(qk, vn) for qd, sb, qk, vn in zip(q_decs, s_bs, qks, v_nbs)]
    for (b, hh), s_old, gl, kdt, vn in zip(probs, s_olds, g_lasts, k_dec_ts, v_nbs):
        state[b, hh] = s_old * jnp.exp(gl) + _dot(kdt, vn)
    for (b, hh), o in zip(probs, outs):
        z = proj[b * chunk:(b + 1) * chunk, conv_w + hh * dv:conv_w + (hh + 1) * dv]
        o = o * _rms_scale(o, dv) * normw_ref[...] * _silu(z)
        o_ref[b, :, hh * dv:(hh + 1) * dv] = o.astype(o_ref.dtype)


def _gdn(h2, w_qkvz, w_ab, conv_w, a_log, dt_bias, norm_w):
    bsz, s, d = h2.shape
    chunk = min(GDN_CHUNK, s)
    nh, dv = GDN_HEADS, GDN_DV
    cw = conv_w.shape[1]
    full = lambda j: (0, 0)
    return pl.pallas_call(
        functools.partial(_gdn_kernel, chunk=chunk),
        grid=(s // chunk,),
        in_specs=[pl.BlockSpec((bsz, chunk, d), lambda j: (0, j, 0)),
                  pl.BlockSpec(w_qkvz.shape, full),
                  pl.BlockSpec(w_ab.shape, full),
                  pl.BlockSpec(conv_w.shape, full),
                  pl.BlockSpec((1, LANES), full),
                  pl.BlockSpec((1, LANES), full),
                  pl.BlockSpec((1, dv), full)],
        out_specs=pl.BlockSpec((bsz, chunk, nh * dv), lambda j: (0, j, 0)),
        out_shape=jax.ShapeDtypeStruct((bsz, s, nh * dv), BF16),
        scratch_shapes=[pltpu.VMEM((bsz, chunk + SUBLANES, cw), F32),
                        pltpu.VMEM((bsz, nh, GDN_DK, dv), F32)],
        compiler_params=pltpu.CompilerParams(
            dimension_semantics=("arbitrary",),
            vmem_limit_bytes=VMEM_LIMIT),
        name="gdn",
    )(h2, w_qkvz, w_ab, conv_w, a_log, dt_bias, norm_w)


def _mla_prep_kernel(h_ref, pos_ref, wm_ref, wuq_ref, wukv_ref, qnw_ref, kvnw_ref,
                     wqn_ref, wqr_ref, wqrs_ref, wkn_ref, wkr_ref, wkrs_ref,
                     invf_ref, csel_ref, ssel_ref, qt_ref, k_ref, vt_ref):
    nh = MLA_HEADS
    scale = (MLA_NOPE + MLA_ROPE) ** -0.5 * LOG2_E
    h = h_ref[0]
    lat = _dot(h, wm_ref[...])
    o1 = MLA_Q_LORA
    o2 = o1 + MLA_KV_LORA
    cq = lat[:, :o1]
    ckv = lat[:, o1:o2]
    kr = lat[:, o2:o2 + LANES]
    krs = lat[:, o2 + LANES:o2 + 2 * LANES]
    cqn = (cq * _rms_scale(cq, MLA_Q_LORA) * qnw_ref[...]).astype(BF16)
    ckvn = (ckv * _rms_scale(ckv, MLA_KV_LORA) * kvnw_ref[...]).astype(BF16)
    qf = _dot(cqn, wuq_ref[...])
    kvf = _dot(ckvn, wukv_ref[...])

    ang_t = invf_ref[...] * pos_ref[0].astype(F32)

    def spread(table_t, sel_ref):
        parts = [_dot_tn(part, sel_ref[...]) for part in _split3(table_t)]
        return (parts[0] + parts[1]) + parts[2]

    cos_p = spread(jnp.cos(ang_t), csel_ref)
    sin_p = spread(jnp.sin(ang_t), ssel_ref)

    k_rope = (kr * wkr_ref[...] * cos_p + krs * wkrs_ref[...] * sin_p) * _rms_scale(kr, MLA_ROPE)
    qw = MLA_NOPE + 2 * LANES
    kw = MLA_NOPE + MLA_V
    for hh in range(nh):
        qn = qf[:, hh * qw:hh * qw + MLA_NOPE]
        qr = qf[:, hh * qw + MLA_NOPE:hh * qw + MLA_NOPE + LANES]
        qrs = qf[:, hh * qw + MLA_NOPE + LANES:(hh + 1) * qw]
        qn = qn * _rms_scale(qn, MLA_NOPE) * (wqn_ref[...] * scale)
        q_rope = ((qr * wqr_ref[...] * cos_p + qrs * wqrs_ref[...] * sin_p)
                  * (_rms_scale(qr, MLA_ROPE) * scale))
        qt_ref[0, hh] = jnp.concatenate([qn, q_rope], axis=1).T.astype(BF16)
        kn = kvf[:, hh * kw:hh * kw + MLA_NOPE]
        v = kvf[:, hh * kw + MLA_NOPE:(hh + 1) * kw]
        kn = kn * _rms_scale(kn, MLA_NOPE) * wkn_ref[...]
        k_ref[0, hh] = jnp.concatenate([kn, k_rope], axis=1).astype(BF16)
        vt_ref[0, hh] = v.T.astype(BF16)


def _mla_prep(h2, pos3, w_m, w_uq, w_ukv, vecs, *, tm=512):
    bsz, s, d = h2.shape
    tm = min(tm, s)
    nh = MLA_HEADS
    qk_dim = MLA_NOPE + LANES
    full = lambda b, i: (0, 0)
    vec_specs = [pl.BlockSpec(v.shape, full) for v in vecs]
    return pl.pallas_call(
        _mla_prep_kernel,
        grid=(bsz, s // tm),
        in_specs=[pl.BlockSpec((1, tm, d), lambda b, i: (b, i, 0)),
                  pl.BlockSpec((1, 1, tm), lambda b, i: (b, 0, i)),
                  pl.BlockSpec(w_m.shape, full),
                  pl.BlockSpec(w_uq.shape, full),
                  pl.BlockSpec(w_ukv.shape, full)] + vec_specs,
        out_specs=[pl.BlockSpec((1, nh, qk_dim, tm), lambda b, i: (b, 0, 0, i)),
                   pl.BlockSpec((1, nh, tm, qk_dim), lambda b, i: (b, 0, i, 0)),
                   pl.BlockSpec((1, nh, MLA_V, tm), lambda b, i: (b, 0, 0, i))],
        out_shape=[jax.ShapeDtypeStruct((bsz, nh, qk_dim, s), BF16),
                   jax.ShapeDtypeStruct((bsz, nh, s, qk_dim), BF16),
                   jax.ShapeDtypeStruct((bsz, nh, MLA_V, s), BF16)],
        compiler_params=pltpu.CompilerParams(
            dimension_semantics=("parallel", "parallel"),
            vmem_limit_bytes=VMEM_LIMIT),
        name="mla_prep",
    )(h2, pos3, w_m, w_uq, w_ukv, *vecs)


def _attn_kernel(qt_ref, k_ref, vt_ref, nw_ref, o_ref, s_scr, m_scr, l_scr, acc_scr, *,
                 tq, tk):
    i = pl.program_id(2)
    per_q = tq // tk

    def scores(hh, j, q0=0):
        start = pl.multiple_of(j * tk, tk)
        s_scr[hh, :, :tq - q0] = _dot(k_ref[0, hh, pl.ds(start, tk), :], qt_ref[0, hh, :, q0:])

    def accumulate(hh, j, q0=0, visible=None):
        w = tq - q0
        if visible is not None:
            s_scr[hh, :, :w] = jnp.where(visible, s_scr[hh, :, :w], -jnp.inf)
        m = m_scr[hh, :, q0:]
        m_new = jnp.maximum(m, jnp.max(s_scr[hh, :, :w], axis=0, keepdims=True))
        alpha = jnp.exp2(m - m_new)
        p = jnp.exp2(s_scr[hh, :, :w] - m_new)
        m_scr[hh, :, q0:] = m_new
        l_scr[hh, :, q0:] = alpha * l_scr[hh, :, q0:] + jnp.sum(p, axis=0, keepdims=True)
        start = pl.multiple_of(j * tk, tk)
        vj = vt_ref[0, hh, :, pl.ds(start, tk)]
        acc_scr[hh, :, q0:] = alpha * acc_scr[hh, :, q0:] + _dot(vj, p.astype(BF16))

    m_scr[...] = jnp.full(m_scr.shape, -jnp.inf, F32)
    l_scr[...] = jnp.zeros(l_scr.shape, F32)
    acc_scr[...] = jnp.zeros(acc_scr.shape, F32)
    scores(0, 0)

    def body(jj, _):
        for r in range(per_q):
            j = jj * per_q + r
            scores(1, j)
            accumulate(0, j)
            scores(0, j + 1)
            accumulate(1, j)
        return 0

    lax.fori_loop(0, i, body, 0)

    kc = lax.broadcasted_iota(jnp.int32, (tk, tq), 0) // ATTN_CHUNK
    qc = lax.broadcasted_iota(jnp.int32, (tk, tq), 1) // ATTN_CHUNK
    visible = qc >= kc
    for r in range(per_q):
        q0 = r * tk
        j = i * per_q + r
        scores(1, j, q0)
        accumulate(0, j, q0, visible[:, :tq - q0])
        if r < per_q - 1:
            scores(0, j + 1, q0 + tk)
        accumulate(1, j, q0, visible[:, :tq - q0])
    for hh in range(2):
        o = (acc_scr[hh] / l_scr[hh]).T
        o = o * _rms_scale(o, MLA_V) * nw_ref[...]
        o_ref[0, :, hh * MLA_V:(hh + 1) * MLA_V] = o.astype(o_ref.dtype)


def _attn(qt, k, vt, out_norm_w, *, tq=1024, tk=512):
    bsz, nh, qk_dim, s = qt.shape
    tq = min(tq, s)
    tk = min(tk, tq)
    heads = 2
    return pl.pallas_call(
        functools.partial(_attn_kernel, tq=tq, tk=tk),
        grid=(bsz, nh // heads, s // tq),
        in_specs=[pl.BlockSpec((1, heads, qk_dim, tq), lambda b, h, i: (b, h, 0, i)),
                  pl.BlockSpec((1, heads, s, qk_dim), lambda b, h, i: (b, h, 0, 0)),
                  pl.BlockSpec((1, heads, MLA_V, s), lambda b, h, i: (b, h, 0, 0)),
                  pl.BlockSpec((1, MLA_V), lambda b, h, i: (0, 0))],
        out_specs=pl.BlockSpec((1, tq, heads * MLA_V), lambda b, h, i: (b, i, h)),
        out_shape=jax.ShapeDtypeStruct((bsz, s, nh * MLA_V), BF16),
        scratch_shapes=[pltpu.VMEM((heads, tk, tq), F32),
                        pltpu.VMEM((heads, 1, tq), F32),
                        pltpu.VMEM((heads, 1, tq), F32),
                        pltpu.VMEM((heads, MLA_V, tq), F32)],
        compiler_params=pltpu.CompilerParams(
            dimension_semantics=("parallel", "parallel", "arbitrary"),
            vmem_limit_bytes=VMEM_LIMIT),
        name="attn",
    )(qt, k, vt, out_norm_w)


def _pad_lanes(w, width=LANES):
    return jnp.pad(w, [(0, 0)] * (w.ndim - 1) + [(0, width - w.shape[-1])])


def _swap_halves(w):
    half = w.shape[-1] // 2
    return jnp.concatenate([w[..., half:], w[..., :half]], axis=-1)


def _layer(x, mods, pos3, w_ffn1_in, w_ffn1_out, w_in, conv_w, a_log, dt_bias, gdn_norm_w,
           q_norm_w, w_uq, kv_norm_w, w_ukv, qn_q_nope, qn_q_rope, qn_k_nope, qn_k_rope,
           out_norm_w, w_out, w_ffn2_in, w_ffn2_out):
    sh1, s1, g1, sh2, s2, g2, sh3, s3, g3 = mods
    nh = MLA_HEADS

    o_gz = 2 * GDN_HEADS * GDN_DK + 2 * GDN_HEADS * GDN_DV
    o_ab = o_gz + 2 * GDN_HEADS
    o_cq = o_ab + MLA_Q_LORA
    o_ckv = o_cq + MLA_KV_LORA
    w_qkvz = w_in[:, :o_gz].astype(BF16)
    w_ab = _pad_lanes(w_in[:, o_gz:o_ab]).astype(BF16)
    w_kr = w_in[:, o_ckv:]
    w_m = jnp.concatenate([w_in[:, o_ab:o_ckv], _pad_lanes(w_kr), _pad_lanes(_swap_halves(w_kr))],
                          axis=1).astype(BF16)
    per_q = MLA_NOPE + MLA_ROPE
    uq_parts = []
    for hh in range(nh):
        wn = w_uq[:, hh * per_q:hh * per_q + MLA_NOPE]
        wr = w_uq[:, hh * per_q + MLA_NOPE:(hh + 1) * per_q]
        uq_parts += [wn, _pad_lanes(wr), _pad_lanes(_swap_halves(wr))]
    w_uq_p = jnp.concatenate(uq_parts, axis=1).astype(BF16)
    w_ukv_b = w_ukv.astype(BF16)

    half = MLA_ROPE // 2
    inv_freq = ROPE_BASE ** (-jnp.arange(half, dtype=F32) / half)
    invf = inv_freq.reshape(half, 1)
    eye = jnp.eye(half, dtype=F32)
    cos_sel = _pad_lanes(jnp.concatenate([eye, eye], axis=1)).astype(BF16)
    sin_sel = _pad_lanes(jnp.concatenate([-eye, eye], axis=1)).astype(BF16)
    row = lambda v: v.reshape(1, -1)
    vecs = [row(q_norm_w), row(kv_norm_w),
            row(qn_q_nope), row(_pad_lanes(qn_q_rope)), row(_pad_lanes(_swap_halves(qn_q_rope))),
            row(qn_k_nope), row(_pad_lanes(qn_k_rope)), row(_pad_lanes(_swap_halves(qn_k_rope))),
            invf, cos_sel, sin_sel]

    x1, h2 = _ffn(x, sh1, s1, g1, w_ffn1_in.astype(BF16), w_ffn1_out.astype(BF16), nxt=(sh2, s2))

    o_a = _gdn(h2, w_qkvz, w_ab, conv_w, row(_pad_lanes(a_log)), row(_pad_lanes(dt_bias)),
               row(gdn_norm_w))
    qt, k, vt = _mla_prep(h2, pos3, w_m, w_uq_p, w_ukv_b, vecs)
    o_b = _attn(qt, k, vt, row(out_norm_w))

    return _ffn(x1, sh3, s3, g3, w_ffn2_in.astype(BF16), w_ffn2_out.astype(BF16),
                mix=(o_a, o_b, w_out.astype(BF16), g2))


def kernel(x, c, positions, w_ada, b_ada, ffn1_w_in, ffn1_w_out, w_in, gdn_conv_w, gdn_a_log, gdn_dt_bias, gdn_norm_w, mla_q_norm_w, mla_w_uq, mla_kv_norm_w, mla_w_ukv, qkn_q_nope, qkn_q_rope, qkn_k_nope, qkn_k_rope, mla_out_norm_w, w_out, ffn2_w_in, ffn2_w_out):
    bsz, s, d = x.shape
    pos3 = positions.reshape(bsz, 1, s)
    for l in range(w_ada.shape[0]):
        mod = _mod(c, w_ada[l], b_ada[l])
        mods = [m.reshape(bsz, 1, d) for m in jnp.split(mod, N_MOD, axis=-1)]
        x = _layer(x, mods, pos3, ffn1_w_in[l], ffn1_w_out[l], w_in[l], gdn_conv_w[l],
                   gdn_a_log[l], gdn_dt_bias[l], gdn_norm_w[l], mla_q_norm_w[l], mla_w_uq[l],
                   mla_kv_norm_w[l], mla_w_ukv[l], qkn_q_nope[l], qkn_q_rope[l], qkn_k_nope[l],
                   qkn_k_rope[l], mla_out_norm_w[l], w_out[l], ffn2_w_in[l], ffn2_w_out[l])
    return x
```

```python
import functools

import jax
import jax.numpy as jnp
from jax import lax
from jax.experimental import pallas as pl
from jax.experimental.pallas import tpu as pltpu

F32 = jnp.float32
BF16 = jnp.bfloat16

EPS = 1e-6
ATTN_CHUNK = 64
GDN_HEADS = 4
GDN_DK = 128
GDN_DV = 128
CONV_K = 4
MLA_HEADS = 4
MLA_NOPE = 128
MLA_ROPE = 64
MLA_V = 128
MLA_Q_LORA = 384
MLA_KV_LORA = 256
ROPE_BASE = 10000.0
N_MOD = 9
LOG2_E = 1.4426950408889634
MAX_FIXED_REFERENCE = 60.0
SCORE_BOUND_MARGIN = 1.0625

LANES = 128
SUBLANES = 8
VMEM_LIMIT = 56 * 1024 * 1024

GDN_CHUNK = 128


def _dot(a, b):
    return jnp.dot(a, b, preferred_element_type=F32)


def _dot_nt(a, b):
    return lax.dot_general(a, b, (((1,), (1,)), ((), ())), preferred_element_type=F32)


def _dot_tn(a, b):
    return lax.dot_general(a, b, (((0,), (0,)), ((), ())), preferred_element_type=F32)


def _silu(x):
    return x * jax.nn.sigmoid(x)


def _rms_scale(x, n):
    return lax.rsqrt(jnp.sum(x * x, axis=-1, keepdims=True) * (1.0 / n) + EPS)


def _mod_kernel(c_ref, w_ref, b_ref, o_ref):
    sc = _silu(c_ref[...])
    o_ref[...] = jnp.dot(sc, w_ref[...], precision=lax.Precision.HIGHEST,
                         preferred_element_type=F32) + b_ref[...]


def _mod(c, w_ada, b_ada):
    bsz, d = c.shape
    n = w_ada.shape[1]
    tn = 1024
    return pl.pallas_call(
        _mod_kernel,
        grid=(n // tn,),
        in_specs=[pl.BlockSpec((bsz, d), lambda j: (0, 0)),
                  pl.BlockSpec((d, tn), lambda j: (0, j)),
                  pl.BlockSpec((1, tn), lambda j: (0, j))],
        out_specs=pl.BlockSpec((bsz, tn), lambda j: (0, j)),
        out_shape=jax.ShapeDtypeStruct((bsz, n), F32),
        compiler_params=pltpu.CompilerParams(dimension_semantics=("arbitrary",),
                                             vmem_limit_bytes=VMEM_LIMIT),
        name="mod",
    )(c, w_ada, b_ada.reshape(1, n))


def _ffn_kernel(*refs, has_mix, emit_next, d_model, d_ff, tf):
    it = iter(refs)
    x_ref = next(it)
    if has_mix:
        oa_ref, ob_ref, wmix_ref, gmix_ref = next(it), next(it), next(it), next(it)
    shift_ref, scale_ref, gate_ref = next(it), next(it), next(it)
    win_ref, wout_ref = next(it), next(it)
    if emit_next:
        nshift_ref, nscale_ref = next(it), next(it)
    out_ref = next(it)
    if emit_next:
        hn_ref = next(it)
    a_sc = next(it)

    x = x_ref[0]
    if has_mix:
        half = oa_ref.shape[-1]
        y = _dot(oa_ref[0], wmix_ref[:half, :]) + _dot(ob_ref[0], wmix_ref[half:, :])
        x = x + gmix_ref[0] * y
    h = (x * _rms_scale(x, d_model) * (1.0 + scale_ref[0]) + shift_ref[0]).astype(BF16)
    for c in range(d_ff // tf):
        g = _dot(h, win_ref[:, c * tf:(c + 1) * tf])
        u = _dot(h, win_ref[:, d_ff + c * tf:d_ff + (c + 1) * tf])
        a_sc[:, c * tf:(c + 1) * tf] = (_silu(g) * u).astype(BF16)
    out = x + 0.5 * gate_ref[0] * _dot(a_sc[...], wout_ref[...])
    out_ref[0] = out
    if emit_next:
        hn = out * _rms_scale(out, d_model) * (1.0 + nscale_ref[0]) + nshift_ref[0]
        hn_ref[0] = hn.astype(BF16)


def _resident(shape, index_map):
    return pl.BlockSpec(shape, index_map, pipeline_mode=pl.Buffered(1))


def _ffn(x, shift, scale, gate, w_in_bf, w_out_bf, *, mix=None, nxt=None, tm=1024, tf=256):
    bsz, s, d = x.shape
    ff = w_out_bf.shape[0]
    tm = min(tm, s)
    assert ff % tf == 0 and s % tm == 0
    has_mix = mix is not None
    emit_next = nxt is not None

    row = lambda b, i: (b, i, 0)
    vec = lambda b, i: (b, 0, 0)
    const = lambda b, i: (0, 0)
    args = [x]
    in_specs = [pl.BlockSpec((1, tm, d), row)]
    if has_mix:
        o_a, o_b, w_mix, g_mix = mix
        args += [o_a, o_b, w_mix, g_mix]
        in_specs += [pl.BlockSpec((1, tm, o_a.shape[-1]), row),
                     pl.BlockSpec((1, tm, o_b.shape[-1]), row),
                     _resident(w_mix.shape, const),
                     pl.BlockSpec((1, 1, d), vec)]
    args += [shift, scale, gate, w_in_bf, w_out_bf]
    in_specs += [pl.BlockSpec((1, 1, d), vec)] * 3
    in_specs += [_resident(w_in_bf.shape, const), _resident(w_out_bf.shape, const)]
    if emit_next:
        args += list(nxt)
        in_specs += [pl.BlockSpec((1, 1, d), vec)] * 2
    out_shape = [jax.ShapeDtypeStruct((bsz, s, d), F32)]
    out_specs = [pl.BlockSpec((1, tm, d), row)]
    if emit_next:
        out_shape.append(jax.ShapeDtypeStruct((bsz, s, d), BF16))
        out_specs.append(pl.BlockSpec((1, tm, d), row))

    res = pl.pallas_call(
        functools.partial(_ffn_kernel, has_mix=has_mix, emit_next=emit_next, d_model=d,
                          d_ff=ff, tf=tf),
        grid=(bsz, s // tm),
        in_specs=in_specs,
        out_specs=out_specs,
        out_shape=out_shape,
        scratch_shapes=[pltpu.VMEM((tm, ff), BF16)],
        compiler_params=pltpu.CompilerParams(
            dimension_semantics=("parallel", "parallel"),
            vmem_limit_bytes=VMEM_LIMIT),
        name="ffn_mix" if has_mix else "ffn",
    )(*args)
    return res if emit_next else res[0]


def _unit_lower_inverses(a_mats, ii, jj):
    n = a_mats[0].shape[0]
    eye = (ii == jj).astype(F32)

    def same_block(size):
        return (ii // size) == (jj // size)

    base = same_block(SUBLANES)
    d1 = [jnp.where(base, a, 0.0) for a in a_mats]
    d1b = [d.astype(BF16) for d in d1]
    d2 = [_dot(d, d) for d in d1b]
    d2b = [d.astype(BF16) for d in d2]
    d3 = [_dot(a, b) for a, b in zip(d1b, d2b)]
    d4b = [_dot(d, d).astype(BF16) for d in d2b]
    p1 = [eye - a + b - c for a, b, c in zip(d1, d2, d3)]
    ts = [p + _dot(p.astype(BF16), d) for p, d in zip(p1, d4b)]
    size = SUBLANES
    while size < n:
        below = same_block(2 * size) & jnp.logical_not(same_block(size))
        offs = [jnp.where(below, a, 0.0).astype(BF16) for a in a_mats]
        tbs = [t.astype(BF16) for t in ts]
        xs = [_dot(o, t).astype(BF16) for o, t in zip(offs, tbs)]
        ts = [t - _dot(tb, x) for t, tb, x in zip(ts, tbs, xs)]
        size *= 2
    return ts


def _split3(x):
    hi = x.astype(BF16)
    r1 = x - hi.astype(F32)
    mid = r1.astype(BF16)
    lo = (r1 - mid.astype(F32)).astype(BF16)
    return hi, mid, lo


def _gdn_kernel(h0_ref, hn_ref, wproj_ref, convw_ref, alog_ref, dtb_ref, normw_ref,
                o_ref, proj_scr, xbuf, state, *, chunk):
    nb, _, d = hn_ref.shape
    nh, dk, dv = GDN_HEADS, GDN_DK, GDN_DV
    qk_w = nh * dk
    conv_w = 2 * qk_w + nh * dv
    z_off = conv_w
    ab_off = conv_w + nh * dv
    pad = SUBLANES
    probs = [(b, hh) for b in range(nb) for hh in range(nh)]

    @pl.when(pl.program_id(0) == 0)
    def _():
        state[...] = jnp.zeros_like(state)
        xbuf[:, 0:pad, :] = jnp.zeros((nb, pad, conv_w), F32)
        proj_scr[...] = _dot(h0_ref[...].reshape(nb * chunk, d), wproj_ref[...])

    for b in range(nb):
        xbuf[b, pad:pad + chunk, :] = proj_scr[b * chunk:(b + 1) * chunk, :conv_w]
    ab = proj_scr[:, ab_off:ab_off + LANES]
    z_gates = [_silu(proj_scr[b * chunk:(b + 1) * chunk, z_off + hh * dv:z_off + (hh + 1) * dv])
               for b, hh in probs]
    h_next = hn_ref[...].reshape(nb * chunk, d)
    n_cols = proj_scr.shape[1]
    col_blocks = [(c0, min(c0 + 4 * LANES, n_cols)) for c0 in range(0, n_cols, 4 * LANES)]

    g_all = -jnp.exp(alog_ref[...]) * jax.nn.softplus(ab + dtb_ref[...])
    beta_all = jax.nn.sigmoid(ab)

    ii = lax.broadcasted_iota(jnp.int32, (chunk, chunk), 0)
    jj = lax.broadcasted_iota(jnp.int32, (chunk, chunk), 1)
    incl = ii >= jj
    strict = ii > jj
    tri = incl.astype(BF16)
    cw = convw_ref[...]

    qkv, gcum, gcum_t = [], [], []
    for b in range(nb):
        rows = slice(b * chunk, (b + 1) * chunk)
        for c0, c1 in col_blocks[b::nb]:
            proj_scr[:, c0:c1] = _dot(h_next, wproj_ref[:, c0:c1])
        conv = xbuf[b, pad:pad + chunk, :] * cw[CONV_K - 1:CONV_K, :]
        for tap in range(1, CONV_K):
            conv = conv + (xbuf[b, pad - tap:pad - tap + chunk, :]
                           * cw[CONV_K - 1 - tap:CONV_K - tap, :])
        xbuf[b, 0:pad, :] = xbuf[b, chunk:chunk + pad, :]
        qkv.append(_silu(conv))
        g_hi, g_mid, g_lo = _split3(g_all[rows, :])
        gc = _dot(tri, g_hi) + _dot(tri, g_mid) + _dot(tri, g_lo)
        gcum.append(gc)
        gcum_t.append(gc.T)

    qs, ks, vs, kbs, g_cols, b_cols, decays = [], [], [], [], [], [], []
    for b, hh in probs:
        q = qkv[b][:, hh * dk:(hh + 1) * dk]
        k = qkv[b][:, qk_w + hh * dk:qk_w + (hh + 1) * dk]
        vs.append(qkv[b][:, 2 * qk_w + hh * dv:2 * qk_w + (hh + 1) * dv])
        q = q * (lax.rsqrt(jnp.sum(q * q, axis=-1, keepdims=True) + EPS) * (dk ** -0.5))
        k = k * lax.rsqrt(jnp.sum(k * k, axis=-1, keepdims=True) + EPS)
        qs.append(q)
        ks.append(k)
        kbs.append(k.astype(BF16))
        g_col = gcum[b][:, hh:hh + 1]
        g_row = gcum_t[b][hh:hh + 1, :]
        g_cols.append(g_col)
        b_cols.append(beta_all[b * chunk:(b + 1) * chunk, nh + hh:nh + hh + 1])
        decays.append(jnp.exp(jnp.where(incl, g_col - g_row, -jnp.inf)))

    kks = [_dot_nt(kb, kb) for kb in kbs]
    qks = [(_dot_nt(q.astype(BF16), kb) * dec).astype(BF16) for q, kb, dec in zip(qs, kbs, decays)]
    a_mats = [jnp.where(strict, bc * kk * dec, 0.0) for bc, kk, dec in zip(b_cols, kks, decays)]
    t_invs = _unit_lower_inverses(a_mats, ii, jj)

    e_gs = [jnp.exp(g) for g in g_cols]
    rhss = [jnp.concatenate([v * bc, k * (bc * eg)], axis=1).astype(BF16)
            for v, k, bc, eg in zip(vs, ks, b_cols, e_gs)]
    w_alls = [_dot(t.astype(BF16), r) for t, r in zip(t_invs, rhss)]
    g_lasts = [g[chunk - 1:chunk, :] for g in g_cols]
    q_decs = [(q * eg).astype(BF16) for q, eg in zip(qs, e_gs)]
    k_dec_ts = [(k * jnp.exp(gl - g)).T.astype(BF16) for k, gl, g in zip(ks, g_lasts, g_cols)]

    s_olds = [state[b, hh] for b, hh in probs]
    s_bs = [s.astype(BF16) for s in s_olds]
    v_nbs = [(w[:, :dv] - _dot(w[:, dv:].astype(BF16), sb)).astype(BF16)
             for w, sb in zip(w_alls, s_bs)]
    outs = [_dot(qd, sb) + _dot(qk, vn) for qd, sb, qk, vn in zip(q_decs, s_bs, qks, v_nbs)]
    for (b, hh), s_old, gl, kdt, vn in zip(probs, s_olds, g_lasts, k_dec_ts, v_nbs):
        state[b, hh] = s_old * jnp.exp(gl) + _dot(kdt, vn)
    for (b, hh), o, zg in zip(probs, outs, z_gates):
        o = o * _rms_scale(o, dv) * normw_ref[...] * zg
        o_ref[b, :, hh * dv:(hh + 1) * dv] = o.astype(o_ref.dtype)


def _gdn(h2, w_proj, conv_w, a_log, dt_bias, norm_w):
    bsz, s, d = h2.shape
    chunk = min(GDN_CHUNK, s)
    n_chunks = s // chunk
    nh, dv = GDN_HEADS, GDN_DV
    cw = conv_w.shape[1]
    full = lambda j: (0, 0)
    return pl.pallas_call(
        functools.partial(_gdn_kernel, chunk=chunk),
        grid=(n_chunks,),
        in_specs=[pl.BlockSpec((bsz, chunk, d), lambda j: (0, 0, 0)),
                  pl.BlockSpec((bsz, chunk, d), lambda j: (0, jnp.minimum(j + 1, n_chunks - 1), 0)),
                  pl.BlockSpec(w_proj.shape, full),
                  pl.BlockSpec(conv_w.shape, full),
                  pl.BlockSpec((1, LANES), full),
                  pl.BlockSpec((1, LANES), full),
                  pl.BlockSpec((1, dv), full)],
        out_specs=pl.BlockSpec((bsz, chunk, nh * dv), lambda j: (0, j, 0)),
        out_shape=jax.ShapeDtypeStruct((bsz, s, nh * dv), BF16),
        scratch_shapes=[pltpu.VMEM((bsz * chunk, w_proj.shape[1]), F32),
                        pltpu.VMEM((bsz, chunk + SUBLANES, cw), F32),
                        pltpu.VMEM((bsz, nh, GDN_DK, dv), F32)],
        compiler_params=pltpu.CompilerParams(
            dimension_semantics=("arbitrary",),
            vmem_limit_bytes=VMEM_LIMIT),
        name="gdn",
    )(h2, h2, w_proj, conv_w, a_log, dt_bias, norm_w)


def _mla_prep_kernel(h_ref, pos_ref, wm_ref, wuq_ref, wukv_ref, qnw_ref, kvnw_ref,
                     wqn_ref, wqr_ref, wqrs_ref, wkn_ref, wkr_ref, wkrs_ref,
                     invf_ref, csel_ref, ssel_ref, qext_ref, kext_ref, qt_ref, k_ref, vt_ref):
    nh = MLA_HEADS
    scale = (MLA_NOPE + MLA_ROPE) ** -0.5 * LOG2_E
    h = h_ref[0]
    lat = _dot(h, wm_ref[...])
    o1 = MLA_Q_LORA
    o2 = o1 + MLA_KV_LORA
    cq = lat[:, :o1]
    ckv = lat[:, o1:o2]
    kr = lat[:, o2:o2 + LANES]
    krs = lat[:, o2 + LANES:o2 + 2 * LANES]
    cqn = (cq * _rms_scale(cq, MLA_Q_LORA) * qnw_ref[...]).astype(BF16)
    ckvn = (ckv * _rms_scale(ckv, MLA_KV_LORA) * kvnw_ref[...]).astype(BF16)
    qf = _dot(cqn, wuq_ref[...])
    kvf = _dot(ckvn, wukv_ref[...])

    ang_t = invf_ref[...] * pos_ref[0].astype(F32)

    def spread(table_t, sel_ref):
        parts = [_dot_tn(part, sel_ref[...]) for part in _split3(table_t)]
        return (parts[0] + parts[1]) + parts[2]

    cos_p = spread(jnp.cos(ang_t), csel_ref)
    sin_p = spread(jnp.sin(ang_t), ssel_ref)

    k_rope = ((kr * wkr_ref[...] * cos_p + krs * wkrs_ref[...] * sin_p) * _rms_scale(kr, MLA_ROPE)
              + kext_ref[...])
    qw = MLA_NOPE + 2 * LANES
    kw = MLA_NOPE + MLA_V
    for hh in range(nh):
        qn = qf[:, hh * qw:hh * qw + MLA_NOPE]
        qr = qf[:, hh * qw + MLA_NOPE:hh * qw + MLA_NOPE + LANES]
        qrs = qf[:, hh * qw + MLA_NOPE + LANES:(hh + 1) * qw]
        qn = qn * _rms_scale(qn, MLA_NOPE) * (wqn_ref[...] * scale)
        q_rope = ((qr * wqr_ref[...] * cos_p + qrs * wqrs_ref[...] * sin_p)
                  * (_rms_scale(qr, MLA_ROPE) * scale)) + qext_ref[...]
        qt_ref[0, hh] = jnp.concatenate([qn, q_rope], axis=1).T.astype(BF16)
        kn = kvf[:, hh * kw:hh * kw + MLA_NOPE]
        v = kvf[:, hh * kw + MLA_NOPE:(hh + 1) * kw]
        kn = kn * _rms_scale(kn, MLA_NOPE) * wkn_ref[...]
        k_ref[0, hh] = jnp.concatenate([kn, k_rope], axis=1).astype(BF16)
        vt_ref[0, hh] = v.T.astype(BF16)


def _mla_prep(h2, pos3, w_m, w_uq, w_ukv, vecs, *, tm=512):
    bsz, s, d = h2.shape
    tm = min(tm, s)
    nh = MLA_HEADS
    qk_dim = MLA_NOPE + LANES
    full = lambda b, i: (0, 0)
    vec_specs = [pl.BlockSpec(v.shape, full) for v in vecs]
    return pl.pallas_call(
        _mla_prep_kernel,
        grid=(bsz, s // tm),
        in_specs=[pl.BlockSpec((1, tm, d), lambda b, i: (b, i, 0)),
                  pl.BlockSpec((1, 1, tm), lambda b, i: (b, 0, i)),
                  pl.BlockSpec(w_m.shape, full),
                  pl.BlockSpec(w_uq.shape, full),
                  pl.BlockSpec(w_ukv.shape, full)] + vec_specs,
        out_specs=[pl.BlockSpec((1, nh, qk_dim, tm), lambda b, i: (b, 0, 0, i)),
                   pl.BlockSpec((1, nh, tm, qk_dim), lambda b, i: (b, 0, i, 0)),
                   pl.BlockSpec((1, nh, MLA_V, tm), lambda b, i: (b, 0, 0, i))],
        out_shape=[jax.ShapeDtypeStruct((bsz, nh, qk_dim, s), BF16),
                   jax.ShapeDtypeStruct((bsz, nh, s, qk_dim), BF16),
                   jax.ShapeDtypeStruct((bsz, nh, MLA_V, s), BF16)],
        compiler_params=pltpu.CompilerParams(
            dimension_semantics=("parallel", "parallel"),
            vmem_limit_bytes=VMEM_LIMIT),
        name="mla_prep",
    )(h2, pos3, w_m, w_uq, w_ukv, *vecs)


def _attn_kernel(mode_ref, qt_ref, k_ref, vt_ref, nw_ref, o_ref, s_scr, m_scr, l_scr, acc_scr, *,
                 tq, tk):
    i = pl.program_id(2)
    per_q = tq // tk
    kc = lax.broadcasted_iota(jnp.int32, (tk, tq), 0) // ATTN_CHUNK
    qc = lax.broadcasted_iota(jnp.int32, (tk, tq), 1) // ATTN_CHUNK
    visible = qc >= kc

    def key_tile(hh, j):
        start = pl.multiple_of(j * tk, tk)
        return k_ref[0, hh, pl.ds(start, tk), :], vt_ref[0, hh, :, pl.ds(start, tk)]

    def finish():
        for hh in range(2):
            o = (acc_scr[hh] / l_scr[hh]).T
            o = o * _rms_scale(o, MLA_V) * nw_ref[...]
            o_ref[0, :, hh * MLA_V:(hh + 1) * MLA_V] = o.astype(o_ref.dtype)

    l_scr[...] = jnp.zeros(l_scr.shape, F32)
    acc_scr[...] = jnp.zeros(acc_scr.shape, F32)

    @pl.when(mode_ref[0] == 1)
    def _fixed_reference():
        def tile(hh, j, q0=0, mask=None):
            kj, vj = key_tile(hh, j)
            st = _dot(kj, qt_ref[0, hh, :, q0:])
            if mask is not None:
                st = jnp.where(mask, st, -jnp.inf)
            p = jnp.exp2(st)
            l_scr[hh, :, q0:] += jnp.sum(p, axis=0, keepdims=True)
            acc_scr[hh, :, q0:] += _dot(vj, p.astype(BF16))

        def body(jj, _):
            for r in range(per_q):
                tile(0, jj * per_q + r)
                tile(1, jj * per_q + r)
            return 0

        lax.fori_loop(0, i, body, 0)
        for r in range(per_q):
            for hh in range(2):
                tile(hh, i * per_q + r, r * tk, visible[:, :tq - r * tk])
        finish()

    @pl.when(mode_ref[0] == 0)
    def _online():
        def scores(hh, j, q0=0):
            kj, _ = key_tile(hh, j)
            s_scr[hh, :, :tq - q0] = _dot(kj, qt_ref[0, hh, :, q0:])

        def accumulate(hh, j, q0=0, mask=None):
            w = tq - q0
            if mask is not None:
                s_scr[hh, :, :w] = jnp.where(mask, s_scr[hh, :, :w], -jnp.inf)
            m = m_scr[hh, :, q0:]
            m_new = jnp.maximum(m, jnp.max(s_scr[hh, :, :w], axis=0, keepdims=True))
            alpha = jnp.exp2(m - m_new)
            p = jnp.exp2(s_scr[hh, :, :w] - m_new)
            m_scr[hh, :, q0:] = m_new
            l_scr[hh, :, q0:] = alpha * l_scr[hh, :, q0:] + jnp.sum(p, axis=0, keepdims=True)
            _, vj = key_tile(hh, j)
            acc_scr[hh, :, q0:] = alpha * acc_scr[hh, :, q0:] + _dot(vj, p.astype(BF16))

        m_scr[...] = jnp.full(m_scr.shape, -jnp.inf, F32)
        scores(0, 0)

        def body(jj, _):
            for r in range(per_q):
                j = jj * per_q + r
                scores(1, j)
                accumulate(0, j)
                scores(0, j + 1)
                accumulate(1, j)
            return 0

        lax.fori_loop(0, i, body, 0)
        for r in range(per_q):
            q0 = r * tk
            j = i * per_q + r
            scores(1, j, q0)
            accumulate(0, j, q0, visible[:, :tq - q0])
            if r < per_q - 1:
                scores(0, j + 1, q0 + tk)
            accumulate(1, j, q0, visible[:, :tq - q0])
        finish()


def _attn(mode, qt, k, vt, out_norm_w, *, tq=1024, tk=512):
    bsz, nh, qk_dim, s = qt.shape
    tq = min(tq, s)
    tk = min(tk, tq)
    heads = 2
    return pl.pallas_call(
        functools.partial(_attn_kernel, tq=tq, tk=tk),
        grid=(bsz, nh // heads, s // tq),
        in_specs=[pl.BlockSpec(memory_space=pltpu.SMEM),
                  pl.BlockSpec((1, heads, qk_dim, tq), lambda b, h, i: (b, h, 0, i)),
                  pl.BlockSpec((1, heads, s, qk_dim), lambda b, h, i: (b, h, 0, 0)),
                  pl.BlockSpec((1, heads, MLA_V, s), lambda b, h, i: (b, h, 0, 0)),
                  pl.BlockSpec((1, MLA_V), lambda b, h, i: (0, 0))],
        out_specs=pl.BlockSpec((1, tq, heads * MLA_V), lambda b, h, i: (b, i, h)),
        out_shape=jax.ShapeDtypeStruct((bsz, s, nh * MLA_V), BF16),
        scratch_shapes=[pltpu.VMEM((heads, tk, tq), F32),
                        pltpu.VMEM((heads, 1, tq), F32),
                        pltpu.VMEM((heads, 1, tq), F32),
                        pltpu.VMEM((heads, MLA_V, tq), F32)],
        compiler_params=pltpu.CompilerParams(
            dimension_semantics=("parallel", "parallel", "arbitrary"),
            vmem_limit_bytes=VMEM_LIMIT),
        name="attn",
    )(mode, qt, k, vt, out_norm_w)


def _pad_lanes(w, width=LANES):
    return jnp.pad(w, [(0, 0)] * (w.ndim - 1) + [(0, width - w.shape[-1])])


def _swap_halves(w):
    half = w.shape[-1] // 2
    return jnp.concatenate([w[..., half:], w[..., :half]], axis=-1)


def _layer(x, mods, pos3, w_ffn1_in, w_ffn1_out, w_in, conv_w, a_log, dt_bias, gdn_norm_w,
           q_norm_w, w_uq, kv_norm_w, w_ukv, qn_q_nope, qn_q_rope, qn_k_nope, qn_k_rope,
           out_norm_w, w_out, w_ffn2_in, w_ffn2_out):
    sh1, s1, g1, sh2, s2, g2, sh3, s3, g3 = mods
    nh = MLA_HEADS

    o_gz = 2 * GDN_HEADS * GDN_DK + 2 * GDN_HEADS * GDN_DV
    o_ab = o_gz + 2 * GDN_HEADS
    o_cq = o_ab + MLA_Q_LORA
    o_ckv = o_cq + MLA_KV_LORA
    w_gdn = jnp.concatenate([w_in[:, :o_gz], _pad_lanes(w_in[:, o_gz:o_ab])], axis=1).astype(BF16)
    w_kr = w_in[:, o_ckv:]
    w_m = jnp.concatenate([w_in[:, o_ab:o_ckv], _pad_lanes(w_kr), _pad_lanes(_swap_halves(w_kr))],
                          axis=1).astype(BF16)
    per_q = MLA_NOPE + MLA_ROPE
    uq_parts = []
    for hh in range(nh):
        wn = w_uq[:, hh * per_q:hh * per_q + MLA_NOPE]
        wr = w_uq[:, hh * per_q + MLA_NOPE:(hh + 1) * per_q]
        uq_parts += [wn, _pad_lanes(wr), _pad_lanes(_swap_halves(wr))]
    w_uq_p = jnp.concatenate(uq_parts, axis=1).astype(BF16)
    w_ukv_b = w_ukv.astype(BF16)

    half = MLA_ROPE // 2
    inv_freq = ROPE_BASE ** (-jnp.arange(half, dtype=F32) / half)
    invf = inv_freq.reshape(half, 1)
    eye = jnp.eye(half, dtype=F32)
    cos_sel = _pad_lanes(jnp.concatenate([eye, eye], axis=1)).astype(BF16)
    sin_sel = _pad_lanes(jnp.concatenate([-eye, eye], axis=1)).astype(BF16)
    row = lambda v: v.reshape(1, -1)

    def sq_norm_bound(w_nope, w_rope):
        return MLA_NOPE * jnp.max(w_nope * w_nope) + MLA_ROPE * jnp.max(w_rope * w_rope)
    q_scale = (MLA_NOPE + MLA_ROPE) ** -0.5 * LOG2_E
    bound = (SCORE_BOUND_MARGIN * q_scale
             * jnp.sqrt(sq_norm_bound(qn_q_nope, qn_q_rope) * sq_norm_bound(qn_k_nope, qn_k_rope)))
    use_bound = bound < MAX_FIXED_REFERENCE
    mode = use_bound.astype(jnp.int32).reshape(1)
    ext_lane = (jnp.arange(LANES) == MLA_ROPE).astype(F32).reshape(1, LANES)
    q_ext = ext_lane * jnp.where(use_bound, -bound, 0.0)
    k_ext = ext_lane
    vecs = [row(q_norm_w), row(kv_norm_w),
            row(qn_q_nope), row(_pad_lanes(qn_q_rope)), row(_pad_lanes(_swap_halves(qn_q_rope))),
            row(qn_k_nope), row(_pad_lanes(qn_k_rope)), row(_pad_lanes(_swap_halves(qn_k_rope))),
            invf, cos_sel, sin_sel, q_ext, k_ext]

    x1, h2 = _ffn(x, sh1, s1, g1, w_ffn1_in.astype(BF16), w_ffn1_out.astype(BF16), nxt=(sh2, s2))

    o_a = _gdn(h2, w_gdn, conv_w, row(_pad_lanes(a_log)), row(_pad_lanes(dt_bias)), row(gdn_norm_w))
    qt, k, vt = _mla_prep(h2, pos3, w_m, w_uq_p, w_ukv_b, vecs)
    o_b = _attn(mode, qt, k, vt, row(out_norm_w))

    return _ffn(x1, sh3, s3, g3, w_ffn2_in.astype(BF16), w_ffn2_out.astype(BF16),
                mix=(o_a, o_b, w_out.astype(BF16), g2))


def kernel(x, c, positions, w_ada, b_ada, ffn1_w_in, ffn1_w_out, w_in, gdn_conv_w, gdn_a_log, gdn_dt_bias, gdn_norm_w, mla_q_norm_w, mla_w_uq, mla_kv_norm_w, mla_w_ukv, qkn_q_nope, qkn_q_rope, qkn_k_nope, qkn_k_rope, mla_out_norm_w, w_out, ffn2_w_in, ffn2_w_out):
    bsz, s, d = x.shape
    pos3 = positions.reshape(bsz, 1, s)
    for l in range(w_ada.shape[0]):
        mod = _mod(c, w_ada[l], b_ada[l])
        mods = [m.reshape(bsz, 1, d) for m in jnp.split(mod, N_MOD, axis=-1)]
        x = _layer(x, mods, pos3, ffn1_w_in[l], ffn1_w_out[l], w_in[l], gdn_conv_w[l],
                   gdn_a_log[l], gdn_dt_bias[l], gdn_norm_w[l], mla_q_norm_w[l], mla_w_uq[l],
                   mla_kv_norm_w[l], mla_w_ukv[l], qkn_q_nope[l], qkn_q_rope[l], qkn_k_nope[l],
                   qkn_k_rope[l], mla_out_norm_w[l], w_out[l], ffn2_w_in[l], ffn2_w_out[l])
    return x
```

```python
import functools

import jax
import jax.numpy as jnp
from jax import lax
from jax.experimental import pallas as pl
from jax.experimental.pallas import tpu as pltpu

F32 = jnp.float32
BF16 = jnp.bfloat16

EPS = 1e-6
ATTN_CHUNK = 64
GDN_HEADS = 4
GDN_DK = 128
GDN_DV = 128
CONV_K = 4
MLA_HEADS = 4
MLA_NOPE = 128
MLA_ROPE = 64
MLA_V = 128
MLA_Q_LORA = 384
MLA_KV_LORA = 256
ROPE_BASE = 10000.0
N_MOD = 9
LOG2_E = 1.4426950408889634
MAX_FIXED_REFERENCE = 60.0
SCORE_BOUND_MARGIN = 1.0625

LANES = 128
SUBLANES = 8
VMEM_LIMIT = 56 * 1024 * 1024

GDN_CHUNK = 128
GDN_PROJ_BLOCK = 256
GDN_CONV_PIECE = 512
FFN_SUBTILES = 4


def _dot(a, b):
    return jnp.dot(a, b, preferred_element_type=F32)


def _dot_nt(a, b):
    return lax.dot_general(a, b, (((1,), (1,)), ((), ())), preferred_element_type=F32)


def _dot_tn(a, b):
    return lax.dot_general(a, b, (((0,), (0,)), ((), ())), preferred_element_type=F32)


def _silu(x):
    return x * jax.nn.sigmoid(x)


def _rms_scale(x, n):
    return lax.rsqrt(jnp.sum(x * x, axis=-1, keepdims=True) * (1.0 / n) + EPS)


def _mod_kernel(c_ref, w_ref, b_ref, o_ref):
    sc = _silu(c_ref[...])
    o_ref[...] = jnp.dot(sc, w_ref[...], precision=lax.Precision.HIGHEST,
                         preferred_element_type=F32) + b_ref[...]


def _mod(c, w_ada, b_ada):
    bsz, d = c.shape
    n = w_ada.shape[1]
    tn = 1024
    return pl.pallas_call(
        _mod_kernel,
        grid=(n // tn,),
        in_specs=[pl.BlockSpec((bsz, d), lambda j: (0, 0)),
                  pl.BlockSpec((d, tn), lambda j: (0, j)),
                  pl.BlockSpec((1, tn), lambda j: (0, j))],
        out_specs=pl.BlockSpec((bsz, tn), lambda j: (0, j)),
        out_shape=jax.ShapeDtypeStruct((bsz, n), F32),
        compiler_params=pltpu.CompilerParams(dimension_semantics=("arbitrary",),
                                             vmem_limit_bytes=VMEM_LIMIT),
        name="mod",
    )(c, w_ada, b_ada.reshape(1, n))


def _ffn_kernel(*refs, has_mix, emit_next, d_model, d_ff, tf, n_sub):
    it = iter(refs)
    x_ref = next(it)
    if has_mix:
        oa_ref, ob_ref, wmix_ref, gmix_ref = next(it), next(it), next(it), next(it)
    shift_ref, scale_ref, gate_ref = next(it), next(it), next(it)
    win_ref, wout_ref = next(it), next(it)
    if emit_next:
        nshift_ref, nscale_ref = next(it), next(it)
    out_ref = next(it)
    if emit_next:
        hn_ref = next(it)
    h_sc, a_sc = next(it), next(it)

    tm = x_ref.shape[1]
    ts = tm // n_sub
    subs = [slice(r * ts, (r + 1) * ts) for r in range(n_sub)]

    def prologue(rows):
        x = x_ref[0, rows]
        if has_mix:
            half = oa_ref.shape[-1]
            y = (_dot(oa_ref[0, rows], wmix_ref[:half, :])
                 + _dot(ob_ref[0, rows], wmix_ref[half:, :]))
            x = x + gmix_ref[0] * y
            out_ref[0, rows] = x
        h_sc[rows] = (x * _rms_scale(x, d_model) * (1.0 + scale_ref[0])
                      + shift_ref[0]).astype(BF16)

    def hidden(rows):
        h = h_sc[rows]
        for c in range(d_ff // tf):
            g = _dot(h, win_ref[:, c * tf:(c + 1) * tf])
            u = _dot(h, win_ref[:, d_ff + c * tf:d_ff + (c + 1) * tf])
            a_sc[rows, c * tf:(c + 1) * tf] = (_silu(g) * u).astype(BF16)

    def epilogue(rows):
        x = out_ref[0, rows] if has_mix else x_ref[0, rows]
        out = x + 0.5 * gate_ref[0] * _dot(a_sc[rows], wout_ref[...])
        out_ref[0, rows] = out
        if emit_next:
            hn = out * _rms_scale(out, d_model) * (1.0 + nscale_ref[0]) + nshift_ref[0]
            hn_ref[0, rows] = hn.astype(BF16)

    prologue(subs[0])
    for r in range(n_sub):
        if r + 1 < n_sub:
            prologue(subs[r + 1])
        hidden(subs[r])
        epilogue(subs[r])


def _resident(shape, index_map):
    return pl.BlockSpec(shape, index_map, pipeline_mode=pl.Buffered(1))


def _ffn(x, shift, scale, gate, w_in_bf, w_out_bf, *, mix=None, nxt=None, tm=1024, tf=256):
    bsz, s, d = x.shape
    ff = w_out_bf.shape[0]
    tm = min(tm, s)
    assert ff % tf == 0 and s % tm == 0 and tm % (FFN_SUBTILES * 2 * SUBLANES) == 0
    has_mix = mix is not None
    emit_next = nxt is not None

    row = lambda b, i: (b, i, 0)
    vec = lambda b, i: (b, 0, 0)
    const = lambda b, i: (0, 0)
    args = [x]
    in_specs = [pl.BlockSpec((1, tm, d), row)]
    if has_mix:
        o_a, o_b, w_mix, g_mix = mix
        args += [o_a, o_b, w_mix, g_mix]
        in_specs += [pl.BlockSpec((1, tm, o_a.shape[-1]), row),
                     pl.BlockSpec((1, tm, o_b.shape[-1]), row),
                     _resident(w_mix.shape, const),
                     pl.BlockSpec((1, 1, d), vec)]
    args += [shift, scale, gate, w_in_bf, w_out_bf]
    in_specs += [pl.BlockSpec((1, 1, d), vec)] * 3
    in_specs += [_resident(w_in_bf.shape, const), _resident(w_out_bf.shape, const)]
    if emit_next:
        args += list(nxt)
        in_specs += [pl.BlockSpec((1, 1, d), vec)] * 2
    out_shape = [jax.ShapeDtypeStruct((bsz, s, d), F32)]
    out_specs = [pl.BlockSpec((1, tm, d), row)]
    if emit_next:
        out_shape.append(jax.ShapeDtypeStruct((bsz, s, d), BF16))
        out_specs.append(pl.BlockSpec((1, tm, d), row))

    res = pl.pallas_call(
        functools.partial(_ffn_kernel, has_mix=has_mix, emit_next=emit_next, d_model=d,
                          d_ff=ff, tf=tf, n_sub=FFN_SUBTILES),
        grid=(bsz, s // tm),
        in_specs=in_specs,
        out_specs=out_specs,
        out_shape=out_shape,
        scratch_shapes=[pltpu.VMEM((tm, d), BF16), pltpu.VMEM((tm, ff), BF16)],
        compiler_params=pltpu.CompilerParams(
            dimension_semantics=("parallel", "parallel"),
            vmem_limit_bytes=VMEM_LIMIT),
        name="ffn_mix" if has_mix else "ffn",
    )(*args)
    return res if emit_next else res[0]


def _unit_lower_inverses(a_mats, ii, jj):
    n = a_mats[0].shape[0]
    eye = (ii == jj).astype(F32)

    def same_block(size):
        return (ii // size) == (jj // size)

    base = same_block(SUBLANES)
    d1 = [jnp.where(base, a, 0.0) for a in a_mats]
    d1b = [d.astype(BF16) for d in d1]
    d2 = [_dot(d, d) for d in d1b]
    d2b = [d.astype(BF16) for d in d2]
    d3 = [_dot(a, b) for a, b in zip(d1b, d2b)]
    d4b = [_dot(d, d).astype(BF16) for d in d2b]
    p1 = [eye - a + b - c for a, b, c in zip(d1, d2, d3)]
    ts = [p + _dot(p.astype(BF16), d) for p, d in zip(p1, d4b)]
    size = SUBLANES
    while size < n:
        below = same_block(2 * size) & jnp.logical_not(same_block(size))
        offs = [jnp.where(below, a, 0.0).astype(BF16) for a in a_mats]
        tbs = [t.astype(BF16) for t in ts]
        xs = [_dot(o, t).astype(BF16) for o, t in zip(offs, tbs)]
        ts = [t - _dot(tb, x) for t, tb, x in zip(ts, tbs, xs)]
        size *= 2
    return ts


def _split3(x):
    hi = x.astype(BF16)
    r1 = x - hi.astype(F32)
    mid = r1.astype(BF16)
    lo = (r1 - mid.astype(F32)).astype(BF16)
    return hi, mid, lo


def _gdn_kernel(h0_ref, hn_ref, wproj_ref, convw_ref, alog_ref, dtb_ref, normw_ref,
                o_ref, proj_scr, xbuf, state, *, chunk):
    nb, _, d = hn_ref.shape
    nh, dk, dv = GDN_HEADS, GDN_DK, GDN_DV
    qk_w = nh * dk
    conv_w = 2 * qk_w + nh * dv
    z_off = conv_w
    ab_off = conv_w + nh * dv
    pad = SUBLANES
    probs = [(b, hh) for b in range(nb) for hh in range(nh)]

    @pl.when(pl.program_id(0) == 0)
    def _():
        state[...] = jnp.zeros_like(state)
        xbuf[:, 0:pad, :] = jnp.zeros((nb, pad, conv_w), F32)
        proj_scr[...] = _dot(h0_ref[...].reshape(nb * chunk, d), wproj_ref[...])

    for b in range(nb):
        xbuf[b, pad:pad + chunk, :] = proj_scr[b * chunk:(b + 1) * chunk, :conv_w]
    ab = proj_scr[:, ab_off:ab_off + LANES]
    z_gates = [_silu(proj_scr[b * chunk:(b + 1) * chunk, z_off + hh * dv:z_off + (hh + 1) * dv])
               for b, hh in probs]
    h_next = hn_ref[...].reshape(nb * chunk, d)
    n_cols = proj_scr.shape[1]
    col_blocks = [(c0, min(c0 + GDN_PROJ_BLOCK, n_cols))
                  for c0 in range(0, n_cols, GDN_PROJ_BLOCK)]

    g_all = -jnp.exp(alog_ref[...]) * jax.nn.softplus(ab + dtb_ref[...])
    beta_all = jax.nn.sigmoid(ab)

    ii = lax.broadcasted_iota(jnp.int32, (chunk, chunk), 0)
    jj = lax.broadcasted_iota(jnp.int32, (chunk, chunk), 1)
    incl = ii >= jj
    strict = ii > jj
    tri = incl.astype(BF16)
    cw = convw_ref[...]

    n_pieces = nb * (conv_w // GDN_CONV_PIECE)
    qkv, gcum, gcum_t = [], [], []
    piece = 0
    for b in range(nb):
        rows = slice(b * chunk, (b + 1) * chunk)
        parts = []
        for p0 in range(0, conv_w, GDN_CONV_PIECE):
            for c0, c1 in col_blocks[piece::n_pieces]:
                proj_scr[:, c0:c1] = _dot(h_next, wproj_ref[:, c0:c1])
            piece += 1
            cols = slice(p0, p0 + GDN_CONV_PIECE)
            conv = xbuf[b, pad:pad + chunk, cols] * cw[CONV_K - 1:CONV_K, cols]
            for tap in range(1, CONV_K):
                conv = conv + (xbuf[b, pad - tap:pad - tap + chunk, cols]
                               * cw[CONV_K - 1 - tap:CONV_K - tap, cols])
            parts.append(_silu(conv))
        xbuf[b, 0:pad, :] = xbuf[b, chunk:chunk + pad, :]
        qkv.append(jnp.concatenate(parts, axis=1))
        g_hi, g_mid, g_lo = _split3(g_all[rows, :])
        gc = _dot(tri, g_hi) + _dot(tri, g_mid) + _dot(tri, g_lo)
        gcum.append(gc)
        gcum_t.append(gc.T)

    qs, ks, vs, kbs, g_cols, b_cols, decays = [], [], [], [], [], [], []
    for b, hh in probs:
        q = qkv[b][:, hh * dk:(hh + 1) * dk]
        k = qkv[b][:, qk_w + hh * dk:qk_w + (hh + 1) * dk]
        vs.append(qkv[b][:, 2 * qk_w + hh * dv:2 * qk_w + (hh + 1) * dv])
        q = q * (lax.rsqrt(jnp.sum(q * q, axis=-1, keepdims=True) + EPS) * (dk ** -0.5))
        k = k * lax.rsqrt(jnp.sum(k * k, axis=-1, keepdims=True) + EPS)
        qs.append(q)
        ks.append(k)
        kbs.append(k.astype(BF16))
        g_col = gcum[b][:, hh:hh + 1]
        g_row = gcum_t[b][hh:hh + 1, :]
        g_cols.append(g_col)
        b_cols.append(beta_all[b * chunk:(b + 1) * chunk, nh + hh:nh + hh + 1])
        decays.append(jnp.exp(jnp.where(incl, g_col - g_row, -jnp.inf)))

    kks = [_dot_nt(kb, kb) for kb in kbs]
    qks = [(_dot_nt(q.astype(BF16), kb) * dec).astype(BF16) for q, kb, dec in zip(qs, kbs, decays)]
    a_mats = [jnp.where(strict, bc * kk * dec, 0.0) for bc, kk, dec in zip(b_cols, kks, decays)]
    t_invs = _unit_lower_inverses(a_mats, ii, jj)

    e_gs = [jnp.exp(g) for g in g_cols]
    rhss = [jnp.concatenate([v * bc, k * (bc * eg)], axis=1).astype(BF16)
            for v, k, bc, eg in zip(vs, ks, b_cols, e_gs)]
    w_alls = [_dot(t.astype(BF16), r) for t, r in zip(t_invs, rhss)]
    g_lasts = [g[chunk - 1:chunk, :] for g in g_cols]
    q_decs = [(q * eg).astype(BF16) for q, eg in zip(qs, e_gs)]
    k_dec_ts = [(k * jnp.exp(gl - g)).T.astype(BF16) for k, gl, g in zip(ks, g_lasts, g_cols)]

    s_olds = [state[b, hh] for b, hh in probs]
    s_bs = [s.astype(BF16) for s in s_olds]
    v_nbs = [(w[:, :dv] - _dot(w[:, dv:].astype(BF16), sb)).astype(BF16)
             for w, sb in zip(w_alls, s_bs)]
    outs = [_dot(qd, sb) + _dot(qk, vn) for qd, sb, qk, vn in zip(q_decs, s_bs, qks, v_nbs)]
    for (b, hh), s_old, gl, kdt, vn in zip(probs, s_olds, g_lasts, k_dec_ts, v_nbs):
        state[b, hh] = s_old * jnp.exp(gl) + _dot(kdt, vn)
    for (b, hh), o, zg in zip(probs, outs, z_gates):
        o = o * _rms_scale(o, dv) * normw_ref[...] * zg
        o_ref[b, :, hh * dv:(hh + 1) * dv] = o.astype(o_ref.dtype)


def _gdn(h2, w_proj, conv_w, a_log, dt_bias, norm_w):
    bsz, s, d = h2.shape
    chunk = min(GDN_CHUNK, s)
    n_chunks = s // chunk
    nh, dv = GDN_HEADS, GDN_DV
    cw = conv_w.shape[1]
    full = lambda j: (0, 0)
    return pl.pallas_call(
        functools.partial(_gdn_kernel, chunk=chunk),
        grid=(n_chunks,),
        in_specs=[pl.BlockSpec((bsz, chunk, d), lambda j: (0, 0, 0)),
                  pl.BlockSpec((bsz, chunk, d), lambda j: (0, jnp.minimum(j + 1, n_chunks - 1), 0)),
                  pl.BlockSpec(w_proj.shape, full),
                  pl.BlockSpec(conv_w.shape, full),
                  pl.BlockSpec((1, LANES), full),
                  pl.BlockSpec((1, LANES), full),
                  pl.BlockSpec((1, dv), full)],
        out_specs=pl.BlockSpec((bsz, chunk, nh * dv), lambda j: (0, j, 0)),
        out_shape=jax.ShapeDtypeStruct((bsz, s, nh * dv), BF16),
        scratch_shapes=[pltpu.VMEM((bsz * chunk, w_proj.shape[1]), F32),
                        pltpu.VMEM((bsz, chunk + SUBLANES, cw), F32),
                        pltpu.VMEM((bsz, nh, GDN_DK, dv), F32)],
        compiler_params=pltpu.CompilerParams(
            dimension_semantics=("arbitrary",),
            vmem_limit_bytes=VMEM_LIMIT),
        name="gdn",
    )(h2, h2, w_proj, conv_w, a_log, dt_bias, norm_w)


def _mla_prep_kernel(h_ref, pos_ref, wm_ref, wuq_ref, wukv_ref, qnw_ref, kvnw_ref,
                     wqn_ref, wqr_ref, wqrs_ref, wkn_ref, wkr_ref, wkrs_ref,
                     invf_ref, csel_ref, ssel_ref, qext_ref, kext_ref, qt_ref, k_ref, vt_ref):
    nh = MLA_HEADS
    scale = (MLA_NOPE + MLA_ROPE) ** -0.5 * LOG2_E
    h = h_ref[0]
    lat = _dot(h, wm_ref[...])
    o1 = MLA_Q_LORA
    o2 = o1 + MLA_KV_LORA
    cq = lat[:, :o1]
    ckv = lat[:, o1:o2]
    kr = lat[:, o2:o2 + LANES]
    krs = lat[:, o2 + LANES:o2 + 2 * LANES]
    cqn = (cq * _rms_scale(cq, MLA_Q_LORA) * qnw_ref[...]).astype(BF16)
    ckvn = (ckv * _rms_scale(ckv, MLA_KV_LORA) * kvnw_ref[...]).astype(BF16)
    qf = _dot(cqn, wuq_ref[...])
    kvf = _dot(ckvn, wukv_ref[...])

    ang_t = invf_ref[...] * pos_ref[0].astype(F32)

    def spread(table_t, sel_ref):
        parts = [_dot_tn(part, sel_ref[...]) for part in _split3(table_t)]
        return (parts[0] + parts[1]) + parts[2]

    cos_p = spread(jnp.cos(ang_t), csel_ref)
    sin_p = spread(jnp.sin(ang_t), ssel_ref)

    k_rope = ((kr * wkr_ref[...] * cos_p + krs * wkrs_ref[...] * sin_p) * _rms_scale(kr, MLA_ROPE)
              + kext_ref[...])
    qw = MLA_NOPE + 2 * LANES
    kw = MLA_NOPE + MLA_V
    for hh in range(nh):
        qn = qf[:, hh * qw:hh * qw + MLA_NOPE]
        qr = qf[:, hh * qw + MLA_NOPE:hh * qw + MLA_NOPE + LANES]
        qrs = qf[:, hh * qw + MLA_NOPE + LANES:(hh + 1) * qw]
        qn = qn * _rms_scale(qn, MLA_NOPE) * (wqn_ref[...] * scale)
        q_rope = ((qr * wqr_ref[...] * cos_p + qrs * wqrs_ref[...] * sin_p)
                  * (_rms_scale(qr, MLA_ROPE) * scale)) + qext_ref[...]
        qt_ref[0, hh] = jnp.concatenate([qn, q_rope], axis=1).T.astype(BF16)
        kn = kvf[:, hh * kw:hh * kw + MLA_NOPE]
        v = kvf[:, hh * kw + MLA_NOPE:(hh + 1) * kw]
        kn = kn * _rms_scale(kn, MLA_NOPE) * wkn_ref[...]
        k_ref[0, hh] = jnp.concatenate([kn, k_rope], axis=1).astype(BF16)
        vt_ref[0, hh] = v.T.astype(BF16)


def _mla_prep(h2, pos3, w_m, w_uq, w_ukv, vecs, *, tm=512):
    bsz, s, d = h2.shape
    tm = min(tm, s)
    nh = MLA_HEADS
    qk_dim = MLA_NOPE + LANES
    full = lambda b, i: (0, 0)
    vec_specs = [pl.BlockSpec(v.shape, full) for v in vecs]
    return pl.pallas_call(
        _mla_prep_kernel,
        grid=(bsz, s // tm),
        in_specs=[pl.BlockSpec((1, tm, d), lambda b, i: (b, i, 0)),
                  pl.BlockSpec((1, 1, tm), lambda b, i: (b, 0, i)),
                  pl.BlockSpec(w_m.shape, full),
                  pl.BlockSpec(w_uq.shape, full),
                  pl.BlockSpec(w_ukv.shape, full)] + vec_specs,
        out_specs=[pl.BlockSpec((1, nh, qk_dim, tm), lambda b, i: (b, 0, 0, i)),
                   pl.BlockSpec((1, nh, tm, qk_dim), lambda b, i: (b, 0, i, 0)),
                   pl.BlockSpec((1, nh, MLA_V, tm), lambda b, i: (b, 0, 0, i))],
        out_shape=[jax.ShapeDtypeStruct((bsz, nh, qk_dim, s), BF16),
                   jax.ShapeDtypeStruct((bsz, nh, s, qk_dim), BF16),
                   jax.ShapeDtypeStruct((bsz, nh, MLA_V, s), BF16)],
        compiler_params=pltpu.CompilerParams(
            dimension_semantics=("parallel", "parallel"),
            vmem_limit_bytes=VMEM_LIMIT),
        name="mla_prep",
    )(h2, pos3, w_m, w_uq, w_ukv, *vecs)


def _attn_kernel(mode_ref, qt_ref, k_ref, vt_ref, nw_ref, o_ref, s_scr, m_scr, l_scr, acc_scr, *,
                 tq, tk):
    i = pl.program_id(2)
    per_q = tq // tk
    kc = lax.broadcasted_iota(jnp.int32, (tk, tq), 0) // ATTN_CHUNK
    qc = lax.broadcasted_iota(jnp.int32, (tk, tq), 1) // ATTN_CHUNK
    visible = qc >= kc

    def key_tile(hh, j):
        start = pl.multiple_of(j * tk, tk)
        return k_ref[0, hh, pl.ds(start, tk), :], vt_ref[0, hh, :, pl.ds(start, tk)]

    def finish():
        for hh in range(2):
            o = (acc_scr[hh] / l_scr[hh]).T
            o = o * _rms_scale(o, MLA_V) * nw_ref[...]
            o_ref[0, :, hh * MLA_V:(hh + 1) * MLA_V] = o.astype(o_ref.dtype)

    l_scr[...] = jnp.zeros(l_scr.shape, F32)
    acc_scr[...] = jnp.zeros(acc_scr.shape, F32)

    @pl.when(mode_ref[0] == 1)
    def _fixed_reference():
        def tile(hh, j, q0=0, mask=None):
            kj, vj = key_tile(hh, j)
            st = _dot(kj, qt_ref[0, hh, :, q0:])
            if mask is not None:
                st = jnp.where(mask, st, -jnp.inf)
            p = jnp.exp2(st)
            l_scr[hh, :, q0:] += jnp.sum(p, axis=0, keepdims=True)
            acc_scr[hh, :, q0:] += _dot(vj, p.astype(BF16))

        def body(jj, _):
            for r in range(per_q):
                tile(0, jj * per_q + r)
                tile(1, jj * per_q + r)
            return 0

        lax.fori_loop(0, i, body, 0)
        for r in range(per_q):
            for hh in range(2):
                tile(hh, i * per_q + r, r * tk, visible[:, :tq - r * tk])
        finish()

    @pl.when(mode_ref[0] == 0)
    def _online():
        def scores(hh, j, q0=0):
            kj, _ = key_tile(hh, j)
            s_scr[hh, :, :tq - q0] = _dot(kj, qt_ref[0, hh, :, q0:])

        def accumulate(hh, j, q0=0, mask=None):
            w = tq - q0
            if mask is not None:
                s_scr[hh, :, :w] = jnp.where(mask, s_scr[hh, :, :w], -jnp.inf)
            m = m_scr[hh, :, q0:]
            m_new = jnp.maximum(m, jnp.max(s_scr[hh, :, :w], axis=0, keepdims=True))
            alpha = jnp.exp2(m - m_new)
            p = jnp.exp2(s_scr[hh, :, :w] - m_new)
            m_scr[hh, :, q0:] = m_new
            l_scr[hh, :, q0:] = alpha * l_scr[hh, :, q0:] + jnp.sum(p, axis=0, keepdims=True)
            _, vj = key_tile(hh, j)
            acc_scr[hh, :, q0:] = alpha * acc_scr[hh, :, q0:] + _dot(vj, p.astype(BF16))

        m_scr[...] = jnp.full(m_scr.shape, -jnp.inf, F32)
        scores(0, 0)

        def body(jj, _):
            for r in range(per_q):
                j = jj * per_q + r
                scores(1, j)
                accumulate(0, j)
                scores(0, j + 1)
                accumulate(1, j)
            return 0

        lax.fori_loop(0, i, body, 0)
        for r in range(per_q):
            q0 = r * tk
            j = i * per_q + r
            scores(1, j, q0)
            accumulate(0, j, q0, visible[:, :tq - q0])
            if r < per_q - 1:
                scores(0, j + 1, q0 + tk)
            accumulate(1, j, q0, visible[:, :tq - q0])
        finish()


def _attn(mode, qt, k, vt, out_norm_w, *, tq=1024, tk=512):
    bsz, nh, qk_dim, s = qt.shape
    tq = min(tq, s)
    tk = min(tk, tq)
    heads = 2
    return pl.pallas_call(
        functools.partial(_attn_kernel, tq=tq, tk=tk),
        grid=(bsz, nh // heads, s // tq),
        in_specs=[pl.BlockSpec(memory_space=pltpu.SMEM),
                  pl.BlockSpec((1, heads, qk_dim, tq), lambda b, h, i: (b, h, 0, i)),
                  pl.BlockSpec((1, heads, s, qk_dim), lambda b, h, i: (b, h, 0, 0)),
                  pl.BlockSpec((1, heads, MLA_V, s), lambda b, h, i: (b, h, 0, 0)),
                  pl.BlockSpec((1, MLA_V), lambda b, h, i: (0, 0))],
        out_specs=pl.BlockSpec((1, tq, heads * MLA_V), lambda b, h, i: (b, i, h)),
        out_shape=jax.ShapeDtypeStruct((bsz, s, nh * MLA_V), BF16),
        scratch_shapes=[pltpu.VMEM((heads, tk, tq), F32),
                        pltpu.VMEM((heads, 1, tq), F32),
                        pltpu.VMEM((heads, 1, tq), F32),
                        pltpu.VMEM((heads, MLA_V, tq), F32)],
        compiler_params=pltpu.CompilerParams(
            dimension_semantics=("parallel", "parallel", "arbitrary"),
            vmem_limit_bytes=VMEM_LIMIT),
        name="attn",
    )(mode, qt, k, vt, out_norm_w)


def _pad_lanes(w, width=LANES):
    return jnp.pad(w, [(0, 0)] * (w.ndim - 1) + [(0, width - w.shape[-1])])


def _swap_halves(w):
    half = w.shape[-1] // 2
    return jnp.concatenate([w[..., half:], w[..., :half]], axis=-1)


def _layer(x, mods, pos3, w_ffn1_in, w_ffn1_out, w_in, conv_w, a_log, dt_bias, gdn_norm_w,
           q_norm_w, w_uq, kv_norm_w, w_ukv, qn_q_nope, qn_q_rope, qn_k_nope, qn_k_rope,
           out_norm_w, w_out, w_ffn2_in, w_ffn2_out):
    sh1, s1, g1, sh2, s2, g2, sh3, s3, g3 = mods
    nh = MLA_HEADS

    o_gz = 2 * GDN_HEADS * GDN_DK + 2 * GDN_HEADS * GDN_DV
    o_ab = o_gz + 2 * GDN_HEADS
    o_cq = o_ab + MLA_Q_LORA
    o_ckv = o_cq + MLA_KV_LORA
    w_gdn = jnp.concatenate([w_in[:, :o_gz], _pad_lanes(w_in[:, o_gz:o_ab])], axis=1).astype(BF16)
    w_kr = w_in[:, o_ckv:]
    w_m = jnp.concatenate([w_in[:, o_ab:o_ckv], _pad_lanes(w_kr), _pad_lanes(_swap_halves(w_kr))],
                          axis=1).astype(BF16)
    per_q = MLA_NOPE + MLA_ROPE
    uq_parts = []
    for hh in range(nh):
        wn = w_uq[:, hh * per_q:hh * per_q + MLA_NOPE]
        wr = w_uq[:, hh * per_q + MLA_NOPE:(hh + 1) * per_q]
        uq_parts += [wn, _pad_lanes(wr), _pad_lanes(_swap_halves(wr))]
    w_uq_p = jnp.concatenate(uq_parts, axis=1).astype(BF16)
    w_ukv_b = w_ukv.astype(BF16)

    half = MLA_ROPE // 2
    inv_freq = ROPE_BASE ** (-jnp.arange(half, dtype=F32) / half)
    invf = inv_freq.reshape(half, 1)
    eye = jnp.eye(half, dtype=F32)
    cos_sel = _pad_lanes(jnp.concatenate([eye, eye], axis=1)).astype(BF16)
    sin_sel = _pad_lanes(jnp.concatenate([-eye, eye], axis=1)).astype(BF16)
    row = lambda v: v.reshape(1, -1)

    def sq_norm_bound(w_nope, w_rope):
        return MLA_NOPE * jnp.max(w_nope * w_nope) + MLA_ROPE * jnp.max(w_rope * w_rope)
    q_scale = (MLA_NOPE + MLA_ROPE) ** -0.5 * LOG2_E
    bound = (SCORE_BOUND_MARGIN * q_scale
             * jnp.sqrt(sq_norm_bound(qn_q_nope, qn_q_rope) * sq_norm_bound(qn_k_nope, qn_k_rope)))
    use_bound = bound < MAX_FIXED_REFERENCE
    mode = use_bound.astype(jnp.int32).reshape(1)
    ext_lane = (jnp.arange(LANES) == MLA_ROPE).astype(F32).reshape(1, LANES)
    q_ext = ext_lane * jnp.where(use_bound, -bound, 0.0)
    k_ext = ext_lane
    vecs = [row(q_norm_w), row(kv_norm_w),
            row(qn_q_nope), row(_pad_lanes(qn_q_rope)), row(_pad_lanes(_swap_halves(qn_q_rope))),
            row(qn_k_nope), row(_pad_lanes(qn_k_rope)), row(_pad_lanes(_swap_halves(qn_k_rope))),
            invf, cos_sel, sin_sel, q_ext, k_ext]

    x1, h2 = _ffn(x, sh1, s1, g1, w_ffn1_in.astype(BF16), w_ffn1_out.astype(BF16), nxt=(sh2, s2))

    o_a = _gdn(h2, w_gdn, conv_w, row(_pad_lanes(a_log)), row(_pad_lanes(dt_bias)), row(gdn_norm_w))
    qt, k, vt = _mla_prep(h2, pos3, w_m, w_uq_p, w_ukv_b, vecs)
    o_b = _attn(mode, qt, k, vt, row(out_norm_w))

    return _ffn(x1, sh3, s3, g3, w_ffn2_in.astype(BF16), w_ffn2_out.astype(BF16),
                mix=(o_a, o_b, w_out.astype(BF16), g2))


def kernel(x, c, positions, w_ada, b_ada, ffn1_w_in, ffn1_w_out, w_in, gdn_conv_w, gdn_a_log, gdn_dt_bias, gdn_norm_w, mla_q_norm_w, mla_w_uq, mla_kv_norm_w, mla_w_ukv, qkn_q_nope, qkn_q_rope, qkn_k_nope, qkn_k_rope, mla_out_norm_w, w_out, ffn2_w_in, ffn2_w_out):
    bsz, s, d = x.shape
    pos3 = positions.reshape(bsz, 1, s)
    for l in range(w_ada.shape[0]):
        mod = _mod(c, w_ada[l], b_ada[l])
        mods = [m.reshape(bsz, 1, d) for m in jnp.split(mod, N_MOD, axis=-1)]
        x = _layer(x, mods, pos3, ffn1_w_in[l], ffn1_w_out[l], w_in[l], gdn_conv_w[l],
                   gdn_a_log[l], gdn_dt_bias[l], gdn_norm_w[l], mla_q_norm_w[l], mla_w_uq[l],
                   mla_kv_norm_w[l], mla_w_ukv[l], qkn_q_nope[l], qkn_q_rope[l], qkn_k_nope[l],
                   qkn_k_rope[l], mla_out_norm_w[l], w_out[l], ffn2_w_in[l], ffn2_w_out[l])
    return x
```

```python
import functools

import jax
import jax.numpy as jnp
from jax import lax
from jax.experimental import pallas as pl
from jax.experimental.pallas import tpu as pltpu

F32 = jnp.float32
BF16 = jnp.bfloat16

EPS = 1e-6
ATTN_CHUNK = 64
GDN_HEADS = 4
GDN_DK = 128
GDN_DV = 128
CONV_K = 4
MLA_HEADS = 4
MLA_NOPE = 128
MLA_ROPE = 64
MLA_V = 128
MLA_Q_LORA = 384
MLA_KV_LORA = 256
ROPE_BASE = 10000.0
N_MOD = 9
LOG2_E = 1.4426950408889634
MAX_FIXED_REFERENCE = 60.0
SCORE_BOUND_MARGIN = 1.0625

LANES = 128
SUBLANES = 8
VMEM_LIMIT = 56 * 1024 * 1024

GDN_CHUNK = 128
GDN_PROJ_BLOCK = 256
GDN_CONV_PIECE = 512
FFN_SUBTILES = 4


def _dot(a, b):
    return jnp.dot(a, b, preferred_element_type=F32)


def _dot_nt(a, b):
    return lax.dot_general(a, b, (((1,), (1,)), ((), ())), preferred_element_type=F32)


def _dot_tn(a, b):
    return lax.dot_general(a, b, (((0,), (0,)), ((), ())), preferred_element_type=F32)


def _silu(x):
    return x * jax.nn.sigmoid(x)


def _rms_scale(x, n):
    return lax.rsqrt(jnp.sum(x * x, axis=-1, keepdims=True) * (1.0 / n) + EPS)


def _mod_kernel(c_ref, w_ref, b_ref, o_ref):
    sc = _silu(c_ref[...])
    o_ref[...] = jnp.dot(sc, w_ref[...], precision=lax.Precision.HIGHEST,
                         preferred_element_type=F32) + b_ref[...]


def _mod(c, w_ada, b_ada):
    bsz, d = c.shape
    n = w_ada.shape[1]
    tn = 1024
    return pl.pallas_call(
        _mod_kernel,
        grid=(n // tn,),
        in_specs=[pl.BlockSpec((bsz, d), lambda j: (0, 0)),
                  pl.BlockSpec((d, tn), lambda j: (0, j)),
                  pl.BlockSpec((1, tn), lambda j: (0, j))],
        out_specs=pl.BlockSpec((bsz, tn), lambda j: (0, j)),
        out_shape=jax.ShapeDtypeStruct((bsz, n), F32),
        compiler_params=pltpu.CompilerParams(dimension_semantics=("arbitrary",),
                                             vmem_limit_bytes=VMEM_LIMIT),
        name="mod",
    )(c, w_ada, b_ada.reshape(1, n))


def _ffn_kernel(*refs, has_mix, emit_next, d_model, d_ff, tf, n_sub):
    it = iter(refs)
    x_ref = next(it)
    if has_mix:
        oa_ref, ob_ref, wmix_ref, gmix_ref = next(it), next(it), next(it), next(it)
    shift_ref, scale_ref, gate_ref = next(it), next(it), next(it)
    win_ref, wout_ref = next(it), next(it)
    if emit_next:
        nshift_ref, nscale_ref = next(it), next(it)
    out_ref = next(it)
    if emit_next:
        hn_ref = next(it)
    h_sc, a_sc = next(it), next(it)

    tm = x_ref.shape[1]
    ts = tm // n_sub
    subs = [slice(r * ts, (r + 1) * ts) for r in range(n_sub)]

    def prologue(rows):
        x = x_ref[0, rows]
        if has_mix:
            half = oa_ref.shape[-1]
            y = (_dot(oa_ref[0, rows], wmix_ref[:half, :])
                 + _dot(ob_ref[0, rows], wmix_ref[half:, :]))
            x = x + gmix_ref[0] * y
            out_ref[0, rows] = x
        h_sc[rows] = (x * _rms_scale(x, d_model) * (1.0 + scale_ref[0])
                      + shift_ref[0]).astype(BF16)

    def hidden(rows):
        h = h_sc[rows]
        for c in range(d_ff // tf):
            g = _dot(h, win_ref[:, c * tf:(c + 1) * tf])
            u = _dot(h, win_ref[:, d_ff + c * tf:d_ff + (c + 1) * tf])
            a_sc[rows, c * tf:(c + 1) * tf] = (_silu(g) * u).astype(BF16)

    def epilogue(rows):
        x = out_ref[0, rows] if has_mix else x_ref[0, rows]
        out = x + 0.5 * gate_ref[0] * _dot(a_sc[rows], wout_ref[...])
        out_ref[0, rows] = out
        if emit_next:
            hn = out * _rms_scale(out, d_model) * (1.0 + nscale_ref[0]) + nshift_ref[0]
            hn_ref[0, rows] = hn.astype(BF16)

    prologue(subs[0])
    for r in range(n_sub):
        if r + 1 < n_sub:
            prologue(subs[r + 1])
        hidden(subs[r])
        epilogue(subs[r])


def _resident(shape, index_map):
    return pl.BlockSpec(shape, index_map, pipeline_mode=pl.Buffered(1))


def _ffn(x, shift, scale, gate, w_in_bf, w_out_bf, *, mix=None, nxt=None, tm=1024, tf=256):
    bsz, s, d = x.shape
    ff = w_out_bf.shape[0]
    tm = min(tm, s)
    assert ff % tf == 0 and s % tm == 0 and tm % (FFN_SUBTILES * 2 * SUBLANES) == 0
    has_mix = mix is not None
    emit_next = nxt is not None

    row = lambda b, i: (b, i, 0)
    vec = lambda b, i: (b, 0, 0)
    const = lambda b, i: (0, 0)
    args = [x]
    in_specs = [pl.BlockSpec((1, tm, d), row)]
    if has_mix:
        o_a, o_b, w_mix, g_mix = mix
        args += [o_a, o_b, w_mix, g_mix]
        in_specs += [pl.BlockSpec((1, tm, o_a.shape[-1]), row),
                     pl.BlockSpec((1, tm, o_b.shape[-1]), row),
                     _resident(w_mix.shape, const),
                     pl.BlockSpec((1, 1, d), vec)]
    args += [shift, scale, gate, w_in_bf, w_out_bf]
    in_specs += [pl.BlockSpec((1, 1, d), vec)] * 3
    in_specs += [_resident(w_in_bf.shape, const), _resident(w_out_bf.shape, const)]
    if emit_next:
        args += list(nxt)
        in_specs += [pl.BlockSpec((1, 1, d), vec)] * 2
    out_shape = [jax.ShapeDtypeStruct((bsz, s, d), F32)]
    out_specs = [pl.BlockSpec((1, tm, d), row)]
    if emit_next:
        out_shape.append(jax.ShapeDtypeStruct((bsz, s, d), BF16))
        out_specs.append(pl.BlockSpec((1, tm, d), row))

    res = pl.pallas_call(
        functools.partial(_ffn_kernel, has_mix=has_mix, emit_next=emit_next, d_model=d,
                          d_ff=ff, tf=tf, n_sub=FFN_SUBTILES),
        grid=(bsz, s // tm),
        in_specs=in_specs,
        out_specs=out_specs,
        out_shape=out_shape,
        scratch_shapes=[pltpu.VMEM((tm, d), BF16), pltpu.VMEM((tm, ff), BF16)],
        compiler_params=pltpu.CompilerParams(
            dimension_semantics=("parallel", "parallel"),
            vmem_limit_bytes=VMEM_LIMIT),
        name="ffn_mix" if has_mix else "ffn",
    )(*args)
    return res if emit_next else res[0]


def _unit_lower_inverses(a_mats, ii, jj):
    n = a_mats[0].shape[0]
    eye = (ii == jj).astype(F32)

    def same_block(size):
        return (ii // size) == (jj // size)

    base = same_block(SUBLANES)
    d1 = [jnp.where(base, a, 0.0) for a in a_mats]
    d1b = [d.astype(BF16) for d in d1]
    d2 = [_dot(d, d) for d in d1b]
    d2b = [d.astype(BF16) for d in d2]
    d3 = [_dot(a, b) for a, b in zip(d1b, d2b)]
    d4b = [_dot(d, d).astype(BF16) for d in d2b]
    p1 = [eye - a + b - c for a, b, c in zip(d1, d2, d3)]
    ts = [p + _dot(p.astype(BF16), d) for p, d in zip(p1, d4b)]
    size = SUBLANES
    while size < n:
        below = same_block(2 * size) & jnp.logical_not(same_block(size))
        offs = [jnp.where(below, a, 0.0).astype(BF16) for a in a_mats]
        tbs = [t.astype(BF16) for t in ts]
        xs = [_dot(o, t).astype(BF16) for o, t in zip(offs, tbs)]
        ts = [t - _dot(tb, x) for t, tb, x in zip(ts, tbs, xs)]
        size *= 2
    return ts


def _split3(x):
    hi = x.astype(BF16)
    r1 = x - hi.astype(F32)
    mid = r1.astype(BF16)
    lo = (r1 - mid.astype(F32)).astype(BF16)
    return hi, mid, lo


def _gdn_kernel(h0_ref, hn_ref, wproj_ref, convw_ref, alog_ref, dtb_ref, normw_ref,
                o_ref, proj_scr, xbuf, state, *, chunk):
    nb, _, d = hn_ref.shape
    nh, dk, dv = GDN_HEADS, GDN_DK, GDN_DV
    qk_w = nh * dk
    conv_w = 2 * qk_w + nh * dv
    z_off = conv_w
    ab_off = conv_w + nh * dv
    pad = SUBLANES
    probs = [(b, hh) for b in range(nb) for hh in range(nh)]

    @pl.when(pl.program_id(0) == 0)
    def _():
        state[...] = jnp.zeros_like(state)
        xbuf[:, 0:pad, :] = jnp.zeros((nb, pad, conv_w), F32)
        proj_scr[...] = _dot(h0_ref[...].reshape(nb * chunk, d), wproj_ref[...])

    for b in range(nb):
        xbuf[b, pad:pad + chunk, :] = proj_scr[b * chunk:(b + 1) * chunk, :conv_w]
    ab = proj_scr[:, ab_off:ab_off + LANES]
    z_gates = [_silu(proj_scr[b * chunk:(b + 1) * chunk, z_off + hh * dv:z_off + (hh + 1) * dv])
               for b, hh in probs]
    h_next = hn_ref[...].reshape(nb * chunk, d)
    n_cols = proj_scr.shape[1]
    col_blocks = [(c0, min(c0 + GDN_PROJ_BLOCK, n_cols))
                  for c0 in range(0, n_cols, GDN_PROJ_BLOCK)]

    g_all = -jnp.exp(alog_ref[...]) * jax.nn.softplus(ab + dtb_ref[...])
    beta_all = jax.nn.sigmoid(ab)

    ii = lax.broadcasted_iota(jnp.int32, (chunk, chunk), 0)
    jj = lax.broadcasted_iota(jnp.int32, (chunk, chunk), 1)
    incl = ii >= jj
    strict = ii > jj
    tri = incl.astype(BF16)
    cw = convw_ref[...]

    n_pieces = nb * (conv_w // GDN_CONV_PIECE)
    qkv, gcum, gcum_t = [], [], []
    piece = 0
    for b in range(nb):
        rows = slice(b * chunk, (b + 1) * chunk)
        parts = []
        for p0 in range(0, conv_w, GDN_CONV_PIECE):
            for c0, c1 in col_blocks[piece::n_pieces]:
                proj_scr[:, c0:c1] = _dot(h_next, wproj_ref[:, c0:c1])
            piece += 1
            cols = slice(p0, p0 + GDN_CONV_PIECE)
            conv = xbuf[b, pad:pad + chunk, cols] * cw[CONV_K - 1:CONV_K, cols]
            for tap in range(1, CONV_K):
                conv = conv + (xbuf[b, pad - tap:pad - tap + chunk, cols]
                               * cw[CONV_K - 1 - tap:CONV_K - tap, cols])
            parts.append(_silu(conv))
        xbuf[b, 0:pad, :] = xbuf[b, chunk:chunk + pad, :]
        qkv.append(jnp.concatenate(parts, axis=1))
        g_hi, g_mid, g_lo = _split3(g_all[rows, :])
        gc = _dot(tri, g_hi) + _dot(tri, g_mid) + _dot(tri, g_lo)
        gcum.append(gc)
        gcum_t.append(gc.T)

    qs, ks, vs, kbs, g_cols, b_cols, decays = [], [], [], [], [], [], []
    for b, hh in probs:
        q = qkv[b][:, hh * dk:(hh + 1) * dk]
        k = qkv[b][:, qk_w + hh * dk:qk_w + (hh + 1) * dk]
        vs.append(qkv[b][:, 2 * qk_w + hh * dv:2 * qk_w + (hh + 1) * dv])
        q = q * (lax.rsqrt(jnp.sum(q * q, axis=-1, keepdims=True) + EPS) * (dk ** -0.5))
        k = k * lax.rsqrt(jnp.sum(k * k, axis=-1, keepdims=True) + EPS)
        qs.append(q)
        ks.append(k)
        kbs.append(k.astype(BF16))
        g_col = gcum[b][:, hh:hh + 1]
        g_row = gcum_t[b][hh:hh + 1, :]
        g_cols.append(g_col)
        b_cols.append(beta_all[b * chunk:(b + 1) * chunk, nh + hh:nh + hh + 1])
        decays.append(jnp.exp(jnp.where(incl, g_col - g_row, -jnp.inf)))

    kks = [_dot_nt(kb, kb) for kb in kbs]
    qks = [(_dot_nt(q.astype(BF16), kb) * dec).astype(BF16) for q, kb, dec in zip(qs, kbs, decays)]
    a_mats = [jnp.where(strict, bc * kk * dec, 0.0) for bc, kk, dec in zip(b_cols, kks, decays)]
    t_invs = _unit_lower_inverses(a_mats, ii, jj)

    e_gs = [jnp.exp(g) for g in g_cols]
    rhs_us = [v * bc for v, bc in zip(vs, b_cols)]
    rhs_ws = [(k * (bc * eg)).astype(BF16) for k, bc, eg in zip(ks, b_cols, e_gs)]
    g_lasts = [g[chunk - 1:chunk, :] for g in g_cols]
    q_decs = [(q * eg).astype(BF16) for q, eg in zip(qs, e_gs)]
    k_dec_ts = [(k * jnp.exp(gl - g)).T.astype(BF16) for k, gl, g in zip(ks, g_lasts, g_cols)]

    s_olds = [state[b, hh] for b, hh in probs]
    s_bs = [s.astype(BF16) for s in s_olds]
    resid = [(ru - _dot(rw, sb)).astype(BF16) for ru, rw, sb in zip(rhs_us, rhs_ws, s_bs)]
    v_nbs = [_dot(t.astype(BF16), r).astype(BF16) for t, r in zip(t_invs, resid)]
    outs = [_dot(jnp.concatenate([qd, qk], axis=1), jnp.concatenate([sb, vn], axis=0))
            for qd, sb, qk, vn in zip(q_decs, s_bs, qks, v_nbs)]
    for (b, hh), s_old, gl, kdt, vn in zip(probs, s_olds, g_lasts, k_dec_ts, v_nbs):
        state[b, hh] = s_old * jnp.exp(gl) + _dot(kdt, vn)
    for (b, hh), o, zg in zip(probs, outs, z_gates):
        o = o * _rms_scale(o, dv) * normw_ref[...] * zg
        o_ref[b, :, hh * dv:(hh + 1) * dv] = o.astype(o_ref.dtype)


def _gdn(h2, w_proj, conv_w, a_log, dt_bias, norm_w):
    bsz, s, d = h2.shape
    chunk = min(GDN_CHUNK, s)
    n_chunks = s // chunk
    nh, dv = GDN_HEADS, GDN_DV
    cw = conv_w.shape[1]
    full = lambda j: (0, 0)
    return pl.pallas_call(
        functools.partial(_gdn_kernel, chunk=chunk),
        grid=(n_chunks,),
        in_specs=[pl.BlockSpec((bsz, chunk, d), lambda j: (0, 0, 0)),
                  pl.BlockSpec((bsz, chunk, d), lambda j: (0, jnp.minimum(j + 1, n_chunks - 1), 0)),
                  pl.BlockSpec(w_proj.shape, full),
                  pl.BlockSpec(conv_w.shape, full),
                  pl.BlockSpec((1, LANES), full),
                  pl.BlockSpec((1, LANES), full),
                  pl.BlockSpec((1, dv), full)],
        out_specs=pl.BlockSpec((bsz, chunk, nh * dv), lambda j: (0, j, 0)),
        out_shape=jax.ShapeDtypeStruct((bsz, s, nh * dv), BF16),
        scratch_shapes=[pltpu.VMEM((bsz * chunk, w_proj.shape[1]), F32),
                        pltpu.VMEM((bsz, chunk + SUBLANES, cw), F32),
                        pltpu.VMEM((bsz, nh, GDN_DK, dv), F32)],
        compiler_params=pltpu.CompilerParams(
            dimension_semantics=("arbitrary",),
            vmem_limit_bytes=VMEM_LIMIT),
        name="gdn",
    )(h2, h2, w_proj, conv_w, a_log, dt_bias, norm_w)


def _mla_prep_kernel(h_ref, pos_ref, wm_ref, wuq_ref, wukv_ref, qnw_ref, kvnw_ref,
                     wqn_ref, wqr_ref, wqrs_ref, wkn_ref, wkr_ref, wkrs_ref,
                     invf_ref, csel_ref, ssel_ref, qext_ref, kext_ref, qt_ref, k_ref, vt_ref):
    nh = MLA_HEADS
    scale = (MLA_NOPE + MLA_ROPE) ** -0.5 * LOG2_E
    h = h_ref[0]
    lat = _dot(h, wm_ref[...])
    o1 = MLA_Q_LORA
    o2 = o1 + MLA_KV_LORA
    cq = lat[:, :o1]
    ckv = lat[:, o1:o2]
    kr = lat[:, o2:o2 + LANES]
    krs = lat[:, o2 + LANES:o2 + 2 * LANES]
    cqn = (cq * _rms_scale(cq, MLA_Q_LORA) * qnw_ref[...]).astype(BF16)
    ckvn = (ckv * _rms_scale(ckv, MLA_KV_LORA) * kvnw_ref[...]).astype(BF16)
    qf = _dot(cqn, wuq_ref[...])
    kvf = _dot(ckvn, wukv_ref[...])

    ang_t = invf_ref[...] * pos_ref[0].astype(F32)

    def spread(table_t, sel_ref):
        parts = [_dot_tn(part, sel_ref[...]) for part in _split3(table_t)]
        return (parts[0] + parts[1]) + parts[2]

    cos_p = spread(jnp.cos(ang_t), csel_ref)
    sin_p = spread(jnp.sin(ang_t), ssel_ref)

    k_rope = ((kr * wkr_ref[...] * cos_p + krs * wkrs_ref[...] * sin_p) * _rms_scale(kr, MLA_ROPE)
              + kext_ref[...])
    qw = MLA_NOPE + 2 * LANES
    kw = MLA_NOPE + MLA_V
    for hh in range(nh):
        qn = qf[:, hh * qw:hh * qw + MLA_NOPE]
        qr = qf[:, hh * qw + MLA_NOPE:hh * qw + MLA_NOPE + LANES]
        qrs = qf[:, hh * qw + MLA_NOPE + LANES:(hh + 1) * qw]
        qn = qn * _rms_scale(qn, MLA_NOPE) * (wqn_ref[...] * scale)
        q_rope = ((qr * wqr_ref[...] * cos_p + qrs * wqrs_ref[...] * sin_p)
                  * (_rms_scale(qr, MLA_ROPE) * scale)) + qext_ref[...]
        qt_ref[0, hh] = jnp.concatenate([qn, q_rope], axis=1).T.astype(BF16)
        kn = kvf[:, hh * kw:hh * kw + MLA_NOPE]
        v = kvf[:, hh * kw + MLA_NOPE:(hh + 1) * kw]
        kn = kn * _rms_scale(kn, MLA_NOPE) * wkn_ref[...]
        k_ref[0, hh] = jnp.concatenate([kn, k_rope], axis=1).astype(BF16)
        vt_ref[0, hh] = v.T.astype(BF16)


def _mla_prep(h2, pos3, w_m, w_uq, w_ukv, vecs, *, tm=1024):
    bsz, s, d = h2.shape
    tm = min(tm, s)
    nh = MLA_HEADS
    qk_dim = MLA_NOPE + LANES
    full = lambda b, i: (0, 0)
    vec_specs = [pl.BlockSpec(v.shape, full) for v in vecs]
    return pl.pallas_call(
        _mla_prep_kernel,
        grid=(bsz, s // tm),
        in_specs=[pl.BlockSpec((1, tm, d), lambda b, i: (b, i, 0)),
                  pl.BlockSpec((1, 1, tm), lambda b, i: (b, 0, i)),
                  pl.BlockSpec(w_m.shape, full),
                  pl.BlockSpec(w_uq.shape, full),
                  pl.BlockSpec(w_ukv.shape, full)] + vec_specs,
        out_specs=[pl.BlockSpec((1, nh, qk_dim, tm), lambda b, i: (b, 0, 0, i)),
                   pl.BlockSpec((1, nh, tm, qk_dim), lambda b, i: (b, 0, i, 0)),
                   pl.BlockSpec((1, nh, MLA_V, tm), lambda b, i: (b, 0, 0, i))],
        out_shape=[jax.ShapeDtypeStruct((bsz, nh, qk_dim, s), BF16),
                   jax.ShapeDtypeStruct((bsz, nh, s, qk_dim), BF16),
                   jax.ShapeDtypeStruct((bsz, nh, MLA_V, s), BF16)],
        compiler_params=pltpu.CompilerParams(
            dimension_semantics=("parallel", "parallel"),
            vmem_limit_bytes=VMEM_LIMIT),
        name="mla_prep",
    )(h2, pos3, w_m, w_uq, w_ukv, *vecs)


def _attn_kernel(mode_ref, qt_ref, k_ref, vt_ref, nw_ref, o_ref, s_scr, m_scr, l_scr, acc_scr, *,
                 tq, tk):
    i = pl.program_id(2)
    per_q = tq // tk
    kc = lax.broadcasted_iota(jnp.int32, (tk, tq), 0) // ATTN_CHUNK
    qc = lax.broadcasted_iota(jnp.int32, (tk, tq), 1) // ATTN_CHUNK
    visible = qc >= kc

    def key_tile(hh, j):
        start = pl.multiple_of(j * tk, tk)
        return k_ref[0, hh, pl.ds(start, tk), :], vt_ref[0, hh, :, pl.ds(start, tk)]

    def finish():
        for hh in range(2):
            o = (acc_scr[hh] / l_scr[hh]).T
            o = o * _rms_scale(o, MLA_V) * nw_ref[...]
            o_ref[0, :, hh * MLA_V:(hh + 1) * MLA_V] = o.astype(o_ref.dtype)

    l_scr[...] = jnp.zeros(l_scr.shape, F32)
    acc_scr[...] = jnp.zeros(acc_scr.shape, F32)

    @pl.when(mode_ref[0] == 1)
    def _fixed_reference():
        def tile(hh, j, q0=0, mask=None):
            kj, vj = key_tile(hh, j)
            st = _dot(kj, qt_ref[0, hh, :, q0:])
            if mask is not None:
                st = jnp.where(mask, st, -jnp.inf)
            p = jnp.exp2(st)
            l_scr[hh, :, q0:] += jnp.sum(p, axis=0, keepdims=True)
            acc_scr[hh, :, q0:] += _dot(vj, p.astype(BF16))

        def body(jj, _):
            for r in range(per_q):
                tile(0, jj * per_q + r)
                tile(1, jj * per_q + r)
            return 0

        lax.fori_loop(0, i, body, 0)
        for r in range(per_q):
            for hh in range(2):
                tile(hh, i * per_q + r, r * tk, visible[:, :tq - r * tk])
        finish()

    @pl.when(mode_ref[0] == 0)
    def _online():
        def scores(hh, j, q0=0):
            kj, _ = key_tile(hh, j)
            s_scr[hh, :, :tq - q0] = _dot(kj, qt_ref[0, hh, :, q0:])

        def accumulate(hh, j, q0=0, mask=None):
            w = tq - q0
            if mask is not None:
                s_scr[hh, :, :w] = jnp.where(mask, s_scr[hh, :, :w], -jnp.inf)
            m = m_scr[hh, :, q0:]
            m_new = jnp.maximum(m, jnp.max(s_scr[hh, :, :w], axis=0, keepdims=True))
            alpha = jnp.exp2(m - m_new)
            p = jnp.exp2(s_scr[hh, :, :w] - m_new)
            m_scr[hh, :, q0:] = m_new
            l_scr[hh, :, q0:] = alpha * l_scr[hh, :, q0:] + jnp.sum(p, axis=0, keepdims=True)
            _, vj = key_tile(hh, j)
            acc_scr[hh, :, q0:] = alpha * acc_scr[hh, :, q0:] + _dot(vj, p.astype(BF16))

        m_scr[...] = jnp.full(m_scr.shape, -jnp.inf, F32)
        scores(0, 0)

        def body(jj, _):
            for r in range(per_q):
                j = jj * per_q + r
                scores(1, j)
                accumulate(0, j)
                scores(0, j + 1)
                accumulate(1, j)
            return 0

        lax.fori_loop(0, i, body, 0)
        for r in range(per_q):
            q0 = r * tk
            j = i * per_q + r
            scores(1, j, q0)
            accumulate(0, j, q0, visible[:, :tq - q0])
            if r < per_q - 1:
                scores(0, j + 1, q0 + tk)
            accumulate(1, j, q0, visible[:, :tq - q0])
        finish()


def _attn(mode, qt, k, vt, out_norm_w, *, tq=2048, tk=512):
    bsz, nh, qk_dim, s = qt.shape
    tq = min(tq, s)
    tk = min(tk, tq)
    heads = 2
    return pl.pallas_call(
        functools.partial(_attn_kernel, tq=tq, tk=tk),
        grid=(bsz, nh // heads, s // tq),
        in_specs=[pl.BlockSpec(memory_space=pltpu.SMEM),
                  pl.BlockSpec((1, heads, qk_dim, tq), lambda b, h, i: (b, h, 0, i)),
                  pl.BlockSpec((1, heads, s, qk_dim), lambda b, h, i: (b, h, 0, 0)),
                  pl.BlockSpec((1, heads, MLA_V, s), lambda b, h, i: (b, h, 0, 0)),
                  pl.BlockSpec((1, MLA_V), lambda b, h, i: (0, 0))],
        out_specs=pl.BlockSpec((1, tq, heads * MLA_V), lambda b, h, i: (b, i, h)),
        out_shape=jax.ShapeDtypeStruct((bsz, s, nh * MLA_V), BF16),
        scratch_shapes=[pltpu.VMEM((heads, tk, tq), F32),
                        pltpu.VMEM((heads, 1, tq), F32),
                        pltpu.VMEM((heads, 1, tq), F32),
                        pltpu.VMEM((heads, MLA_V, tq), F32)],
        compiler_params=pltpu.CompilerParams(
            dimension_semantics=("parallel", "parallel", "arbitrary"),
            vmem_limit_bytes=VMEM_LIMIT),
        name="attn",
    )(mode, qt, k, vt, out_norm_w)


def _pad_lanes(w, width=LANES):
    return jnp.pad(w, [(0, 0)] * (w.ndim - 1) + [(0, width - w.shape[-1])])


def _swap_halves(w):
    half = w.shape[-1] // 2
    return jnp.concatenate([w[..., half:], w[..., :half]], axis=-1)


def _layer(x, mods, pos3, w_ffn1_in, w_ffn1_out, w_in, conv_w, a_log, dt_bias, gdn_norm_w,
           q_norm_w, w_uq, kv_norm_w, w_ukv, qn_q_nope, qn_q_rope, qn_k_nope, qn_k_rope,
           out_norm_w, w_out, w_ffn2_in, w_ffn2_out):
    sh1, s1, g1, sh2, s2, g2, sh3, s3, g3 = mods
    nh = MLA_HEADS

    o_gz = 2 * GDN_HEADS * GDN_DK + 2 * GDN_HEADS * GDN_DV
    o_ab = o_gz + 2 * GDN_HEADS
    o_cq = o_ab + MLA_Q_LORA
    o_ckv = o_cq + MLA_KV_LORA
    w_gdn = jnp.concatenate([w_in[:, :o_gz], _pad_lanes(w_in[:, o_gz:o_ab])], axis=1).astype(BF16)
    w_kr = w_in[:, o_ckv:]
    w_m = jnp.concatenate([w_in[:, o_ab:o_ckv], _pad_lanes(w_kr), _pad_lanes(_swap_halves(w_kr))],
                          axis=1).astype(BF16)
    per_q = MLA_NOPE + MLA_ROPE
    uq_parts = []
    for hh in range(nh):
        wn = w_uq[:, hh * per_q:hh * per_q + MLA_NOPE]
        wr = w_uq[:, hh * per_q + MLA_NOPE:(hh + 1) * per_q]
        uq_parts += [wn, _pad_lanes(wr), _pad_lanes(_swap_halves(wr))]
    w_uq_p = jnp.concatenate(uq_parts, axis=1).astype(BF16)
    w_ukv_b = w_ukv.astype(BF16)

    half = MLA_ROPE // 2
    inv_freq = ROPE_BASE ** (-jnp.arange(half, dtype=F32) / half)
    invf = inv_freq.reshape(half, 1)
    eye = jnp.eye(half, dtype=F32)
    cos_sel = _pad_lanes(jnp.concatenate([eye, eye], axis=1)).astype(BF16)
    sin_sel = _pad_lanes(jnp.concatenate([-eye, eye], axis=1)).astype(BF16)
    row = lambda v: v.reshape(1, -1)

    def sq_norm_bound(w_nope, w_rope):
        return MLA_NOPE * jnp.max(w_nope * w_nope) + MLA_ROPE * jnp.max(w_rope * w_rope)
    q_scale = (MLA_NOPE + MLA_ROPE) ** -0.5 * LOG2_E
    bound = (SCORE_BOUND_MARGIN * q_scale
             * jnp.sqrt(sq_norm_bound(qn_q_nope, qn_q_rope) * sq_norm_bound(qn_k_nope, qn_k_rope)))
    use_bound = bound < MAX_FIXED_REFERENCE
    mode = use_bound.astype(jnp.int32).reshape(1)
    ext_lane = (jnp.arange(LANES) == MLA_ROPE).astype(F32).reshape(1, LANES)
    q_ext = ext_lane * jnp.where(use_bound, -bound, 0.0)
    k_ext = ext_lane
    vecs = [row(q_norm_w), row(kv_norm_w),
            row(qn_q_nope), row(_pad_lanes(qn_q_rope)), row(_pad_lanes(_swap_halves(qn_q_rope))),
            row(qn_k_nope), row(_pad_lanes(qn_k_rope)), row(_pad_lanes(_swap_halves(qn_k_rope))),
            invf, cos_sel, sin_sel, q_ext, k_ext]

    x1, h2 = _ffn(x, sh1, s1, g1, w_ffn1_in.astype(BF16), w_ffn1_out.astype(BF16), nxt=(sh2, s2))

    o_a = _gdn(h2, w_gdn, conv_w, row(_pad_lanes(a_log)), row(_pad_lanes(dt_bias)), row(gdn_norm_w))
    qt, k, vt = _mla_prep(h2, pos3, w_m, w_uq_p, w_ukv_b, vecs)
    o_b = _attn(mode, qt, k, vt, row(out_norm_w))

    return _ffn(x1, sh3, s3, g3, w_ffn2_in.astype(BF16), w_ffn2_out.astype(BF16),
                mix=(o_a, o_b, w_out.astype(BF16), g2))


def kernel(x, c, positions, w_ada, b_ada, ffn1_w_in, ffn1_w_out, w_in, gdn_conv_w, gdn_a_log, gdn_dt_bias, gdn_norm_w, mla_q_norm_w, mla_w_uq, mla_kv_norm_w, mla_w_ukv, qkn_q_nope, qkn_q_rope, qkn_k_nope, qkn_k_rope, mla_out_norm_w, w_out, ffn2_w_in, ffn2_w_out):
    bsz, s, d = x.shape
    pos3 = positions.reshape(bsz, 1, s)
    for l in range(w_ada.shape[0]):
        mod = _mod(c, w_ada[l], b_ada[l])
        mods = [m.reshape(bsz, 1, d) for m in jnp.split(mod, N_MOD, axis=-1)]
        x = _layer(x, mods, pos3, ffn1_w_in[l], ffn1_w_out[l], w_in[l], gdn_conv_w[l],
                   gdn_a_log[l], gdn_dt_bias[l], gdn_norm_w[l], mla_q_norm_w[l], mla_w_uq[l],
                   mla_kv_norm_w[l], mla_w_ukv[l], qkn_q_nope[l], qkn_q_rope[l], qkn_k_nope[l],
                   qkn_k_rope[l], mla_out_norm_w[l], w_out[l], ffn2_w_in[l], ffn2_w_out[l])
    return x
```

```python
import functools

import jax
import jax.numpy as jnp
from jax import lax
from jax.experimental import pallas as pl
from jax.experimental.pallas import tpu as pltpu

F32 = jnp.float32
BF16 = jnp.bfloat16

EPS = 1e-6
ATTN_CHUNK = 64
GDN_HEADS = 4
GDN_DK = 128
GDN_DV = 128
CONV_K = 4
MLA_HEADS = 4
MLA_NOPE = 128
MLA_ROPE = 64
MLA_V = 128
MLA_Q_LORA = 384
MLA_KV_LORA = 256
ROPE_BASE = 10000.0
N_MOD = 9
LOG2_E = 1.4426950408889634
MAX_FIXED_REFERENCE = 60.0
SCORE_BOUND_MARGIN = 1.0625

LANES = 128
SUBLANES = 8
VMEM_LIMIT = 56 * 1024 * 1024

GDN_CHUNK = 128
GDN_PROJ_BLOCK = 256
GDN_CONV_PIECE = 512
FFN_SUBTILES = 4
MLA_SUBTILES = 4


def _dot(a, b):
    return jnp.dot(a, b, preferred_element_type=F32)


def _dot_nt(a, b):
    return lax.dot_general(a, b, (((1,), (1,)), ((), ())), preferred_element_type=F32)


def _dot_tn(a, b):
    return lax.dot_general(a, b, (((0,), (0,)), ((), ())), preferred_element_type=F32)


def _silu(x):
    return x * jax.nn.sigmoid(x)


def _rms_scale(x, n):
    return lax.rsqrt(jnp.sum(x * x, axis=-1, keepdims=True) * (1.0 / n) + EPS)


def _mod_kernel(c_ref, w_ref, b_ref, o_ref):
    sc = _silu(c_ref[...])
    o_ref[...] = jnp.dot(sc, w_ref[...], precision=lax.Precision.HIGHEST,
                         preferred_element_type=F32) + b_ref[...]


def _mod(c, w_ada, b_ada):
    bsz, d = c.shape
    n = w_ada.shape[1]
    tn = 1024
    return pl.pallas_call(
        _mod_kernel,
        grid=(n // tn,),
        in_specs=[pl.BlockSpec((bsz, d), lambda j: (0, 0)),
                  pl.BlockSpec((d, tn), lambda j: (0, j)),
                  pl.BlockSpec((1, tn), lambda j: (0, j))],
        out_specs=pl.BlockSpec((bsz, tn), lambda j: (0, j)),
        out_shape=jax.ShapeDtypeStruct((bsz, n), F32),
        compiler_params=pltpu.CompilerParams(dimension_semantics=("arbitrary",),
                                             vmem_limit_bytes=VMEM_LIMIT),
        name="mod",
    )(c, w_ada, b_ada.reshape(1, n))


def _ffn_kernel(*refs, has_mix, emit_next, d_model, d_ff, tf, n_sub):
    it = iter(refs)
    x_ref = next(it)
    if has_mix:
        oa_ref, ob_ref, wmix_ref, gmix_ref = next(it), next(it), next(it), next(it)
    shift_ref, scale_ref, gate_ref = next(it), next(it), next(it)
    win_ref, wout_ref = next(it), next(it)
    if emit_next:
        nshift_ref, nscale_ref = next(it), next(it)
    out_ref = next(it)
    if emit_next:
        hn_ref = next(it)
    h_sc, a_sc = next(it), next(it)

    tm = x_ref.shape[1]
    ts = tm // n_sub
    subs = [slice(r * ts, (r + 1) * ts) for r in range(n_sub)]

    def prologue(rows):
        x = x_ref[0, rows]
        if has_mix:
            half = oa_ref.shape[-1]
            y = (_dot(oa_ref[0, rows], wmix_ref[:half, :])
                 + _dot(ob_ref[0, rows], wmix_ref[half:, :]))
            x = x + gmix_ref[0] * y
            out_ref[0, rows] = x
        h_sc[rows] = (x * _rms_scale(x, d_model) * (1.0 + scale_ref[0])
                      + shift_ref[0]).astype(BF16)

    def hidden(rows):
        h = h_sc[rows]
        for c in range(d_ff // tf):
            g = _dot(h, win_ref[:, c * tf:(c + 1) * tf])
            u = _dot(h, win_ref[:, d_ff + c * tf:d_ff + (c + 1) * tf])
            a_sc[rows, c * tf:(c + 1) * tf] = (_silu(g) * u).astype(BF16)

    def epilogue(rows):
        x = out_ref[0, rows] if has_mix else x_ref[0, rows]
        out = x + 0.5 * gate_ref[0] * _dot(a_sc[rows], wout_ref[...])
        out_ref[0, rows] = out
        if emit_next:
            hn = out * _rms_scale(out, d_model) * (1.0 + nscale_ref[0]) + nshift_ref[0]
            hn_ref[0, rows] = hn.astype(BF16)

    prologue(subs[0])
    for r in range(n_sub):
        if r + 1 < n_sub:
            prologue(subs[r + 1])
        hidden(subs[r])
        epilogue(subs[r])


def _resident(shape, index_map):
    return pl.BlockSpec(shape, index_map, pipeline_mode=pl.Buffered(1))


def _ffn(x, shift, scale, gate, w_in_bf, w_out_bf, *, mix=None, nxt=None, tm=1024, tf=256):
    bsz, s, d = x.shape
    ff = w_out_bf.shape[0]
    tm = min(tm, s)
    assert ff % tf == 0 and s % tm == 0 and tm % (FFN_SUBTILES * 2 * SUBLANES) == 0
    has_mix = mix is not None
    emit_next = nxt is not None

    row = lambda b, i: (b, i, 0)
    vec = lambda b, i: (b, 0, 0)
    const = lambda b, i: (0, 0)
    args = [x]
    in_specs = [pl.BlockSpec((1, tm, d), row)]
    if has_mix:
        o_a, o_b, w_mix, g_mix = mix
        args += [o_a, o_b, w_mix, g_mix]
        in_specs += [pl.BlockSpec((1, tm, o_a.shape[-1]), row),
                     pl.BlockSpec((1, tm, o_b.shape[-1]), row),
                     _resident(w_mix.shape, const),
                     pl.BlockSpec((1, 1, d), vec)]
    args += [shift, scale, gate, w_in_bf, w_out_bf]
    in_specs += [pl.BlockSpec((1, 1, d), vec)] * 3
    in_specs += [_resident(w_in_bf.shape, const), _resident(w_out_bf.shape, const)]
    if emit_next:
        args += list(nxt)
        in_specs += [pl.BlockSpec((1, 1, d), vec)] * 2
    out_shape = [jax.ShapeDtypeStruct((bsz, s, d), F32)]
    out_specs = [pl.BlockSpec((1, tm, d), row)]
    if emit_next:
        out_shape.append(jax.ShapeDtypeStruct((bsz, s, d), BF16))
        out_specs.append(pl.BlockSpec((1, tm, d), row))

    res = pl.pallas_call(
        functools.partial(_ffn_kernel, has_mix=has_mix, emit_next=emit_next, d_model=d,
                          d_ff=ff, tf=tf, n_sub=FFN_SUBTILES),
        grid=(bsz, s // tm),
        in_specs=in_specs,
        out_specs=out_specs,
        out_shape=out_shape,
        scratch_shapes=[pltpu.VMEM((tm, d), BF16), pltpu.VMEM((tm, ff), BF16)],
        compiler_params=pltpu.CompilerParams(
            dimension_semantics=("parallel", "parallel"),
            vmem_limit_bytes=VMEM_LIMIT),
        name="ffn_mix" if has_mix else "ffn",
    )(*args)
    return res if emit_next else res[0]


def _unit_lower_inverses(a_mats, ii, jj):
    n = a_mats[0].shape[0]
    eye = (ii == jj).astype(F32)

    def same_block(size):
        return (ii // size) == (jj // size)

    base = same_block(SUBLANES)
    d1 = [jnp.where(base, a, 0.0) for a in a_mats]
    d1b = [d.astype(BF16) for d in d1]
    d2 = [_dot(d, d) for d in d1b]
    d2b = [d.astype(BF16) for d in d2]
    d3 = [_dot(a, b) for a, b in zip(d1b, d2b)]
    d4b = [_dot(d, d).astype(BF16) for d in d2b]
    p1 = [eye - a + b - c for a, b, c in zip(d1, d2, d3)]
    ts = [p + _dot(p.astype(BF16), d) for p, d in zip(p1, d4b)]
    size = SUBLANES
    while size < n:
        below = same_block(2 * size) & jnp.logical_not(same_block(size))
        offs = [jnp.where(below, a, 0.0).astype(BF16) for a in a_mats]
        tbs = [t.astype(BF16) for t in ts]
        xs = [_dot(o, t).astype(BF16) for o, t in zip(offs, tbs)]
        ts = [t - _dot(tb, x) for t, tb, x in zip(ts, tbs, xs)]
        size *= 2
    return ts


def _split3(x):
    hi = x.astype(BF16)
    r1 = x - hi.astype(F32)
    mid = r1.astype(BF16)
    lo = (r1 - mid.astype(F32)).astype(BF16)
    return hi, mid, lo


def _gdn_kernel(h0_ref, hn_ref, wproj_ref, convw_ref, alog_ref, dtb_ref, normw_ref,
                o_ref, proj_scr, xbuf, state, *, chunk):
    nb, _, d = hn_ref.shape
    nh, dk, dv = GDN_HEADS, GDN_DK, GDN_DV
    qk_w = nh * dk
    conv_w = 2 * qk_w + nh * dv
    z_off = conv_w
    ab_off = conv_w + nh * dv
    pad = SUBLANES
    probs = [(b, hh) for b in range(nb) for hh in range(nh)]

    @pl.when(pl.program_id(0) == 0)
    def _():
        state[...] = jnp.zeros_like(state)
        xbuf[:, 0:pad, :] = jnp.zeros((nb, pad, conv_w), F32)
        proj_scr[...] = _dot(h0_ref[...].reshape(nb * chunk, d), wproj_ref[...])

    for b in range(nb):
        xbuf[b, pad:pad + chunk, :] = proj_scr[b * chunk:(b + 1) * chunk, :conv_w]
    ab = proj_scr[:, ab_off:ab_off + LANES]
    z_gates = [_silu(proj_scr[b * chunk:(b + 1) * chunk, z_off + hh * dv:z_off + (hh + 1) * dv])
               for b, hh in probs]
    h_next = hn_ref[...].reshape(nb * chunk, d)
    n_cols = proj_scr.shape[1]
    col_blocks = [(c0, min(c0 + GDN_PROJ_BLOCK, n_cols))
                  for c0 in range(0, n_cols, GDN_PROJ_BLOCK)]

    g_all = -jnp.exp(alog_ref[...]) * jax.nn.softplus(ab + dtb_ref[...])
    beta_all = jax.nn.sigmoid(ab)

    ii = lax.broadcasted_iota(jnp.int32, (chunk, chunk), 0)
    jj = lax.broadcasted_iota(jnp.int32, (chunk, chunk), 1)
    incl = ii >= jj
    strict = ii > jj
    tri = incl.astype(BF16)
    cw = convw_ref[...]

    n_pieces = nb * (conv_w // GDN_CONV_PIECE)
    qkv, gcum, gcum_t = [], [], []
    piece = 0
    for b in range(nb):
        rows = slice(b * chunk, (b + 1) * chunk)
        parts = []
        for p0 in range(0, conv_w, GDN_CONV_PIECE):
            for c0, c1 in col_blocks[piece::n_pieces]:
                proj_scr[:, c0:c1] = _dot(h_next, wproj_ref[:, c0:c1])
            piece += 1
            cols = slice(p0, p0 + GDN_CONV_PIECE)
            xe = xbuf[b, :, cols]
            conv = xe[pad:] * cw[CONV_K - 1:CONV_K, cols]
            for tap in range(1, CONV_K):
                conv = conv + (pltpu.roll(xe, tap, axis=0)[pad:]
                               * cw[CONV_K - 1 - tap:CONV_K - tap, cols])
            parts.append(_silu(conv))
        xbuf[b, 0:pad, :] = xbuf[b, chunk:chunk + pad, :]
        qkv.append(jnp.concatenate(parts, axis=1))
        g_hi, g_mid, g_lo = _split3(g_all[rows, :])
        gc = _dot(tri, g_hi) + _dot(tri, g_mid) + _dot(tri, g_lo)
        gcum.append(gc)
        gcum_t.append(gc.T)

    qs, ks, vs, kbs, g_cols, b_cols, decays = [], [], [], [], [], [], []
    for b, hh in probs:
        q = qkv[b][:, hh * dk:(hh + 1) * dk]
        k = qkv[b][:, qk_w + hh * dk:qk_w + (hh + 1) * dk]
        vs.append(qkv[b][:, 2 * qk_w + hh * dv:2 * qk_w + (hh + 1) * dv])
        q = q * (lax.rsqrt(jnp.sum(q * q, axis=-1, keepdims=True) + EPS) * (dk ** -0.5))
        k = k * lax.rsqrt(jnp.sum(k * k, axis=-1, keepdims=True) + EPS)
        qs.append(q)
        ks.append(k)
        kbs.append(k.astype(BF16))
        g_col = gcum[b][:, hh:hh + 1]
        g_row = gcum_t[b][hh:hh + 1, :]
        g_cols.append(g_col)
        b_cols.append(beta_all[b * chunk:(b + 1) * chunk, nh + hh:nh + hh + 1])
        decays.append(jnp.exp(jnp.where(incl, g_col - g_row, -jnp.inf)))

    kks = [_dot_nt(kb, kb) for kb in kbs]
    qks = [(_dot_nt(q.astype(BF16), kb) * dec).astype(BF16) for q, kb, dec in zip(qs, kbs, decays)]
    a_mats = [jnp.where(strict, bc * kk * dec, 0.0) for bc, kk, dec in zip(b_cols, kks, decays)]
    t_invs = _unit_lower_inverses(a_mats, ii, jj)

    e_gs = [jnp.exp(g) for g in g_cols]
    rhs_us = [v * bc for v, bc in zip(vs, b_cols)]
    rhs_ws = [(k * (bc * eg)).astype(BF16) for k, bc, eg in zip(ks, b_cols, e_gs)]
    g_lasts = [g[chunk - 1:chunk, :] for g in g_cols]
    q_decs = [(q * eg).astype(BF16) for q, eg in zip(qs, e_gs)]
    k_dec_ts = [(k * jnp.exp(gl - g)).T.astype(BF16) for k, gl, g in zip(ks, g_lasts, g_cols)]

    s_olds = [state[b, hh] for b, hh in probs]
    s_bs = [s.astype(BF16) for s in s_olds]
    resid = [(ru - _dot(rw, sb)).astype(BF16) for ru, rw, sb in zip(rhs_us, rhs_ws, s_bs)]
    v_nbs = [_dot(t.astype(BF16), r).astype(BF16) for t, r in zip(t_invs, resid)]
    outs = [_dot(jnp.concatenate([qd, qk], axis=1), jnp.concatenate([sb, vn], axis=0))
            for qd, sb, qk, vn in zip(q_decs, s_bs, qks, v_nbs)]
    for (b, hh), s_old, gl, kdt, vn in zip(probs, s_olds, g_lasts, k_dec_ts, v_nbs):
        state[b, hh] = s_old * jnp.exp(gl) + _dot(kdt, vn)
    for (b, hh), o, zg in zip(probs, outs, z_gates):
        o = o * _rms_scale(o, dv) * normw_ref[...] * zg
        o_ref[b, :, hh * dv:(hh + 1) * dv] = o.astype(o_ref.dtype)


def _gdn(h2, w_proj, conv_w, a_log, dt_bias, norm_w):
    bsz, s, d = h2.shape
    chunk = min(GDN_CHUNK, s)
    n_chunks = s // chunk
    nh, dv = GDN_HEADS, GDN_DV
    cw = conv_w.shape[1]
    full = lambda j: (0, 0)
    return pl.pallas_call(
        functools.partial(_gdn_kernel, chunk=chunk),
        grid=(n_chunks,),
        in_specs=[pl.BlockSpec((bsz, chunk, d), lambda j: (0, 0, 0)),
                  pl.BlockSpec((bsz, chunk, d), lambda j: (0, jnp.minimum(j + 1, n_chunks - 1), 0)),
                  pl.BlockSpec(w_proj.shape, full),
                  pl.BlockSpec(conv_w.shape, full),
                  pl.BlockSpec((1, LANES), full),
                  pl.BlockSpec((1, LANES), full),
                  pl.BlockSpec((1, dv), full)],
        out_specs=pl.BlockSpec((bsz, chunk, nh * dv), lambda j: (0, j, 0)),
        out_shape=jax.ShapeDtypeStruct((bsz, s, nh * dv), BF16),
        scratch_shapes=[pltpu.VMEM((bsz * chunk, w_proj.shape[1]), F32),
                        pltpu.VMEM((bsz, chunk + SUBLANES, cw), F32),
                        pltpu.VMEM((bsz, nh, GDN_DK, dv), F32)],
        compiler_params=pltpu.CompilerParams(
            dimension_semantics=("arbitrary",),
            vmem_limit_bytes=VMEM_LIMIT),
        name="gdn",
    )(h2, h2, w_proj, conv_w, a_log, dt_bias, norm_w)


def _mla_prep_kernel(h_ref, pos_ref, wm_ref, wuq_ref, wukv_ref, qnw_ref, kvnw_ref,
                     wqn_ref, wqr_ref, wqrs_ref, wkn_ref, wkr_ref, wkrs_ref,
                     invf_ref, csel_ref, ssel_ref, qext_ref, kext_ref, qt_ref, k_ref, vt_ref,
                     *, n_sub):
    nh = MLA_HEADS
    scale = (MLA_NOPE + MLA_ROPE) ** -0.5 * LOG2_E
    o1 = MLA_Q_LORA
    o2 = o1 + MLA_KV_LORA
    qw = MLA_NOPE + 2 * LANES
    kw = MLA_NOPE + MLA_V
    tm = h_ref.shape[1]
    ts = tm // n_sub
    subs = [slice(r * ts, (r + 1) * ts) for r in range(n_sub)]

    ang_t = invf_ref[...] * pos_ref[0].astype(F32)
    cos_t = _split3(jnp.cos(ang_t))
    sin_t = _split3(jnp.sin(ang_t))

    def spread(parts_t, sel_ref, rows):
        parts = [_dot_tn(part[:, rows], sel_ref[...]) for part in parts_t]
        return (parts[0] + parts[1]) + parts[2]

    def latent(rows):
        return _dot(h_ref[0, rows], wm_ref[...])

    def up_project(lat):
        cq = lat[:, :o1]
        ckv = lat[:, o1:o2]
        cqn = (cq * _rms_scale(cq, MLA_Q_LORA) * qnw_ref[...]).astype(BF16)
        ckvn = (ckv * _rms_scale(ckv, MLA_KV_LORA) * kvnw_ref[...]).astype(BF16)
        return _dot(cqn, wuq_ref[...]), _dot(ckvn, wukv_ref[...])

    def finish(rows, lat, qf, kvf):
        kr = lat[:, o2:o2 + LANES]
        krs = lat[:, o2 + LANES:o2 + 2 * LANES]
        cos_p = spread(cos_t, csel_ref, rows)
        sin_p = spread(sin_t, ssel_ref, rows)
        k_rope = ((kr * wkr_ref[...] * cos_p + krs * wkrs_ref[...] * sin_p)
                  * _rms_scale(kr, MLA_ROPE) + kext_ref[...])
        for hh in range(nh):
            qn = qf[:, hh * qw:hh * qw + MLA_NOPE]
            qr = qf[:, hh * qw + MLA_NOPE:hh * qw + MLA_NOPE + LANES]
            qrs = qf[:, hh * qw + MLA_NOPE + LANES:(hh + 1) * qw]
            qn = qn * _rms_scale(qn, MLA_NOPE) * (wqn_ref[...] * scale)
            q_rope = ((qr * wqr_ref[...] * cos_p + qrs * wqrs_ref[...] * sin_p)
                      * (_rms_scale(qr, MLA_ROPE) * scale)) + qext_ref[...]
            qt_ref[0, hh, :, rows] = jnp.concatenate([qn, q_rope], axis=1).T.astype(BF16)
            kn = kvf[:, hh * kw:hh * kw + MLA_NOPE]
            v = kvf[:, hh * kw + MLA_NOPE:(hh + 1) * kw]
            kn = kn * _rms_scale(kn, MLA_NOPE) * wkn_ref[...]
            k_ref[0, hh, rows, :] = jnp.concatenate([kn, k_rope], axis=1).astype(BF16)
            vt_ref[0, hh, :, rows] = v.T.astype(BF16)

    lats = [latent(subs[0])]
    ups = []
    for r in range(n_sub):
        if r + 1 < n_sub:
            lats.append(latent(subs[r + 1]))
        ups.append(up_project(lats[r]))
        if r >= 1:
            finish(subs[r - 1], lats[r - 1], *ups[r - 1])
    finish(subs[n_sub - 1], lats[n_sub - 1], *ups[n_sub - 1])


def _mla_prep(h2, pos3, w_m, w_uq, w_ukv, vecs, *, tm=1024):
    bsz, s, d = h2.shape
    tm = min(tm, s)
    nh = MLA_HEADS
    qk_dim = MLA_NOPE + LANES
    full = lambda b, i: (0, 0)
    vec_specs = [pl.BlockSpec(v.shape, full) for v in vecs]
    assert tm % (MLA_SUBTILES * LANES) == 0
    return pl.pallas_call(
        functools.partial(_mla_prep_kernel, n_sub=MLA_SUBTILES),
        grid=(bsz, s // tm),
        in_specs=[pl.BlockSpec((1, tm, d), lambda b, i: (b, i, 0)),
                  pl.BlockSpec((1, 1, tm), lambda b, i: (b, 0, i)),
                  pl.BlockSpec(w_m.shape, full),
                  pl.BlockSpec(w_uq.shape, full),
                  pl.BlockSpec(w_ukv.shape, full)] + vec_specs,
        out_specs=[pl.BlockSpec((1, nh, qk_dim, tm), lambda b, i: (b, 0, 0, i)),
                   pl.BlockSpec((1, nh, tm, qk_dim), lambda b, i: (b, 0, i, 0)),
                   pl.BlockSpec((1, nh, MLA_V, tm), lambda b, i: (b, 0, 0, i))],
        out_shape=[jax.ShapeDtypeStruct((bsz, nh, qk_dim, s), BF16),
                   jax.ShapeDtypeStruct((bsz, nh, s, qk_dim), BF16),
                   jax.ShapeDtypeStruct((bsz, nh, MLA_V, s), BF16)],
        compiler_params=pltpu.CompilerParams(
            dimension_semantics=("parallel", "parallel"),
            vmem_limit_bytes=VMEM_LIMIT),
        name="mla_prep",
    )(h2, pos3, w_m, w_uq, w_ukv, *vecs)


def _attn_kernel(mode_ref, qt_ref, k_ref, vt_ref, nw_ref, o_ref, s_scr, m_scr, l_scr, acc_scr, *,
                 tq, tk):
    i = pl.program_id(2)
    per_q = tq // tk
    kc = lax.broadcasted_iota(jnp.int32, (tk, tq), 0) // ATTN_CHUNK
    qc = lax.broadcasted_iota(jnp.int32, (tk, tq), 1) // ATTN_CHUNK
    visible = qc >= kc

    def key_tile(hh, j):
        start = pl.multiple_of(j * tk, tk)
        return k_ref[0, hh, pl.ds(start, tk), :], vt_ref[0, hh, :, pl.ds(start, tk)]

    def finish():
        for hh in range(2):
            o = (acc_scr[hh] / l_scr[hh]).T
            o = o * _rms_scale(o, MLA_V) * nw_ref[...]
            o_ref[0, :, hh * MLA_V:(hh + 1) * MLA_V] = o.astype(o_ref.dtype)

    l_scr[...] = jnp.zeros(l_scr.shape, F32)
    acc_scr[...] = jnp.zeros(acc_scr.shape, F32)

    @pl.when(mode_ref[0] == 1)
    def _fixed_reference():
        def tile(hh, j, q0=0, mask=None):
            kj, vj = key_tile(hh, j)
            st = _dot(kj, qt_ref[0, hh, :, q0:])
            if mask is not None:
                st = jnp.where(mask, st, -jnp.inf)
            p = jnp.exp2(st)
            l_scr[hh, :, q0:] += jnp.sum(p, axis=0, keepdims=True)
            acc_scr[hh, :, q0:] += _dot(vj, p.astype(BF16))

        def body(jj, _):
            for r in range(per_q):
                tile(0, jj * per_q + r)
                tile(1, jj * per_q + r)
            return 0

        lax.fori_loop(0, i, body, 0)
        for r in range(per_q):
            for hh in range(2):
                tile(hh, i * per_q + r, r * tk, visible[:, :tq - r * tk])
        finish()

    @pl.when(mode_ref[0] == 0)
    def _online():
        def scores(hh, j, q0=0):
            kj, _ = key_tile(hh, j)
            s_scr[hh, :, :tq - q0] = _dot(kj, qt_ref[0, hh, :, q0:])

        def accumulate(hh, j, q0=0, mask=None):
            w = tq - q0
            if mask is not None:
                s_scr[hh, :, :w] = jnp.where(mask, s_scr[hh, :, :w], -jnp.inf)
            m = m_scr[hh, :, q0:]
            m_new = jnp.maximum(m, jnp.max(s_scr[hh, :, :w], axis=0, keepdims=True))
            alpha = jnp.exp2(m - m_new)
            p = jnp.exp2(s_scr[hh, :, :w] - m_new)
            m_scr[hh, :, q0:] = m_new
            l_scr[hh, :, q0:] = alpha * l_scr[hh, :, q0:] + jnp.sum(p, axis=0, keepdims=True)
            _, vj = key_tile(hh, j)
            acc_scr[hh, :, q0:] = alpha * acc_scr[hh, :, q0:] + _dot(vj, p.astype(BF16))

        m_scr[...] = jnp.full(m_scr.shape, -jnp.inf, F32)
        scores(0, 0)

        def body(jj, _):
            for r in range(per_q):
                j = jj * per_q + r
                scores(1, j)
                accumulate(0, j)
                scores(0, j + 1)
                accumulate(1, j)
            return 0

        lax.fori_loop(0, i, body, 0)
        for r in range(per_q):
            q0 = r * tk
            j = i * per_q + r
            scores(1, j, q0)
            accumulate(0, j, q0, visible[:, :tq - q0])
            if r < per_q - 1:
                scores(0, j + 1, q0 + tk)
            accumulate(1, j, q0, visible[:, :tq - q0])
        finish()


def _attn(mode, qt, k, vt, out_norm_w, *, tq=2048, tk=512):
    bsz, nh, qk_dim, s = qt.shape
    tq = min(tq, s)
    tk = min(tk, tq)
    heads = 2
    return pl.pallas_call(
        functools.partial(_attn_kernel, tq=tq, tk=tk),
        grid=(bsz, nh // heads, s // tq),
        in_specs=[pl.BlockSpec(memory_space=pltpu.SMEM),
                  pl.BlockSpec((1, heads, qk_dim, tq), lambda b, h, i: (b, h, 0, i)),
                  pl.BlockSpec((1, heads, s, qk_dim), lambda b, h, i: (b, h, 0, 0)),
                  pl.BlockSpec((1, heads, MLA_V, s), lambda b, h, i: (b, h, 0, 0)),
                  pl.BlockSpec((1, MLA_V), lambda b, h, i: (0, 0))],
        out_specs=pl.BlockSpec((1, tq, heads * MLA_V), lambda b, h, i: (b, i, h)),
        out_shape=jax.ShapeDtypeStruct((bsz, s, nh * MLA_V), BF16),
        scratch_shapes=[pltpu.VMEM((heads, tk, tq), F32),
                        pltpu.VMEM((heads, 1, tq), F32),
                        pltpu.VMEM((heads, 1, tq), F32),
                        pltpu.VMEM((heads, MLA_V, tq), F32)],
        compiler_params=pltpu.CompilerParams(
            dimension_semantics=("parallel", "parallel", "arbitrary"),
            vmem_limit_bytes=VMEM_LIMIT),
        name="attn",
    )(mode, qt, k, vt, out_norm_w)


def _pad_lanes(w, width=LANES):
    return jnp.pad(w, [(0, 0)] * (w.ndim - 1) + [(0, width - w.shape[-1])])


def _swap_halves(w):
    half = w.shape[-1] // 2
    return jnp.concatenate([w[..., half:], w[..., :half]], axis=-1)


def _layer(x, mods, pos3, w_ffn1_in, w_ffn1_out, w_in, conv_w, a_log, dt_bias, gdn_norm_w,
           q_norm_w, w_uq, kv_norm_w, w_ukv, qn_q_nope, qn_q_rope, qn_k_nope, qn_k_rope,
           out_norm_w, w_out, w_ffn2_in, w_ffn2_out):
    sh1, s1, g1, sh2, s2, g2, sh3, s3, g3 = mods
    nh = MLA_HEADS

    o_gz = 2 * GDN_HEADS * GDN_DK + 2 * GDN_HEADS * GDN_DV
    o_ab = o_gz + 2 * GDN_HEADS
    o_cq = o_ab + MLA_Q_LORA
    o_ckv = o_cq + MLA_KV_LORA
    w_gdn = jnp.concatenate([w_in[:, :o_gz], _pad_lanes(w_in[:, o_gz:o_ab])], axis=1).astype(BF16)
    w_kr = w_in[:, o_ckv:]
    w_m = jnp.concatenate([w_in[:, o_ab:o_ckv], _pad_lanes(w_kr), _pad_lanes(_swap_halves(w_kr))],
                          axis=1).astype(BF16)
    per_q = MLA_NOPE + MLA_ROPE
    uq_parts = []
    for hh in range(nh):
        wn = w_uq[:, hh * per_q:hh * per_q + MLA_NOPE]
        wr = w_uq[:, hh * per_q + MLA_NOPE:(hh + 1) * per_q]
        uq_parts += [wn, _pad_lanes(wr), _pad_lanes(_swap_halves(wr))]
    w_uq_p = jnp.concatenate(uq_parts, axis=1).astype(BF16)
    w_ukv_b = w_ukv.astype(BF16)

    half = MLA_ROPE // 2
    inv_freq = ROPE_BASE ** (-jnp.arange(half, dtype=F32) / half)
    invf = inv_freq.reshape(half, 1)
    eye = jnp.eye(half, dtype=F32)
    cos_sel = _pad_lanes(jnp.concatenate([eye, eye], axis=1)).astype(BF16)
    sin_sel = _pad_lanes(jnp.concatenate([-eye, eye], axis=1)).astype(BF16)
    row = lambda v: v.reshape(1, -1)

    def sq_norm_bound(w_nope, w_rope):
        return MLA_NOPE * jnp.max(w_nope * w_nope) + MLA_ROPE * jnp.max(w_rope * w_rope)
    q_scale = (MLA_NOPE + MLA_ROPE) ** -0.5 * LOG2_E
    bound = (SCORE_BOUND_MARGIN * q_scale
             * jnp.sqrt(sq_norm_bound(qn_q_nope, qn_q_rope) * sq_norm_bound(qn_k_nope, qn_k_rope)))
    use_bound = bound < MAX_FIXED_REFERENCE
    mode = use_bound.astype(jnp.int32).reshape(1)
    ext_lane = (jnp.arange(LANES) == MLA_ROPE).astype(F32).reshape(1, LANES)
    q_ext = ext_lane * jnp.where(use_bound, -bound, 0.0)
    k_ext = ext_lane
    vecs = [row(q_norm_w), row(kv_norm_w),
            row(qn_q_nope), row(_pad_lanes(qn_q_rope)), row(_pad_lanes(_swap_halves(qn_q_rope))),
            row(qn_k_nope), row(_pad_lanes(qn_k_rope)), row(_pad_lanes(_swap_halves(qn_k_rope))),
            invf, cos_sel, sin_sel, q_ext, k_ext]

    x1, h2 = _ffn(x, sh1, s1, g1, w_ffn1_in.astype(BF16), w_ffn1_out.astype(BF16), nxt=(sh2, s2))

    o_a = _gdn(h2, w_gdn, conv_w, row(_pad_lanes(a_log)), row(_pad_lanes(dt_bias)), row(gdn_norm_w))
    qt, k, vt = _mla_prep(h2, pos3, w_m, w_uq_p, w_ukv_b, vecs)
    o_b = _attn(mode, qt, k, vt, row(out_norm_w))

    return _ffn(x1, sh3, s3, g3, w_ffn2_in.astype(BF16), w_ffn2_out.astype(BF16),
                mix=(o_a, o_b, w_out.astype(BF16), g2))


def kernel(x, c, positions, w_ada, b_ada, ffn1_w_in, ffn1_w_out, w_in, gdn_conv_w, gdn_a_log, gdn_dt_bias, gdn_norm_w, mla_q_norm_w, mla_w_uq, mla_kv_norm_w, mla_w_ukv, qkn_q_nope, qkn_q_rope, qkn_k_nope, qkn_k_rope, mla_out_norm_w, w_out, ffn2_w_in, ffn2_w_out):
    bsz, s, d = x.shape
    pos3 = positions.reshape(bsz, 1, s)
    for l in range(w_ada.shape[0]):
        mod = _mod(c, w_ada[l], b_ada[l])
        mods = [m.reshape(bsz, 1, d) for m in jnp.split(mod, N_MOD, axis=-1)]
        x = _layer(x, mods, pos3, ffn1_w_in[l], ffn1_w_out[l], w_in[l], gdn_conv_w[l],
                   gdn_a_log[l], gdn_dt_bias[l], gdn_norm_w[l], mla_q_norm_w[l], mla_w_uq[l],
                   mla_kv_norm_w[l], mla_w_ukv[l], qkn_q_nope[l], qkn_q_rope[l], qkn_k_nope[l],
                   qkn_k_rope[l], mla_out_norm_w[l], w_out[l], ffn2_w_in[l], ffn2_w_out[l])
    return x
```

```python
import functools

import jax
import jax.numpy as jnp
from jax import lax
from jax.experimental import pallas as pl
from jax.experimental.pallas import tpu as pltpu

F32 = jnp.float32
BF16 = jnp.bfloat16

EPS = 1e-6
ATTN_CHUNK = 64
GDN_HEADS = 4
GDN_DK = 128
GDN_DV = 128
CONV_K = 4
MLA_HEADS = 4
MLA_NOPE = 128
MLA_ROPE = 64
MLA_V = 128
MLA_Q_LORA = 384
MLA_KV_LORA = 256
ROPE_BASE = 10000.0
N_MOD = 9
LOG2_E = 1.4426950408889634
MAX_FIXED_REFERENCE = 60.0
SCORE_BOUND_MARGIN = 1.0625

LANES = 128
SUBLANES = 8
VMEM_LIMIT = 56 * 1024 * 1024

GDN_CHUNK = 128
GDN_PROJ_BLOCK = 256
GDN_CONV_PIECE = 512
FFN_SUBTILES = 4
MLA_SUBTILES = 4


def _dot(a, b):
    return jnp.dot(a, b, preferred_element_type=F32)


def _dot_nt(a, b):
    return lax.dot_general(a, b, (((1,), (1,)), ((), ())), preferred_element_type=F32)


def _dot_tn(a, b):
    return lax.dot_general(a, b, (((0,), (0,)), ((), ())), preferred_element_type=F32)


def _silu(x):
    return x * jax.nn.sigmoid(x)


def _rms_scale(x, n):
    return lax.rsqrt(jnp.sum(x * x, axis=-1, keepdims=True) * (1.0 / n) + EPS)


def _mod_kernel(c_ref, w_ref, b_ref, o_ref):
    sc = _silu(c_ref[...])
    o_ref[...] = jnp.dot(sc, w_ref[...], precision=lax.Precision.HIGHEST,
                         preferred_element_type=F32) + b_ref[...]


def _mod(c, w_ada, b_ada):
    bsz, d = c.shape
    n = w_ada.shape[1]
    tn = 1024
    return pl.pallas_call(
        _mod_kernel,
        grid=(n // tn,),
        in_specs=[pl.BlockSpec((bsz, d), lambda j: (0, 0)),
                  pl.BlockSpec((d, tn), lambda j: (0, j)),
                  pl.BlockSpec((1, tn), lambda j: (0, j))],
        out_specs=pl.BlockSpec((bsz, tn), lambda j: (0, j)),
        out_shape=jax.ShapeDtypeStruct((bsz, n), F32),
        compiler_params=pltpu.CompilerParams(dimension_semantics=("arbitrary",),
                                             vmem_limit_bytes=VMEM_LIMIT),
        name="mod",
    )(c, w_ada, b_ada.reshape(1, n))


def _ffn_kernel(*refs, has_mix, emit_next, d_model, d_ff, tf, n_sub):
    it = iter(refs)
    x_ref = next(it)
    if has_mix:
        oa_ref, ob_ref, wmix_ref, gmix_ref = next(it), next(it), next(it), next(it)
    shift_ref, scale_ref, gate_ref = next(it), next(it), next(it)
    win_ref, wout_ref = next(it), next(it)
    if emit_next:
        nshift_ref, nscale_ref = next(it), next(it)
    out_ref = next(it)
    if emit_next:
        hn_ref = next(it)
    h_sc, a_sc = next(it), next(it)

    tm = x_ref.shape[1]
    ts = tm // n_sub
    subs = [slice(r * ts, (r + 1) * ts) for r in range(n_sub)]

    def prologue(rows):
        x = x_ref[0, rows]
        if has_mix:
            half = oa_ref.shape[-1]
            y = (_dot(oa_ref[0, rows], wmix_ref[:half, :])
                 + _dot(ob_ref[0, rows], wmix_ref[half:, :]))
            x = x + gmix_ref[0] * y
            out_ref[0, rows] = x
        h_sc[rows] = (x * _rms_scale(x, d_model) * (1.0 + scale_ref[0])
                      + shift_ref[0]).astype(BF16)

    def hidden(rows):
        h = h_sc[rows]
        for c in range(d_ff // tf):
            g = _dot(h, win_ref[:, c * tf:(c + 1) * tf])
            u = _dot(h, win_ref[:, d_ff + c * tf:d_ff + (c + 1) * tf])
            a_sc[rows, c * tf:(c + 1) * tf] = (_silu(g) * u).astype(BF16)

    def epilogue(rows):
        x = out_ref[0, rows] if has_mix else x_ref[0, rows]
        out = x + 0.5 * gate_ref[0] * _dot(a_sc[rows], wout_ref[...])
        out_ref[0, rows] = out
        if emit_next:
            hn = out * _rms_scale(out, d_model) * (1.0 + nscale_ref[0]) + nshift_ref[0]
            hn_ref[0, rows] = hn.astype(BF16)

    prologue(subs[0])
    for r in range(n_sub):
        if r + 1 < n_sub:
            prologue(subs[r + 1])
        hidden(subs[r])
        epilogue(subs[r])


def _resident(shape, index_map):
    return pl.BlockSpec(shape, index_map, pipeline_mode=pl.Buffered(1))


def _ffn(x, shift, scale, gate, w_in_bf, w_out_bf, *, mix=None, nxt=None, tm=1024, tf=256):
    bsz, s, d = x.shape
    ff = w_out_bf.shape[0]
    tm = min(tm, s)
    assert ff % tf == 0 and s % tm == 0 and tm % (FFN_SUBTILES * 2 * SUBLANES) == 0
    has_mix = mix is not None
    emit_next = nxt is not None

    row = lambda b, i: (b, i, 0)
    vec = lambda b, i: (b, 0, 0)
    const = lambda b, i: (0, 0)
    args = [x]
    in_specs = [pl.BlockSpec((1, tm, d), row)]
    if has_mix:
        o_a, o_b, w_mix, g_mix = mix
        args += [o_a, o_b, w_mix, g_mix]
        in_specs += [pl.BlockSpec((1, tm, o_a.shape[-1]), row),
                     pl.BlockSpec((1, tm, o_b.shape[-1]), row),
                     _resident(w_mix.shape, const),
                     pl.BlockSpec((1, 1, d), vec)]
    args += [shift, scale, gate, w_in_bf, w_out_bf]
    in_specs += [pl.BlockSpec((1, 1, d), vec)] * 3
    in_specs += [_resident(w_in_bf.shape, const), _resident(w_out_bf.shape, const)]
    if emit_next:
        args += list(nxt)
        in_specs += [pl.BlockSpec((1, 1, d), vec)] * 2
    out_shape = [jax.ShapeDtypeStruct((bsz, s, d), F32)]
    out_specs = [pl.BlockSpec((1, tm, d), row)]
    if emit_next:
        out_shape.append(jax.ShapeDtypeStruct((bsz, s, d), BF16))
        out_specs.append(pl.BlockSpec((1, tm, d), row))

    res = pl.pallas_call(
        functools.partial(_ffn_kernel, has_mix=has_mix, emit_next=emit_next, d_model=d,
                          d_ff=ff, tf=tf, n_sub=FFN_SUBTILES),
        grid=(bsz, s // tm),
        in_specs=in_specs,
        out_specs=out_specs,
        out_shape=out_shape,
        scratch_shapes=[pltpu.VMEM((tm, d), BF16), pltpu.VMEM((tm, ff), BF16)],
        compiler_params=pltpu.CompilerParams(
            dimension_semantics=("parallel", "parallel"),
            vmem_limit_bytes=VMEM_LIMIT),
        name="ffn_mix" if has_mix else "ffn",
    )(*args)
    return res if emit_next else res[0]


def _unit_lower_inverses(a_mats, ii, jj):
    n = a_mats[0].shape[0]
    eye = (ii == jj).astype(F32)

    def same_block(size):
        return (ii // size) == (jj // size)

    base = same_block(SUBLANES)
    d1 = [jnp.where(base, a, 0.0) for a in a_mats]
    d1b = [d.astype(BF16) for d in d1]
    d2 = [_dot(d, d) for d in d1b]
    d2b = [d.astype(BF16) for d in d2]
    d3 = [_dot(a, b) for a, b in zip(d1b, d2b)]
    d4b = [_dot(d, d).astype(BF16) for d in d2b]
    p1 = [eye - a + b - c for a, b, c in zip(d1, d2, d3)]
    ts = [p + _dot(p.astype(BF16), d) for p, d in zip(p1, d4b)]
    size = SUBLANES
    while size < n:
        below = same_block(2 * size) & jnp.logical_not(same_block(size))
        offs = [jnp.where(below, a, 0.0).astype(BF16) for a in a_mats]
        tbs = [t.astype(BF16) for t in ts]
        xs = [_dot(o, t).astype(BF16) for o, t in zip(offs, tbs)]
        ts = [t - _dot(tb, x) for t, tb, x in zip(ts, tbs, xs)]
        size *= 2
    return ts


def _split3(x):
    hi = x.astype(BF16)
    r1 = x - hi.astype(F32)
    mid = r1.astype(BF16)
    lo = (r1 - mid.astype(F32)).astype(BF16)
    return hi, mid, lo


def _gdn_kernel(h0_ref, hn_ref, wproj_ref, convw_ref, alog_ref, dtb_ref, normw_ref,
                o_ref, proj_scr, xbuf, state, *, chunk):
    nb, _, d = hn_ref.shape
    nh, dk, dv = GDN_HEADS, GDN_DK, GDN_DV
    qk_w = nh * dk
    conv_w = 2 * qk_w + nh * dv
    z_off = conv_w
    ab_off = conv_w + nh * dv
    pad = SUBLANES
    probs = [(b, hh) for b in range(nb) for hh in range(nh)]

    @pl.when(pl.program_id(0) == 0)
    def _():
        state[...] = jnp.zeros_like(state)
        xbuf[:, 0:pad, :] = jnp.zeros((nb, pad, conv_w), F32)
        proj_scr[...] = _dot(h0_ref[...].reshape(nb * chunk, d), wproj_ref[...])

    for b in range(nb):
        xbuf[b, pad:pad + chunk, :] = proj_scr[b * chunk:(b + 1) * chunk, :conv_w]
    ab = proj_scr[:, ab_off:ab_off + LANES]
    z_gates = [_silu(proj_scr[b * chunk:(b + 1) * chunk, z_off + hh * dv:z_off + (hh + 1) * dv])
               for b, hh in probs]
    h_next = hn_ref[...].reshape(nb * chunk, d)
    n_cols = proj_scr.shape[1]
    col_blocks = [(c0, min(c0 + GDN_PROJ_BLOCK, n_cols))
                  for c0 in range(0, n_cols, GDN_PROJ_BLOCK)]

    g_all = -jnp.exp(alog_ref[...]) * jax.nn.softplus(ab + dtb_ref[...])
    beta_all = jax.nn.sigmoid(ab)

    ii = lax.broadcasted_iota(jnp.int32, (chunk, chunk), 0)
    jj = lax.broadcasted_iota(jnp.int32, (chunk, chunk), 1)
    incl = ii >= jj
    strict = ii > jj
    tri = incl.astype(BF16)
    cw = convw_ref[...]

    n_pieces = nb * (conv_w // GDN_CONV_PIECE)
    qkv, gcum, gcum_t = [], [], []
    piece = 0
    for b in range(nb):
        rows = slice(b * chunk, (b + 1) * chunk)
        parts = []
        for p0 in range(0, conv_w, GDN_CONV_PIECE):
            for c0, c1 in col_blocks[piece::n_pieces]:
                proj_scr[:, c0:c1] = _dot(h_next, wproj_ref[:, c0:c1])
            piece += 1
            cols = slice(p0, p0 + GDN_CONV_PIECE)
            xe = xbuf[b, :, cols]
            conv = xe[pad:] * cw[CONV_K - 1:CONV_K, cols]
            for tap in range(1, CONV_K):
                conv = conv + (pltpu.roll(xe, tap, axis=0)[pad:]
                               * cw[CONV_K - 1 - tap:CONV_K - tap, cols])
            parts.append(_silu(conv))
        xbuf[b, 0:pad, :] = xbuf[b, chunk:chunk + pad, :]
        qkv.append(jnp.concatenate(parts, axis=1))
        g_hi, g_mid, g_lo = _split3(g_all[rows, :])
        gc = _dot(tri, g_hi) + _dot(tri, g_mid) + _dot(tri, g_lo)
        gcum.append(gc)
        gcum_t.append(gc.T)

    qs, ks, vs, kbs, g_cols, b_cols, decays = [], [], [], [], [], [], []
    for b, hh in probs:
        q = qkv[b][:, hh * dk:(hh + 1) * dk]
        k = qkv[b][:, qk_w + hh * dk:qk_w + (hh + 1) * dk]
        vs.append(qkv[b][:, 2 * qk_w + hh * dv:2 * qk_w + (hh + 1) * dv])
        q = q * (lax.rsqrt(jnp.sum(q * q, axis=-1, keepdims=True) + EPS) * (dk ** -0.5))
        k = k * lax.rsqrt(jnp.sum(k * k, axis=-1, keepdims=True) + EPS)
        qs.append(q)
        ks.append(k)
        kbs.append(k.astype(BF16))
        g_col = gcum[b][:, hh:hh + 1]
        g_row = gcum_t[b][hh:hh + 1, :]
        g_cols.append(g_col)
        b_cols.append(beta_all[b * chunk:(b + 1) * chunk, nh + hh:nh + hh + 1])
        decays.append(jnp.exp(jnp.where(incl, g_col - g_row, -jnp.inf)))

    kks = [_dot_nt(kb, kb) for kb in kbs]
    qks = [(_dot_nt(q.astype(BF16), kb) * dec).astype(BF16) for q, kb, dec in zip(qs, kbs, decays)]
    a_mats = [jnp.where(strict, bc * kk * dec, 0.0) for bc, kk, dec in zip(b_cols, kks, decays)]
    t_invs = _unit_lower_inverses(a_mats, ii, jj)

    e_gs = [jnp.exp(g) for g in g_cols]
    rhs_us = [v * bc for v, bc in zip(vs, b_cols)]
    rhs_ws = [(k * (bc * eg)).astype(BF16) for k, bc, eg in zip(ks, b_cols, e_gs)]
    g_lasts = [g[chunk - 1:chunk, :] for g in g_cols]
    q_decs = [(q * eg).astype(BF16) for q, eg in zip(qs, e_gs)]
    k_dec_ts = [(k * jnp.exp(gl - g)).T.astype(BF16) for k, gl, g in zip(ks, g_lasts, g_cols)]

    s_olds = [state[b, hh] for b, hh in probs]
    s_bs = [s.astype(BF16) for s in s_olds]
    resid = [(ru - _dot(rw, sb)).astype(BF16) for ru, rw, sb in zip(rhs_us, rhs_ws, s_bs)]
    v_nbs = [_dot(t.astype(BF16), r).astype(BF16) for t, r in zip(t_invs, resid)]
    outs = [_dot(jnp.concatenate([qd, qk], axis=1), jnp.concatenate([sb, vn], axis=0))
            for qd, sb, qk, vn in zip(q_decs, s_bs, qks, v_nbs)]
    for (b, hh), s_old, gl, kdt, vn in zip(probs, s_olds, g_lasts, k_dec_ts, v_nbs):
        state[b, hh] = s_old * jnp.exp(gl) + _dot(kdt, vn)
    for (b, hh), o, zg in zip(probs, outs, z_gates):
        o = o * _rms_scale(o, dv) * normw_ref[...] * zg
        o_ref[b, :, hh * dv:(hh + 1) * dv] = o.astype(o_ref.dtype)


def _gdn(h2, w_proj, conv_w, a_log, dt_bias, norm_w):
    bsz, s, d = h2.shape
    chunk = min(GDN_CHUNK, s)
    n_chunks = s // chunk
    nh, dv = GDN_HEADS, GDN_DV
    cw = conv_w.shape[1]
    full = lambda j: (0, 0)
    return pl.pallas_call(
        functools.partial(_gdn_kernel, chunk=chunk),
        grid=(n_chunks,),
        in_specs=[pl.BlockSpec((bsz, chunk, d), lambda j: (0, 0, 0)),
                  pl.BlockSpec((bsz, chunk, d), lambda j: (0, jnp.minimum(j + 1, n_chunks - 1), 0)),
                  pl.BlockSpec(w_proj.shape, full),
                  pl.BlockSpec(conv_w.shape, full),
                  pl.BlockSpec((1, LANES), full),
                  pl.BlockSpec((1, LANES), full),
                  pl.BlockSpec((1, dv), full)],
        out_specs=pl.BlockSpec((bsz, chunk, nh * dv), lambda j: (0, j, 0)),
        out_shape=jax.ShapeDtypeStruct((bsz, s, nh * dv), BF16),
        scratch_shapes=[pltpu.VMEM((bsz * chunk, w_proj.shape[1]), F32),
                        pltpu.VMEM((bsz, chunk + SUBLANES, cw), F32),
                        pltpu.VMEM((bsz, nh, GDN_DK, dv), F32)],
        compiler_params=pltpu.CompilerParams(
            dimension_semantics=("arbitrary",),
            vmem_limit_bytes=VMEM_LIMIT),
        name="gdn",
    )(h2, h2, w_proj, conv_w, a_log, dt_bias, norm_w)


def _mla_prep_kernel(h_ref, pos_ref, wm_ref, wuq_ref, wukv_ref, qnw_ref, kvnw_ref,
                     wqn_ref, wqr_ref, wqrs_ref, wkn_ref, wkr_ref,
                     invf_ref, sel_ref, qexte_ref, qexto_ref, kexte_ref, kexto_ref,
                     qt_ref, k_ref, vt_ref, *, n_sub):
    nh = MLA_HEADS
    scale = (MLA_NOPE + MLA_ROPE) ** -0.5 * LOG2_E
    o1 = MLA_Q_LORA
    o2 = o1 + MLA_KV_LORA
    pw = 2 * MLA_NOPE + 2 * LANES
    kw = MLA_NOPE + MLA_V
    tm = h_ref.shape[1]
    ts = tm // n_sub
    subs = [slice(r * ts, (r + 1) * ts) for r in range(n_sub)]
    lo = lax.broadcasted_iota(jnp.int32, (1, LANES), 1) < MLA_ROPE

    ang_t = invf_ref[...] * pos_ref[0].astype(F32)
    trig_t = _split3(jnp.concatenate([jnp.cos(ang_t), jnp.sin(ang_t)], axis=0))

    def half_sums(x):
        sq = x * x
        inv_lo = lax.rsqrt(jnp.sum(jnp.where(lo, sq, 0.0), axis=-1, keepdims=True)
                           * (1.0 / MLA_ROPE) + EPS)
        inv_hi = lax.rsqrt(jnp.sum(jnp.where(lo, 0.0, sq), axis=-1, keepdims=True)
                           * (1.0 / MLA_ROPE) + EPS)
        return jnp.where(lo, inv_lo, inv_hi)

    def latent(rows):
        return _dot(h_ref[0, rows], wm_ref[...])

    def up_project(lat):
        cq = lat[:, :o1]
        ckv = lat[:, o1:o2]
        cqn = (cq * _rms_scale(cq, MLA_Q_LORA) * qnw_ref[...]).astype(BF16)
        ckvn = (ckv * _rms_scale(ckv, MLA_KV_LORA) * kvnw_ref[...]).astype(BF16)
        return _dot(cqn, wuq_ref[...]), _dot(ckvn, wukv_ref[...])

    def finish(rows, lat, qf, kvf):
        parts = [_dot_tn(part[:, rows], sel_ref[...]) for part in trig_t]
        trig = (parts[0] + parts[1]) + parts[2]
        trig_r = pltpu.roll(trig, MLA_ROPE, axis=1)
        cos4 = jnp.where(lo, trig, trig_r)
        sin4 = jnp.where(lo, trig_r, trig)

        kr2 = lat[:, o2:o2 + LANES]
        a = kr2 * wkr_ref[...] * trig
        k_rope = (a + pltpu.roll(a, MLA_ROPE, axis=1)) * lax.rsqrt(
            jnp.sum(jnp.where(lo, kr2 * kr2, 0.0), axis=-1, keepdims=True) * (1.0 / MLA_ROPE) + EPS)
        k_rope_even = jnp.where(lo, k_rope, 0.0) + kexte_ref[...]
        k_rope_odd = jnp.where(lo, 0.0, k_rope) + kexto_ref[...]
        for pair in range(nh // 2):
            base = pair * pw
            qr2 = qf[:, base + 2 * MLA_NOPE:base + 2 * MLA_NOPE + LANES]
            qrs2 = qf[:, base + 2 * MLA_NOPE + LANES:base + pw]
            q_rope2 = ((qr2 * wqr_ref[...] * cos4 + qrs2 * wqrs_ref[...] * sin4)
                       * (half_sums(qr2) * scale))
            for odd in range(2):
                hh = 2 * pair + odd
                qn = qf[:, base + odd * MLA_NOPE:base + (odd + 1) * MLA_NOPE]
                qn = qn * _rms_scale(qn, MLA_NOPE) * (wqn_ref[...] * scale)
                if odd:
                    q_rope = jnp.where(lo, 0.0, q_rope2) + qexto_ref[...]
                else:
                    q_rope = jnp.where(lo, q_rope2, 0.0) + qexte_ref[...]
                qt_ref[0, hh, :, rows] = jnp.concatenate([qn, q_rope], axis=1).T.astype(BF16)
                kn = kvf[:, hh * kw:hh * kw + MLA_NOPE]
                v = kvf[:, hh * kw + MLA_NOPE:(hh + 1) * kw]
                kn = kn * _rms_scale(kn, MLA_NOPE) * wkn_ref[...]
                k_rope_h = k_rope_odd if odd else k_rope_even
                k_ref[0, hh, rows, :] = jnp.concatenate([kn, k_rope_h], axis=1).astype(BF16)
                vt_ref[0, hh, :, rows] = v.T.astype(BF16)

    lats = [latent(subs[0])]
    ups = []
    for r in range(n_sub):
        if r + 1 < n_sub:
            lats.append(latent(subs[r + 1]))
        ups.append(up_project(lats[r]))
        if r >= 1:
            finish(subs[r - 1], lats[r - 1], *ups[r - 1])
    finish(subs[n_sub - 1], lats[n_sub - 1], *ups[n_sub - 1])


def _mla_prep(h2, pos3, w_m, w_uq, w_ukv, vecs, *, tm=1024):
    bsz, s, d = h2.shape
    tm = min(tm, s)
    nh = MLA_HEADS
    qk_dim = MLA_NOPE + LANES
    full = lambda b, i: (0, 0)
    vec_specs = [pl.BlockSpec(v.shape, full) for v in vecs]
    assert tm % (MLA_SUBTILES * LANES) == 0
    return pl.pallas_call(
        functools.partial(_mla_prep_kernel, n_sub=MLA_SUBTILES),
        grid=(bsz, s // tm),
        in_specs=[pl.BlockSpec((1, tm, d), lambda b, i: (b, i, 0)),
                  pl.BlockSpec((1, 1, tm), lambda b, i: (b, 0, i)),
                  pl.BlockSpec(w_m.shape, full),
                  pl.BlockSpec(w_uq.shape, full),
                  pl.BlockSpec(w_ukv.shape, full)] + vec_specs,
        out_specs=[pl.BlockSpec((1, nh, qk_dim, tm), lambda b, i: (b, 0, 0, i)),
                   pl.BlockSpec((1, nh, tm, qk_dim), lambda b, i: (b, 0, i, 0)),
                   pl.BlockSpec((1, nh, MLA_V, tm), lambda b, i: (b, 0, 0, i))],
        out_shape=[jax.ShapeDtypeStruct((bsz, nh, qk_dim, s), BF16),
                   jax.ShapeDtypeStruct((bsz, nh, s, qk_dim), BF16),
                   jax.ShapeDtypeStruct((bsz, nh, MLA_V, s), BF16)],
        compiler_params=pltpu.CompilerParams(
            dimension_semantics=("parallel", "parallel"),
            vmem_limit_bytes=VMEM_LIMIT),
        name="mla_prep",
    )(h2, pos3, w_m, w_uq, w_ukv, *vecs)


def _attn_kernel(mode_ref, qt_ref, k_ref, vt_ref, nw_ref, o_ref, s_scr, m_scr, l_scr, acc_scr, *,
                 tq, tk):
    i = pl.program_id(2)
    per_q = tq // tk
    kc = lax.broadcasted_iota(jnp.int32, (tk, tq), 0) // ATTN_CHUNK
    qc = lax.broadcasted_iota(jnp.int32, (tk, tq), 1) // ATTN_CHUNK
    visible = qc >= kc

    def key_tile(hh, j):
        start = pl.multiple_of(j * tk, tk)
        return k_ref[0, hh, pl.ds(start, tk), :], vt_ref[0, hh, :, pl.ds(start, tk)]

    def finish():
        for hh in range(2):
            o = (acc_scr[hh] / l_scr[hh]).T
            o = o * _rms_scale(o, MLA_V) * nw_ref[...]
            o_ref[0, :, hh * MLA_V:(hh + 1) * MLA_V] = o.astype(o_ref.dtype)

    l_scr[...] = jnp.zeros(l_scr.shape, F32)
    acc_scr[...] = jnp.zeros(acc_scr.shape, F32)

    @pl.when(mode_ref[0] == 1)
    def _fixed_reference():
        def tile(hh, j, q0=0, mask=None):
            kj, vj = key_tile(hh, j)
            st = _dot(kj, qt_ref[0, hh, :, q0:])
            if mask is not None:
                st = jnp.where(mask, st, -jnp.inf)
            p = jnp.exp2(st)
            l_scr[hh, :, q0:] += jnp.sum(p, axis=0, keepdims=True)
            acc_scr[hh, :, q0:] += _dot(vj, p.astype(BF16))

        def body(jj, _):
            for r in range(per_q):
                tile(0, jj * per_q + r)
                tile(1, jj * per_q + r)
            return 0

        lax.fori_loop(0, i, body, 0)
        for r in range(per_q):
            for hh in range(2):
                tile(hh, i * per_q + r, r * tk, visible[:, :tq - r * tk])
        finish()

    @pl.when(mode_ref[0] == 0)
    def _online():
        def scores(hh, j, q0=0):
            kj, _ = key_tile(hh, j)
            s_scr[hh, :, :tq - q0] = _dot(kj, qt_ref[0, hh, :, q0:])

        def accumulate(hh, j, q0=0, mask=None):
            w = tq - q0
            if mask is not None:
                s_scr[hh, :, :w] = jnp.where(mask, s_scr[hh, :, :w], -jnp.inf)
            m = m_scr[hh, :, q0:]
            m_new = jnp.maximum(m, jnp.max(s_scr[hh, :, :w], axis=0, keepdims=True))
            alpha = jnp.exp2(m - m_new)
            p = jnp.exp2(s_scr[hh, :, :w] - m_new)
            m_scr[hh, :, q0:] = m_new
            l_scr[hh, :, q0:] = alpha * l_scr[hh, :, q0:] + jnp.sum(p, axis=0, keepdims=True)
            _, vj = key_tile(hh, j)
            acc_scr[hh, :, q0:] = alpha * acc_scr[hh, :, q0:] + _dot(vj, p.astype(BF16))

        m_scr[...] = jnp.full(m_scr.shape, -jnp.inf, F32)
        scores(0, 0)

        def body(jj, _):
            for r in range(per_q):
                j = jj * per_q + r
                scores(1, j)
                accumulate(0, j)
                scores(0, j + 1)
                accumulate(1, j)
            return 0

        lax.fori_loop(0, i, body, 0)
        for r in range(per_q):
            q0 = r * tk
            j = i * per_q + r
            scores(1, j, q0)
            accumulate(0, j, q0, visible[:, :tq - q0])
            if r < per_q - 1:
                scores(0, j + 1, q0 + tk)
            accumulate(1, j, q0, visible[:, :tq - q0])
        finish()


def _attn(mode, qt, k, vt, out_norm_w, *, tq=2048, tk=512):
    bsz, nh, qk_dim, s = qt.shape
    tq = min(tq, s)
    tk = min(tk, tq)
    heads = 2
    return pl.pallas_call(
        functools.partial(_attn_kernel, tq=tq, tk=tk),
        grid=(bsz, nh // heads, s // tq),
        in_specs=[pl.BlockSpec(memory_space=pltpu.SMEM),
                  pl.BlockSpec((1, heads, qk_dim, tq), lambda b, h, i: (b, h, 0, i)),
                  pl.BlockSpec((1, heads, s, qk_dim), lambda b, h, i: (b, h, 0, 0)),
                  pl.BlockSpec((1, heads, MLA_V, s), lambda b, h, i: (b, h, 0, 0)),
                  pl.BlockSpec((1, MLA_V), lambda b, h, i: (0, 0))],
        out_specs=pl.BlockSpec((1, tq, heads * MLA_V), lambda b, h, i: (b, i, h)),
        out_shape=jax.ShapeDtypeStruct((bsz, s, nh * MLA_V), BF16),
        scratch_shapes=[pltpu.VMEM((heads, tk, tq), F32),
                        pltpu.VMEM((heads, 1, tq), F32),
                        pltpu.VMEM((heads, 1, tq), F32),
                        pltpu.VMEM((heads, MLA_V, tq), F32)],
        compiler_params=pltpu.CompilerParams(
            dimension_semantics=("parallel", "parallel", "arbitrary"),
            vmem_limit_bytes=VMEM_LIMIT),
        name="attn",
    )(mode, qt, k, vt, out_norm_w)


def _pad_lanes(w, width=LANES):
    return jnp.pad(w, [(0, 0)] * (w.ndim - 1) + [(0, width - w.shape[-1])])


def _swap_halves(w):
    half = w.shape[-1] // 2
    return jnp.concatenate([w[..., half:], w[..., :half]], axis=-1)


def _layer(x, mods, pos3, w_ffn1_in, w_ffn1_out, w_in, conv_w, a_log, dt_bias, gdn_norm_w,
           q_norm_w, w_uq, kv_norm_w, w_ukv, qn_q_nope, qn_q_rope, qn_k_nope, qn_k_rope,
           out_norm_w, w_out, w_ffn2_in, w_ffn2_out):
    sh1, s1, g1, sh2, s2, g2, sh3, s3, g3 = mods
    nh = MLA_HEADS

    o_gz = 2 * GDN_HEADS * GDN_DK + 2 * GDN_HEADS * GDN_DV
    o_ab = o_gz + 2 * GDN_HEADS
    o_cq = o_ab + MLA_Q_LORA
    o_ckv = o_cq + MLA_KV_LORA
    w_gdn = jnp.concatenate([w_in[:, :o_gz], _pad_lanes(w_in[:, o_gz:o_ab])], axis=1).astype(BF16)
    w_kr = w_in[:, o_ckv:]
    w_m = jnp.concatenate([w_in[:, o_ab:o_ckv], w_kr, _swap_halves(w_kr)], axis=1).astype(BF16)
    per_q = MLA_NOPE + MLA_ROPE
    uq_parts = []
    for pair in range(nh // 2):
        wn, wr = [], []
        for hh in (2 * pair, 2 * pair + 1):
            wn.append(w_uq[:, hh * per_q:hh * per_q + MLA_NOPE])
            wr.append(w_uq[:, hh * per_q + MLA_NOPE:(hh + 1) * per_q])
        uq_parts += wn + wr + [_swap_halves(w) for w in wr]
    w_uq_p = jnp.concatenate(uq_parts, axis=1).astype(BF16)
    w_ukv_b = w_ukv.astype(BF16)

    half = MLA_ROPE // 2
    inv_freq = ROPE_BASE ** (-jnp.arange(half, dtype=F32) / half)
    invf = inv_freq.reshape(half, 1)
    eye = jnp.eye(half, dtype=F32)
    zero = jnp.zeros((half, 2 * half), F32)
    trig_sel = jnp.concatenate([jnp.concatenate([eye, eye, zero], axis=1),
                                jnp.concatenate([zero, -eye, eye], axis=1)], axis=0).astype(BF16)
    row = lambda v: v.reshape(1, -1)

    def sq_norm_bound(w_nope, w_rope):
        return MLA_NOPE * jnp.max(w_nope * w_nope) + MLA_ROPE * jnp.max(w_rope * w_rope)
    q_scale = (MLA_NOPE + MLA_ROPE) ** -0.5 * LOG2_E
    bound = (SCORE_BOUND_MARGIN * q_scale
             * jnp.sqrt(sq_norm_bound(qn_q_nope, qn_q_rope) * sq_norm_bound(qn_k_nope, qn_k_rope)))
    use_bound = bound < MAX_FIXED_REFERENCE
    mode = use_bound.astype(jnp.int32).reshape(1)
    lane_id = jnp.arange(LANES)
    ext_even = (lane_id == MLA_ROPE).astype(F32).reshape(1, LANES)
    ext_odd = (lane_id == 0).astype(F32).reshape(1, LANES)
    shift = jnp.where(use_bound, -bound, 0.0)
    twice = lambda v: jnp.concatenate([v, v])
    vecs = [row(q_norm_w), row(kv_norm_w),
            row(qn_q_nope), row(twice(qn_q_rope)), row(twice(_swap_halves(qn_q_rope))),
            row(qn_k_nope), row(jnp.concatenate([qn_k_rope, _swap_halves(qn_k_rope)])),
            invf, trig_sel, ext_even * shift, ext_odd * shift, ext_even, ext_odd]

    x1, h2 = _ffn(x, sh1, s1, g1, w_ffn1_in.astype(BF16), w_ffn1_out.astype(BF16), nxt=(sh2, s2))

    o_a = _gdn(h2, w_gdn, conv_w, row(_pad_lanes(a_log)), row(_pad_lanes(dt_bias)), row(gdn_norm_w))
    qt, k, vt = _mla_prep(h2, pos3, w_m, w_uq_p, w_ukv_b, vecs)
    o_b = _attn(mode, qt, k, vt, row(out_norm_w))

    return _ffn(x1, sh3, s3, g3, w_ffn2_in.astype(BF16), w_ffn2_out.astype(BF16),
                mix=(o_a, o_b, w_out.astype(BF16), g2))


def kernel(x, c, positions, w_ada, b_ada, ffn1_w_in, ffn1_w_out, w_in, gdn_conv_w, gdn_a_log, gdn_dt_bias, gdn_norm_w, mla_q_norm_w, mla_w_uq, mla_kv_norm_w, mla_w_ukv, qkn_q_nope, qkn_q_rope, qkn_k_nope, qkn_k_rope, mla_out_norm_w, w_out, ffn2_w_in, ffn2_w_out):
    bsz, s, d = x.shape
    pos3 = positions.reshape(bsz, 1, s)
    for l in range(w_ada.shape[0]):
        mod = _mod(c, w_ada[l], b_ada[l])
        mods = [m.reshape(bsz, 1, d) for m in jnp.split(mod, N_MOD, axis=-1)]
        x = _layer(x, mods, pos3, ffn1_w_in[l], ffn1_w_out[l], w_in[l], gdn_conv_w[l],
                   gdn_a_log[l], gdn_dt_bias[l], gdn_norm_w[l], mla_q_norm_w[l], mla_w_uq[l],
                   mla_kv_norm_w[l], mla_w_ukv[l], qkn_q_nope[l], qkn_q_rope[l], qkn_k_nope[l],
                   qkn_k_rope[l], mla_out_norm_w[l], w_out[l], ffn2_w_in[l], ffn2_w_out[l])
    return x
```

```python
import functools

import jax
import jax.numpy as jnp
from jax import lax
from jax.experimental import pallas as pl
from jax.experimental.pallas import tpu as pltpu

F32 = jnp.float32
BF16 = jnp.bfloat16

EPS = 1e-6
ATTN_CHUNK = 64
GDN_HEADS = 4
GDN_DK = 128
GDN_DV = 128
CONV_K = 4
MLA_HEADS = 4
MLA_NOPE = 128
MLA_ROPE = 64
MLA_V = 128
MLA_Q_LORA = 384
MLA_KV_LORA = 256
ROPE_BASE = 10000.0
N_MOD = 9
LOG2_E = 1.4426950408889634
MAX_FIXED_REFERENCE = 60.0
SCORE_BOUND_MARGIN = 1.0625

LANES = 128
SUBLANES = 8
VMEM_LIMIT = 56 * 1024 * 1024

GDN_CHUNK = 128
GDN_PROJ_BLOCK = 256
GDN_CONV_PIECE = 512
FFN_SUBTILES = 4
MLA_SUBTILES = 4


def _dot(a, b):
    return jnp.dot(a, b, preferred_element_type=F32)


def _dot_nt(a, b):
    return lax.dot_general(a, b, (((1,), (1,)), ((), ())), preferred_element_type=F32)


def _dot_tn(a, b):
    return lax.dot_general(a, b, (((0,), (0,)), ((), ())), preferred_element_type=F32)


def _silu(x):
    return x * jax.nn.sigmoid(x)


def _rms_scale(x, n):
    return lax.rsqrt(jnp.sum(x * x, axis=-1, keepdims=True) * (1.0 / n) + EPS)


def _cast_kernel(*refs):
    n = len(refs) // 2
    for x_ref, o_ref in zip(refs[:n], refs[n:]):
        o_ref[...] = x_ref[...].astype(o_ref.dtype)


def _to_bf16(weights, *, row_blocks=8):
    specs = []
    for w in weights:
        rows, cols = w.shape
        tr = rows // row_blocks
        assert rows % row_blocks == 0 and tr % (2 * SUBLANES) == 0 and cols % LANES == 0
        specs.append(pl.BlockSpec((tr, cols), lambda i: (i, 0)))
    return pl.pallas_call(
        _cast_kernel,
        grid=(row_blocks,),
        in_specs=specs,
        out_specs=specs,
        out_shape=[jax.ShapeDtypeStruct(w.shape, BF16) for w in weights],
        compiler_params=pltpu.CompilerParams(dimension_semantics=("arbitrary",),
                                             vmem_limit_bytes=VMEM_LIMIT),
        name="cast",
    )(*weights)


def _mod_kernel(c_ref, w_ref, b_ref, o_ref):
    sc = _silu(c_ref[...])
    o_ref[...] = jnp.dot(sc, w_ref[...], precision=lax.Precision.HIGHEST,
                         preferred_element_type=F32) + b_ref[...]


def _mod(c, w_ada, b_ada):
    bsz, d = c.shape
    n = w_ada.shape[1]
    tn = 1024
    return pl.pallas_call(
        _mod_kernel,
        grid=(n // tn,),
        in_specs=[pl.BlockSpec((bsz, d), lambda j: (0, 0)),
                  pl.BlockSpec((d, tn), lambda j: (0, j)),
                  pl.BlockSpec((1, tn), lambda j: (0, j))],
        out_specs=pl.BlockSpec((bsz, tn), lambda j: (0, j)),
        out_shape=jax.ShapeDtypeStruct((bsz, n), F32),
        compiler_params=pltpu.CompilerParams(dimension_semantics=("arbitrary",),
                                             vmem_limit_bytes=VMEM_LIMIT),
        name="mod",
    )(c, w_ada, b_ada.reshape(1, n))


def _ffn_kernel(*refs, has_mix, emit_next, d_model, d_ff, tf, n_sub):
    it = iter(refs)
    x_ref = next(it)
    if has_mix:
        oa_ref, ob_ref, wmix_ref, gmix_ref = next(it), next(it), next(it), next(it)
    shift_ref, scale_ref, gate_ref = next(it), next(it), next(it)
    win_ref, wout_ref = next(it), next(it)
    if emit_next:
        nshift_ref, nscale_ref = next(it), next(it)
    out_ref = next(it)
    if emit_next:
        hn_ref = next(it)
    h_sc, a_sc = next(it), next(it)

    tm = x_ref.shape[1]
    ts = tm // n_sub
    subs = [slice(r * ts, (r + 1) * ts) for r in range(n_sub)]

    def prologue(rows):
        x = x_ref[0, rows]
        if has_mix:
            half = oa_ref.shape[-1]
            y = (_dot(oa_ref[0, rows], wmix_ref[:half, :])
                 + _dot(ob_ref[0, rows], wmix_ref[half:, :]))
            x = x + gmix_ref[0] * y
            out_ref[0, rows] = x
        h_sc[rows] = (x * _rms_scale(x, d_model) * (1.0 + scale_ref[0])
                      + shift_ref[0]).astype(BF16)

    def hidden(rows):
        h = h_sc[rows]
        for c in range(d_ff // tf):
            g = _dot(h, win_ref[:, c * tf:(c + 1) * tf])
            u = _dot(h, win_ref[:, d_ff + c * tf:d_ff + (c + 1) * tf])
            a_sc[rows, c * tf:(c + 1) * tf] = (_silu(g) * u).astype(BF16)

    def epilogue(rows):
        x = out_ref[0, rows] if has_mix else x_ref[0, rows]
        out = x + 0.5 * gate_ref[0] * _dot(a_sc[rows], wout_ref[...])
        out_ref[0, rows] = out
        if emit_next:
            hn = out * _rms_scale(out, d_model) * (1.0 + nscale_ref[0]) + nshift_ref[0]
            hn_ref[0, rows] = hn.astype(BF16)

    prologue(subs[0])
    for r in range(n_sub):
        if r + 1 < n_sub:
            prologue(subs[r + 1])
        hidden(subs[r])
        epilogue(subs[r])


def _resident(shape, index_map):
    return pl.BlockSpec(shape, index_map, pipeline_mode=pl.Buffered(1))


def _ffn(x, shift, scale, gate, w_in_bf, w_out_bf, *, mix=None, nxt=None, tm=1024, tf=256):
    bsz, s, d = x.shape
    ff = w_out_bf.shape[0]
    tm = min(tm, s)
    assert ff % tf == 0 and s % tm == 0 and tm % (FFN_SUBTILES * 2 * SUBLANES) == 0
    has_mix = mix is not None
    emit_next = nxt is not None

    row = lambda b, i: (b, i, 0)
    vec = lambda b, i: (b, 0, 0)
    const = lambda b, i: (0, 0)
    args = [x]
    in_specs = [pl.BlockSpec((1, tm, d), row)]
    if has_mix:
        o_a, o_b, w_mix, g_mix = mix
        args += [o_a, o_b, w_mix, g_mix]
        in_specs += [pl.BlockSpec((1, tm, o_a.shape[-1]), row),
                     pl.BlockSpec((1, tm, o_b.shape[-1]), row),
                     _resident(w_mix.shape, const),
                     pl.BlockSpec((1, 1, d), vec)]
    args += [shift, scale, gate, w_in_bf, w_out_bf]
    in_specs += [pl.BlockSpec((1, 1, d), vec)] * 3
    in_specs += [_resident(w_in_bf.shape, const), _resident(w_out_bf.shape, const)]
    if emit_next:
        args += list(nxt)
        in_specs += [pl.BlockSpec((1, 1, d), vec)] * 2
    out_shape = [jax.ShapeDtypeStruct((bsz, s, d), F32)]
    out_specs = [pl.BlockSpec((1, tm, d), row)]
    if emit_next:
        out_shape.append(jax.ShapeDtypeStruct((bsz, s, d), BF16))
        out_specs.append(pl.BlockSpec((1, tm, d), row))

    res = pl.pallas_call(
        functools.partial(_ffn_kernel, has_mix=has_mix, emit_next=emit_next, d_model=d,
                          d_ff=ff, tf=tf, n_sub=FFN_SUBTILES),
        grid=(bsz, s // tm),
        in_specs=in_specs,
        out_specs=out_specs,
        out_shape=out_shape,
        scratch_shapes=[pltpu.VMEM((tm, d), BF16), pltpu.VMEM((tm, ff), BF16)],
        compiler_params=pltpu.CompilerParams(
            dimension_semantics=("parallel", "parallel"),
            vmem_limit_bytes=VMEM_LIMIT),
        name="ffn_mix" if has_mix else "ffn",
    )(*args)
    return res if emit_next else res[0]


def _unit_lower_inverses(a_mats, ii, jj):
    n = a_mats[0].shape[0]
    eye = (ii == jj).astype(F32)

    def same_block(size):
        return (ii // size) == (jj // size)

    base = same_block(SUBLANES)
    d1 = [jnp.where(base, a, 0.0) for a in a_mats]
    d1b = [d.astype(BF16) for d in d1]
    d2 = [_dot(d, d) for d in d1b]
    d2b = [d.astype(BF16) for d in d2]
    d3 = [_dot(a, b) for a, b in zip(d1b, d2b)]
    d4b = [_dot(d, d).astype(BF16) for d in d2b]
    p1 = [eye - a + b - c for a, b, c in zip(d1, d2, d3)]
    ts = [p + _dot(p.astype(BF16), d) for p, d in zip(p1, d4b)]
    size = SUBLANES
    while size < n:
        below = same_block(2 * size) & jnp.logical_not(same_block(size))
        offs = [jnp.where(below, a, 0.0).astype(BF16) for a in a_mats]
        tbs = [t.astype(BF16) for t in ts]
        xs = [_dot(o, t).astype(BF16) for o, t in zip(offs, tbs)]
        ts = [t - _dot(tb, x) for t, tb, x in zip(ts, tbs, xs)]
        size *= 2
    return ts


def _split3(x):
    hi = x.astype(BF16)
    r1 = x - hi.astype(F32)
    mid = r1.astype(BF16)
    lo = (r1 - mid.astype(F32)).astype(BF16)
    return hi, mid, lo


def _gdn_kernel(h0_ref, hn_ref, wproj_ref, convw_ref, alog_ref, dtb_ref, normw_ref,
                o_ref, proj_scr, xbuf, state, *, chunk):
    nb, _, d = hn_ref.shape
    nh, dk, dv = GDN_HEADS, GDN_DK, GDN_DV
    qk_w = nh * dk
    conv_w = 2 * qk_w + nh * dv
    z_off = conv_w
    ab_off = conv_w + nh * dv
    pad = SUBLANES
    probs = [(b, hh) for b in range(nb) for hh in range(nh)]

    @pl.when(pl.program_id(0) == 0)
    def _():
        state[...] = jnp.zeros_like(state)
        xbuf[:, 0:pad, :] = jnp.zeros((nb, pad, conv_w), F32)
        proj_scr[...] = _dot(h0_ref[...].reshape(nb * chunk, d), wproj_ref[...])

    for b in range(nb):
        xbuf[b, pad:pad + chunk, :] = proj_scr[b * chunk:(b + 1) * chunk, :conv_w]
    ab = proj_scr[:, ab_off:ab_off + LANES]
    z_gates = [_silu(proj_scr[b * chunk:(b + 1) * chunk, z_off + hh * dv:z_off + (hh + 1) * dv])
               for b, hh in probs]
    h_next = hn_ref[...].reshape(nb * chunk, d)
    n_cols = proj_scr.shape[1]
    col_blocks = [(c0, min(c0 + GDN_PROJ_BLOCK, n_cols))
                  for c0 in range(0, n_cols, GDN_PROJ_BLOCK)]

    g_all = -jnp.exp(alog_ref[...]) * jax.nn.softplus(ab + dtb_ref[...])
    beta_all = jax.nn.sigmoid(ab)

    ii = lax.broadcasted_iota(jnp.int32, (chunk, chunk), 0)
    jj = lax.broadcasted_iota(jnp.int32, (chunk, chunk), 1)
    incl = ii >= jj
    strict = ii > jj
    tri = incl.astype(BF16)
    cw = convw_ref[...]

    n_pieces = nb * (conv_w // GDN_CONV_PIECE)
    qkv, gcum, gcum_t = [], [], []
    piece = 0
    for b in range(nb):
        rows = slice(b * chunk, (b + 1) * chunk)
        parts = []
        for p0 in range(0, conv_w, GDN_CONV_PIECE):
            for c0, c1 in col_blocks[piece::n_pieces]:
                proj_scr[:, c0:c1] = _dot(h_next, wproj_ref[:, c0:c1])
            piece += 1
            cols = slice(p0, p0 + GDN_CONV_PIECE)
            xe = xbuf[b, :, cols]
            conv = xe[pad:] * cw[CONV_K - 1:CONV_K, cols]
            for tap in range(1, CONV_K):
                conv = conv + (pltpu.roll(xe, tap, axis=0)[pad:]
                               * cw[CONV_K - 1 - tap:CONV_K - tap, cols])
            parts.append(_silu(conv))
        xbuf[b, 0:pad, :] = xbuf[b, chunk:chunk + pad, :]
        qkv.append(jnp.concatenate(parts, axis=1))
        g_hi, g_mid, g_lo = _split3(g_all[rows, :])
        gc = _dot(tri, g_hi) + _dot(tri, g_mid) + _dot(tri, g_lo)
        gcum.append(gc)
        gcum_t.append(gc.T)

    qs, ks, vs, kbs, g_cols, b_cols, decays = [], [], [], [], [], [], []
    for b, hh in probs:
        q = qkv[b][:, hh * dk:(hh + 1) * dk]
        k = qkv[b][:, qk_w + hh * dk:qk_w + (hh + 1) * dk]
        vs.append(qkv[b][:, 2 * qk_w + hh * dv:2 * qk_w + (hh + 1) * dv])
        q = q * (lax.rsqrt(jnp.sum(q * q, axis=-1, keepdims=True) + EPS) * (dk ** -0.5))
        k = k * lax.rsqrt(jnp.sum(k * k, axis=-1, keepdims=True) + EPS)
        qs.append(q)
        ks.append(k)
        kbs.append(k.astype(BF16))
        g_col = gcum[b][:, hh:hh + 1]
        g_row = gcum_t[b][hh:hh + 1, :]
        g_cols.append(g_col)
        b_cols.append(beta_all[b * chunk:(b + 1) * chunk, nh + hh:nh + hh + 1])
        decays.append(jnp.exp(jnp.where(incl, g_col - g_row, -jnp.inf)))

    kks = [_dot_nt(kb, kb) for kb in kbs]
    qks = [(_dot_nt(q.astype(BF16), kb) * dec).astype(BF16) for q, kb, dec in zip(qs, kbs, decays)]
    a_mats = [jnp.where(strict, bc * kk * dec, 0.0) for bc, kk, dec in zip(b_cols, kks, decays)]
    t_invs = _unit_lower_inverses(a_mats, ii, jj)

    e_gs = [jnp.exp(g) for g in g_cols]
    rhs_us = [v * bc for v, bc in zip(vs, b_cols)]
    rhs_ws = [(k * (bc * eg)).astype(BF16) for k, bc, eg in zip(ks, b_cols, e_gs)]
    g_lasts = [g[chunk - 1:chunk, :] for g in g_cols]
    q_decs = [(q * eg).astype(BF16) for q, eg in zip(qs, e_gs)]
    k_dec_ts = [(k * jnp.exp(gl - g)).T.astype(BF16) for k, gl, g in zip(ks, g_lasts, g_cols)]

    s_olds = [state[b, hh] for b, hh in probs]
    s_bs = [s.astype(BF16) for s in s_olds]
    resid = [(ru - _dot(rw, sb)).astype(BF16) for ru, rw, sb in zip(rhs_us, rhs_ws, s_bs)]
    v_nbs = [_dot(t.astype(BF16), r).astype(BF16) for t, r in zip(t_invs, resid)]
    outs = [_dot(jnp.concatenate([qd, qk], axis=1), jnp.concatenate([sb, vn], axis=0))
            for qd, sb, qk, vn in zip(q_decs, s_bs, qks, v_nbs)]
    for (b, hh), s_old, gl, kdt, vn in zip(probs, s_olds, g_lasts, k_dec_ts, v_nbs):
        state[b, hh] = s_old * jnp.exp(gl) + _dot(kdt, vn)
    for (b, hh), o, zg in zip(probs, outs, z_gates):
        o = o * _rms_scale(o, dv) * normw_ref[...] * zg
        o_ref[b, :, hh * dv:(hh + 1) * dv] = o.astype(o_ref.dtype)


def _gdn(h2, w_proj, conv_w, a_log, dt_bias, norm_w):
    bsz, s, d = h2.shape
    chunk = min(GDN_CHUNK, s)
    n_chunks = s // chunk
    nh, dv = GDN_HEADS, GDN_DV
    cw = conv_w.shape[1]
    full = lambda j: (0, 0)
    return pl.pallas_call(
        functools.partial(_gdn_kernel, chunk=chunk),
        grid=(n_chunks,),
        in_specs=[pl.BlockSpec((bsz, chunk, d), lambda j: (0, 0, 0)),
                  pl.BlockSpec((bsz, chunk, d), lambda j: (0, jnp.minimum(j + 1, n_chunks - 1), 0)),
                  pl.BlockSpec(w_proj.shape, full),
                  pl.BlockSpec(conv_w.shape, full),
                  pl.BlockSpec((1, LANES), full),
                  pl.BlockSpec((1, LANES), full),
                  pl.BlockSpec((1, dv), full)],
        out_specs=pl.BlockSpec((bsz, chunk, nh * dv), lambda j: (0, j, 0)),
        out_shape=jax.ShapeDtypeStruct((bsz, s, nh * dv), BF16),
        scratch_shapes=[pltpu.VMEM((bsz * chunk, w_proj.shape[1]), F32),
                        pltpu.VMEM((bsz, chunk + SUBLANES, cw), F32),
                        pltpu.VMEM((bsz, nh, GDN_DK, dv), F32)],
        compiler_params=pltpu.CompilerParams(
            dimension_semantics=("arbitrary",),
            vmem_limit_bytes=VMEM_LIMIT),
        name="gdn",
    )(h2, h2, w_proj, conv_w, a_log, dt_bias, norm_w)


def _mla_prep_kernel(h_ref, pos_ref, wm_ref, wuq_ref, wukv_ref, qnw_ref, kvnw_ref,
                     wqn_ref, wqr_ref, wqrs_ref, wkn_ref, wkr_ref,
                     invf_ref, sel_ref, qexte_ref, qexto_ref, kexte_ref, kexto_ref,
                     qt_ref, k_ref, vt_ref, *, n_sub):
    nh = MLA_HEADS
    scale = (MLA_NOPE + MLA_ROPE) ** -0.5 * LOG2_E
    o1 = MLA_Q_LORA
    o2 = o1 + MLA_KV_LORA
    pw = 2 * MLA_NOPE + 2 * LANES
    kw = MLA_NOPE + MLA_V
    tm = h_ref.shape[1]
    ts = tm // n_sub
    subs = [slice(r * ts, (r + 1) * ts) for r in range(n_sub)]
    lo = lax.broadcasted_iota(jnp.int32, (1, LANES), 1) < MLA_ROPE

    ang_t = invf_ref[...] * pos_ref[0].astype(F32)
    trig_t = _split3(jnp.concatenate([jnp.cos(ang_t), jnp.sin(ang_t)], axis=0))

    def half_sums(x):
        sq = x * x
        inv_lo = lax.rsqrt(jnp.sum(jnp.where(lo, sq, 0.0), axis=-1, keepdims=True)
                           * (1.0 / MLA_ROPE) + EPS)
        inv_hi = lax.rsqrt(jnp.sum(jnp.where(lo, 0.0, sq), axis=-1, keepdims=True)
                           * (1.0 / MLA_ROPE) + EPS)
        return jnp.where(lo, inv_lo, inv_hi)

    def latent(rows):
        return _dot(h_ref[0, rows], wm_ref[...])

    def up_project(lat):
        cq = lat[:, :o1]
        ckv = lat[:, o1:o2]
        cqn = (cq * _rms_scale(cq, MLA_Q_LORA) * qnw_ref[...]).astype(BF16)
        ckvn = (ckv * _rms_scale(ckv, MLA_KV_LORA) * kvnw_ref[...]).astype(BF16)
        return _dot(cqn, wuq_ref[...]), _dot(ckvn, wukv_ref[...])

    def finish(rows, lat, qf, kvf):
        parts = [_dot_tn(part[:, rows], sel_ref[...]) for part in trig_t]
        trig = (parts[0] + parts[1]) + parts[2]
        trig_r = pltpu.roll(trig, MLA_ROPE, axis=1)
        cos4 = jnp.where(lo, trig, trig_r)
        sin4 = jnp.where(lo, trig_r, trig)

        kr2 = lat[:, o2:o2 + LANES]
        a = kr2 * wkr_ref[...] * trig
        k_rope = (a + pltpu.roll(a, MLA_ROPE, axis=1)) * lax.rsqrt(
            jnp.sum(jnp.where(lo, kr2 * kr2, 0.0), axis=-1, keepdims=True) * (1.0 / MLA_ROPE) + EPS)
        k_rope_even = jnp.where(lo, k_rope, 0.0) + kexte_ref[...]
        k_rope_odd = jnp.where(lo, 0.0, k_rope) + kexto_ref[...]
        for pair in range(nh // 2):
            base = pair * pw
            qr2 = qf[:, base + 2 * MLA_NOPE:base + 2 * MLA_NOPE + LANES]
            qrs2 = qf[:, base + 2 * MLA_NOPE + LANES:base + pw]
            q_rope2 = ((qr2 * wqr_ref[...] * cos4 + qrs2 * wqrs_ref[...] * sin4)
                       * (half_sums(qr2) * scale))
            for odd in range(2):
                hh = 2 * pair + odd
                qn = qf[:, base + odd * MLA_NOPE:base + (odd + 1) * MLA_NOPE]
                qn = qn * _rms_scale(qn, MLA_NOPE) * (wqn_ref[...] * scale)
                if odd:
                    q_rope = jnp.where(lo, 0.0, q_rope2) + qexto_ref[...]
                else:
                    q_rope = jnp.where(lo, q_rope2, 0.0) + qexte_ref[...]
                qt_ref[0, hh, :, rows] = jnp.concatenate([qn, q_rope], axis=1).T.astype(BF16)
                kn = kvf[:, hh * kw:hh * kw + MLA_NOPE]
                v = kvf[:, hh * kw + MLA_NOPE:(hh + 1) * kw]
                kn = kn * _rms_scale(kn, MLA_NOPE) * wkn_ref[...]
                k_rope_h = k_rope_odd if odd else k_rope_even
                k_ref[0, hh, rows, :] = jnp.concatenate([kn, k_rope_h], axis=1).astype(BF16)
                vt_ref[0, hh, :, rows] = v.T.astype(BF16)

    lats = [latent(subs[0])]
    ups = []
    for r in range(n_sub):
        if r + 1 < n_sub:
            lats.append(latent(subs[r + 1]))
        ups.append(up_project(lats[r]))
        if r >= 1:
            finish(subs[r - 1], lats[r - 1], *ups[r - 1])
    finish(subs[n_sub - 1], lats[n_sub - 1], *ups[n_sub - 1])


def _mla_prep(h2, pos3, w_m, w_uq, w_ukv, vecs, *, tm=1024):
    bsz, s, d = h2.shape
    tm = min(tm, s)
    nh = MLA_HEADS
    qk_dim = MLA_NOPE + LANES
    full = lambda b, i: (0, 0)
    vec_specs = [pl.BlockSpec(v.shape, full) for v in vecs]
    assert tm % (MLA_SUBTILES * LANES) == 0
    return pl.pallas_call(
        functools.partial(_mla_prep_kernel, n_sub=MLA_SUBTILES),
        grid=(bsz, s // tm),
        in_specs=[pl.BlockSpec((1, tm, d), lambda b, i: (b, i, 0)),
                  pl.BlockSpec((1, 1, tm), lambda b, i: (b, 0, i)),
                  pl.BlockSpec(w_m.shape, full),
                  pl.BlockSpec(w_uq.shape, full),
                  pl.BlockSpec(w_ukv.shape, full)] + vec_specs,
        out_specs=[pl.BlockSpec((1, nh, qk_dim, tm), lambda b, i: (b, 0, 0, i)),
                   pl.BlockSpec((1, nh, tm, qk_dim), lambda b, i: (b, 0, i, 0)),
                   pl.BlockSpec((1, nh, MLA_V, tm), lambda b, i: (b, 0, 0, i))],
        out_shape=[jax.ShapeDtypeStruct((bsz, nh, qk_dim, s), BF16),
                   jax.ShapeDtypeStruct((bsz, nh, s, qk_dim), BF16),
                   jax.ShapeDtypeStruct((bsz, nh, MLA_V, s), BF16)],
        compiler_params=pltpu.CompilerParams(
            dimension_semantics=("parallel", "parallel"),
            vmem_limit_bytes=VMEM_LIMIT),
        name="mla_prep",
    )(h2, pos3, w_m, w_uq, w_ukv, *vecs)


def _attn_kernel(mode_ref, qt_ref, k_ref, vt_ref, nw_ref, o_ref, s_scr, m_scr, l_scr, acc_scr, *,
                 tq, tk):
    i = pl.program_id(2)
    per_q = tq // tk
    kc = lax.broadcasted_iota(jnp.int32, (tk, tq), 0) // ATTN_CHUNK
    qc = lax.broadcasted_iota(jnp.int32, (tk, tq), 1) // ATTN_CHUNK
    visible = qc >= kc

    def key_tile(hh, j):
        start = pl.multiple_of(j * tk, tk)
        return k_ref[0, hh, pl.ds(start, tk), :], vt_ref[0, hh, :, pl.ds(start, tk)]

    def finish():
        for hh in range(2):
            o = (acc_scr[hh] / l_scr[hh]).T
            o = o * _rms_scale(o, MLA_V) * nw_ref[...]
            o_ref[0, :, hh * MLA_V:(hh + 1) * MLA_V] = o.astype(o_ref.dtype)

    l_scr[...] = jnp.zeros(l_scr.shape, F32)
    acc_scr[...] = jnp.zeros(acc_scr.shape, F32)

    @pl.when(mode_ref[0] == 1)
    def _fixed_reference():
        def tile(hh, j, q0=0, mask=None):
            kj, vj = key_tile(hh, j)
            st = _dot(kj, qt_ref[0, hh, :, q0:])
            if mask is not None:
                st = jnp.where(mask, st, -jnp.inf)
            p = jnp.exp2(st)
            l_scr[hh, :, q0:] += jnp.sum(p, axis=0, keepdims=True)
            acc_scr[hh, :, q0:] += _dot(vj, p.astype(BF16))

        def body(jj, _):
            for r in range(per_q):
                tile(0, jj * per_q + r)
                tile(1, jj * per_q + r)
            return 0

        lax.fori_loop(0, i, body, 0)
        for r in range(per_q):
            for hh in range(2):
                tile(hh, i * per_q + r, r * tk, visible[:, :tq - r * tk])
        finish()

    @pl.when(mode_ref[0] == 0)
    def _online():
        def scores(hh, j, q0=0):
            kj, _ = key_tile(hh, j)
            s_scr[hh, :, :tq - q0] = _dot(kj, qt_ref[0, hh, :, q0:])

        def accumulate(hh, j, q0=0, mask=None):
            w = tq - q0
            if mask is not None:
                s_scr[hh, :, :w] = jnp.where(mask, s_scr[hh, :, :w], -jnp.inf)
            m = m_scr[hh, :, q0:]
            m_new = jnp.maximum(m, jnp.max(s_scr[hh, :, :w], axis=0, keepdims=True))
            alpha = jnp.exp2(m - m_new)
            p = jnp.exp2(s_scr[hh, :, :w] - m_new)
            m_scr[hh, :, q0:] = m_new
            l_scr[hh, :, q0:] = alpha * l_scr[hh, :, q0:] + jnp.sum(p, axis=0, keepdims=True)
            _, vj = key_tile(hh, j)
            acc_scr[hh, :, q0:] = alpha * acc_scr[hh, :, q0:] + _dot(vj, p.astype(BF16))

        m_scr[...] = jnp.full(m_scr.shape, -jnp.inf, F32)
        scores(0, 0)

        def body(jj, _):
            for r in range(per_q):
                j = jj * per_q + r
                scores(1, j)
                accumulate(0, j)
                scores(0, j + 1)
                accumulate(1, j)
            return 0

        lax.fori_loop(0, i, body, 0)
        for r in range(per_q):
            q0 = r * tk
            j = i * per_q + r
            scores(1, j, q0)
            accumulate(0, j, q0, visible[:, :tq - q0])
            if r < per_q - 1:
                scores(0, j + 1, q0 + tk)
            accumulate(1, j, q0, visible[:, :tq - q0])
        finish()


def _attn(mode, qt, k, vt, out_norm_w, *, tq=2048, tk=512):
    bsz, nh, qk_dim, s = qt.shape
    tq = min(tq, s)
    tk = min(tk, tq)
    heads = 2
    return pl.pallas_call(
        functools.partial(_attn_kernel, tq=tq, tk=tk),
        grid=(bsz, nh // heads, s // tq),
        in_specs=[pl.BlockSpec(memory_space=pltpu.SMEM),
                  pl.BlockSpec((1, heads, qk_dim, tq), lambda b, h, i: (b, h, 0, i)),
                  pl.BlockSpec((1, heads, s, qk_dim), lambda b, h, i: (b, h, 0, 0)),
                  pl.BlockSpec((1, heads, MLA_V, s), lambda b, h, i: (b, h, 0, 0)),
                  pl.BlockSpec((1, MLA_V), lambda b, h, i: (0, 0))],
        out_specs=pl.BlockSpec((1, tq, heads * MLA_V), lambda b, h, i: (b, i, h)),
        out_shape=jax.ShapeDtypeStruct((bsz, s, nh * MLA_V), BF16),
        scratch_shapes=[pltpu.VMEM((heads, tk, tq), F32),
                        pltpu.VMEM((heads, 1, tq), F32),
                        pltpu.VMEM((heads, 1, tq), F32),
                        pltpu.VMEM((heads, MLA_V, tq), F32)],
        compiler_params=pltpu.CompilerParams(
            dimension_semantics=("parallel", "parallel", "arbitrary"),
            vmem_limit_bytes=VMEM_LIMIT),
        name="attn",
    )(mode, qt, k, vt, out_norm_w)


def _pad_lanes(w, width=LANES):
    return jnp.pad(w, [(0, 0)] * (w.ndim - 1) + [(0, width - w.shape[-1])])


def _swap_halves(w):
    half = w.shape[-1] // 2
    return jnp.concatenate([w[..., half:], w[..., :half]], axis=-1)


def _layer(x, mods, pos3, w_ffn1_in, w_ffn1_out, w_in, conv_w, a_log, dt_bias, gdn_norm_w,
           q_norm_w, w_uq, kv_norm_w, w_ukv, qn_q_nope, qn_q_rope, qn_k_nope, qn_k_rope,
           out_norm_w, w_out, w_ffn2_in, w_ffn2_out):
    sh1, s1, g1, sh2, s2, g2, sh3, s3, g3 = mods
    nh = MLA_HEADS

    o_gz = 2 * GDN_HEADS * GDN_DK + 2 * GDN_HEADS * GDN_DV
    o_ab = o_gz + 2 * GDN_HEADS
    o_cq = o_ab + MLA_Q_LORA
    o_ckv = o_cq + MLA_KV_LORA
    w_gdn = jnp.concatenate([w_in[:, :o_gz], _pad_lanes(w_in[:, o_gz:o_ab])], axis=1).astype(BF16)
    w_kr = w_in[:, o_ckv:]
    w_m = jnp.concatenate([w_in[:, o_ab:o_ckv], w_kr, _swap_halves(w_kr)], axis=1).astype(BF16)
    per_q = MLA_NOPE + MLA_ROPE
    uq_parts = []
    for pair in range(nh // 2):
        wn, wr = [], []
        for hh in (2 * pair, 2 * pair + 1):
            wn.append(w_uq[:, hh * per_q:hh * per_q + MLA_NOPE])
            wr.append(w_uq[:, hh * per_q + MLA_NOPE:(hh + 1) * per_q])
        uq_parts += wn + wr + [_swap_halves(w) for w in wr]
    w_uq_p = jnp.concatenate(uq_parts, axis=1).astype(BF16)
    w_ukv_b = w_ukv.astype(BF16)

    half = MLA_ROPE // 2
    inv_freq = ROPE_BASE ** (-jnp.arange(half, dtype=F32) / half)
    invf = inv_freq.reshape(half, 1)
    eye = jnp.eye(half, dtype=F32)
    zero = jnp.zeros((half, 2 * half), F32)
    trig_sel = jnp.concatenate([jnp.concatenate([eye, eye, zero], axis=1),
                                jnp.concatenate([zero, -eye, eye], axis=1)], axis=0).astype(BF16)
    row = lambda v: v.reshape(1, -1)

    def sq_norm_bound(w_nope, w_rope):
        return MLA_NOPE * jnp.max(w_nope * w_nope) + MLA_ROPE * jnp.max(w_rope * w_rope)
    q_scale = (MLA_NOPE + MLA_ROPE) ** -0.5 * LOG2_E
    bound = (SCORE_BOUND_MARGIN * q_scale
             * jnp.sqrt(sq_norm_bound(qn_q_nope, qn_q_rope) * sq_norm_bound(qn_k_nope, qn_k_rope)))
    use_bound = bound < MAX_FIXED_REFERENCE
    mode = use_bound.astype(jnp.int32).reshape(1)
    lane_id = jnp.arange(LANES)
    ext_even = (lane_id == MLA_ROPE).astype(F32).reshape(1, LANES)
    ext_odd = (lane_id == 0).astype(F32).reshape(1, LANES)
    shift = jnp.where(use_bound, -bound, 0.0)
    twice = lambda v: jnp.concatenate([v, v])
    vecs = [row(q_norm_w), row(kv_norm_w),
            row(qn_q_nope), row(twice(qn_q_rope)), row(twice(_swap_halves(qn_q_rope))),
            row(qn_k_nope), row(jnp.concatenate([qn_k_rope, _swap_halves(qn_k_rope)])),
            invf, trig_sel, ext_even * shift, ext_odd * shift, ext_even, ext_odd]

    w1_in, w1_out, w2_in, w2_out = _to_bf16([w_ffn1_in, w_ffn1_out, w_ffn2_in, w_ffn2_out])
    x1, h2 = _ffn(x, sh1, s1, g1, w1_in, w1_out, nxt=(sh2, s2))

    o_a = _gdn(h2, w_gdn, conv_w, row(_pad_lanes(a_log)), row(_pad_lanes(dt_bias)), row(gdn_norm_w))
    qt, k, vt = _mla_prep(h2, pos3, w_m, w_uq_p, w_ukv_b, vecs)
    o_b = _attn(mode, qt, k, vt, row(out_norm_w))

    return _ffn(x1, sh3, s3, g3, w2_in, w2_out, mix=(o_a, o_b, w_out.astype(BF16), g2))


def kernel(x, c, positions, w_ada, b_ada, ffn1_w_in, ffn1_w_out, w_in, gdn_conv_w, gdn_a_log, gdn_dt_bias, gdn_norm_w, mla_q_norm_w, mla_w_uq, mla_kv_norm_w, mla_w_ukv, qkn_q_nope, qkn_q_rope, qkn_k_nope, qkn_k_rope, mla_out_norm_w, w_out, ffn2_w_in, ffn2_w_out):
    bsz, s, d = x.shape
    pos3 = positions.reshape(bsz, 1, s)
    for l in range(w_ada.shape[0]):
        mod = _mod(c, w_ada[l], b_ada[l])
        mods = [m.reshape(bsz, 1, d) for m in jnp.split(mod, N_MOD, axis=-1)]
        x = _layer(x, mods, pos3, ffn1_w_in[l], ffn1_w_out[l], w_in[l], gdn_conv_w[l],
                   gdn_a_log[l], gdn_dt_bias[l], gdn_norm_w[l], mla_q_norm_w[l], mla_w_uq[l],
                   mla_kv_norm_w[l], mla_w_ukv[l], qkn_q_nope[l], qkn_q_rope[l], qkn_k_nope[l],
                   qkn_k_rope[l], mla_out_norm_w[l], w_out[l], ffn2_w_in[l], ffn2_w_out[l])
    return x
```

```python
import functools

import jax
import jax.numpy as jnp
from jax import lax
from jax.experimental import pallas as pl
from jax.experimental.pallas import tpu as pltpu

F32 = jnp.float32
BF16 = jnp.bfloat16

EPS = 1e-6
ATTN_CHUNK = 64
GDN_HEADS = 4
GDN_DK = 128
GDN_DV = 128
CONV_K = 4
MLA_HEADS = 4
MLA_NOPE = 128
MLA_ROPE = 64
MLA_V = 128
MLA_Q_LORA = 384
MLA_KV_LORA = 256
ROPE_BASE = 10000.0
N_MOD = 9
LOG2_E = 1.4426950408889634
MAX_FIXED_REFERENCE = 60.0
SCORE_BOUND_MARGIN = 1.0625

LANES = 128
SUBLANES = 8
VMEM_LIMIT = 56 * 1024 * 1024

GDN_CHUNK = 128
GDN_PROJ_BLOCK = 256
GDN_CONV_PIECE = 512
FFN_SUBTILES = 4
MLA_SUBTILES = 4


def _dot(a, b):
    return jnp.dot(a, b, preferred_element_type=F32)


def _dot_nt(a, b):
    return lax.dot_general(a, b, (((1,), (1,)), ((), ())), preferred_element_type=F32)


def _dot_tn(a, b):
    return lax.dot_general(a, b, (((0,), (0,)), ((), ())), preferred_element_type=F32)


def _silu(x):
    return x * jax.nn.sigmoid(x)


def _rms_scale(x, n):
    return lax.rsqrt(jnp.sum(x * x, axis=-1, keepdims=True) * (1.0 / n) + EPS)


def _cast_kernel(*refs):
    n = len(refs) // 2
    for x_ref, o_ref in zip(refs[:n], refs[n:]):
        o_ref[...] = x_ref[...].astype(o_ref.dtype)


def _to_bf16(weights, *, row_blocks=8):
    specs = []
    for w in weights:
        rows, cols = w.shape
        tr = rows // row_blocks
        assert rows % row_blocks == 0 and tr % (2 * SUBLANES) == 0 and cols % LANES == 0
        specs.append(pl.BlockSpec((tr, cols), lambda i: (i, 0)))
    return pl.pallas_call(
        _cast_kernel,
        grid=(row_blocks,),
        in_specs=specs,
        out_specs=specs,
        out_shape=[jax.ShapeDtypeStruct(w.shape, BF16) for w in weights],
        compiler_params=pltpu.CompilerParams(dimension_semantics=("arbitrary",),
                                             vmem_limit_bytes=VMEM_LIMIT),
        name="cast",
    )(*weights)


def _split_w_in_kernel(w_ref, gdn_ref, m_ref):
    o_gz = 2 * GDN_HEADS * GDN_DK + 2 * GDN_HEADS * GDN_DV
    o_ab = o_gz + 2 * GDN_HEADS
    o_ckv = o_ab + MLA_Q_LORA + MLA_KV_LORA
    w = w_ref[...]
    gdn_ref[...] = w[:, :o_gz + LANES].astype(BF16)
    kr = w[:, o_ckv:]
    half = MLA_ROPE // 2
    m_ref[...] = jnp.concatenate([w[:, o_ab:o_ckv], kr, kr[:, half:], kr[:, :half]],
                                 axis=1).astype(BF16)


def _split_w_in(w_in, *, row_blocks=4):
    rows, cols = w_in.shape
    tr = rows // row_blocks
    n_gdn = 2 * GDN_HEADS * GDN_DK + 2 * GDN_HEADS * GDN_DV + LANES
    n_m = MLA_Q_LORA + MLA_KV_LORA + 2 * MLA_ROPE
    assert rows % row_blocks == 0 and tr % (2 * SUBLANES) == 0
    return pl.pallas_call(
        _split_w_in_kernel,
        grid=(row_blocks,),
        in_specs=[pl.BlockSpec((tr, cols), lambda i: (i, 0))],
        out_specs=[pl.BlockSpec((tr, n_gdn), lambda i: (i, 0)),
                   pl.BlockSpec((tr, n_m), lambda i: (i, 0))],
        out_shape=[jax.ShapeDtypeStruct((rows, n_gdn), BF16),
                   jax.ShapeDtypeStruct((rows, n_m), BF16)],
        compiler_params=pltpu.CompilerParams(dimension_semantics=("arbitrary",),
                                             vmem_limit_bytes=VMEM_LIMIT),
        name="split_w_in",
    )(w_in)


def _mod_kernel(c_ref, w_ref, b_ref, o_ref):
    sc = _silu(c_ref[...])
    o_ref[...] = jnp.dot(sc, w_ref[...], precision=lax.Precision.HIGHEST,
                         preferred_element_type=F32) + b_ref[...]


def _mod(c, w_ada, b_ada):
    bsz, d = c.shape
    n = w_ada.shape[1]
    tn = 1024
    return pl.pallas_call(
        _mod_kernel,
        grid=(n // tn,),
        in_specs=[pl.BlockSpec((bsz, d), lambda j: (0, 0)),
                  pl.BlockSpec((d, tn), lambda j: (0, j)),
                  pl.BlockSpec((1, tn), lambda j: (0, j))],
        out_specs=pl.BlockSpec((bsz, tn), lambda j: (0, j)),
        out_shape=jax.ShapeDtypeStruct((bsz, n), F32),
        compiler_params=pltpu.CompilerParams(dimension_semantics=("arbitrary",),
                                             vmem_limit_bytes=VMEM_LIMIT),
        name="mod",
    )(c, w_ada, b_ada.reshape(1, n))


def _ffn_kernel(*refs, has_mix, emit_next, d_model, d_ff, tf, n_sub):
    it = iter(refs)
    x_ref = next(it)
    if has_mix:
        oa_ref, ob_ref, wmix_ref, gmix_ref = next(it), next(it), next(it), next(it)
    shift_ref, scale_ref, gate_ref = next(it), next(it), next(it)
    win_ref, wout_ref = next(it), next(it)
    if emit_next:
        nshift_ref, nscale_ref = next(it), next(it)
    out_ref = next(it)
    if emit_next:
        hn_ref = next(it)
    h_sc, a_sc = next(it), next(it)

    tm = x_ref.shape[1]
    ts = tm // n_sub
    subs = [slice(r * ts, (r + 1) * ts) for r in range(n_sub)]

    def prologue(rows):
        x = x_ref[0, rows]
        if has_mix:
            half = oa_ref.shape[-1]
            y = (_dot(oa_ref[0, rows], wmix_ref[:half, :])
                 + _dot(ob_ref[0, rows], wmix_ref[half:, :]))
            x = x + gmix_ref[0] * y
            out_ref[0, rows] = x
        h_sc[rows] = (x * _rms_scale(x, d_model) * (1.0 + scale_ref[0])
                      + shift_ref[0]).astype(BF16)

    def hidden(rows):
        h = h_sc[rows]
        for c in range(d_ff // tf):
            g = _dot(h, win_ref[:, c * tf:(c + 1) * tf])
            u = _dot(h, win_ref[:, d_ff + c * tf:d_ff + (c + 1) * tf])
            a_sc[rows, c * tf:(c + 1) * tf] = (_silu(g) * u).astype(BF16)

    def epilogue(rows):
        x = out_ref[0, rows] if has_mix else x_ref[0, rows]
        out = x + 0.5 * gate_ref[0] * _dot(a_sc[rows], wout_ref[...])
        out_ref[0, rows] = out
        if emit_next:
            hn = out * _rms_scale(out, d_model) * (1.0 + nscale_ref[0]) + nshift_ref[0]
            hn_ref[0, rows] = hn.astype(BF16)

    prologue(subs[0])
    for r in range(n_sub):
        if r + 1 < n_sub:
            prologue(subs[r + 1])
        hidden(subs[r])
        epilogue(subs[r])


def _resident(shape, index_map):
    return pl.BlockSpec(shape, index_map, pipeline_mode=pl.Buffered(1))


def _ffn(x, shift, scale, gate, w_in_bf, w_out_bf, *, mix=None, nxt=None, tm=1024, tf=256):
    bsz, s, d = x.shape
    ff = w_out_bf.shape[0]
    tm = min(tm, s)
    assert ff % tf == 0 and s % tm == 0 and tm % (FFN_SUBTILES * 2 * SUBLANES) == 0
    has_mix = mix is not None
    emit_next = nxt is not None

    row = lambda b, i: (b, i, 0)
    vec = lambda b, i: (b, 0, 0)
    const = lambda b, i: (0, 0)
    args = [x]
    in_specs = [pl.BlockSpec((1, tm, d), row)]
    if has_mix:
        o_a, o_b, w_mix, g_mix = mix
        args += [o_a, o_b, w_mix, g_mix]
        in_specs += [pl.BlockSpec((1, tm, o_a.shape[-1]), row),
                     pl.BlockSpec((1, tm, o_b.shape[-1]), row),
                     _resident(w_mix.shape, const),
                     pl.BlockSpec((1, 1, d), vec)]
    args += [shift, scale, gate, w_in_bf, w_out_bf]
    in_specs += [pl.BlockSpec((1, 1, d), vec)] * 3
    in_specs += [_resident(w_in_bf.shape, const), _resident(w_out_bf.shape, const)]
    if emit_next:
        args += list(nxt)
        in_specs += [pl.BlockSpec((1, 1, d), vec)] * 2
    out_shape = [jax.ShapeDtypeStruct((bsz, s, d), F32)]
    out_specs = [pl.BlockSpec((1, tm, d), row)]
    if emit_next:
        out_shape.append(jax.ShapeDtypeStruct((bsz, s, d), BF16))
        out_specs.append(pl.BlockSpec((1, tm, d), row))

    res = pl.pallas_call(
        functools.partial(_ffn_kernel, has_mix=has_mix, emit_next=emit_next, d_model=d,
                          d_ff=ff, tf=tf, n_sub=FFN_SUBTILES),
        grid=(bsz, s // tm),
        in_specs=in_specs,
        out_specs=out_specs,
        out_shape=out_shape,
        scratch_shapes=[pltpu.VMEM((tm, d), BF16), pltpu.VMEM((tm, ff), BF16)],
        compiler_params=pltpu.CompilerParams(
            dimension_semantics=("parallel", "parallel"),
            vmem_limit_bytes=VMEM_LIMIT),
        name="ffn_mix" if has_mix else "ffn",
    )(*args)
    return res if emit_next else res[0]


def _unit_lower_inverses(a_mats, ii, jj):
    n = a_mats[0].shape[0]
    eye = (ii == jj).astype(F32)

    def same_block(size):
        return (ii // size) == (jj // size)

    base = same_block(SUBLANES)
    d1 = [jnp.where(base, a, 0.0) for a in a_mats]
    d1b = [d.astype(BF16) for d in d1]
    d2 = [_dot(d, d) for d in d1b]
    d2b = [d.astype(BF16) for d in d2]
    d3 = [_dot(a, b) for a, b in zip(d1b, d2b)]
    d4b = [_dot(d, d).astype(BF16) for d in d2b]
    p1 = [eye - a + b - c for a, b, c in zip(d1, d2, d3)]
    ts = [p + _dot(p.astype(BF16), d) for p, d in zip(p1, d4b)]
    size = SUBLANES
    while size < n:
        below = same_block(2 * size) & jnp.logical_not(same_block(size))
        offs = [jnp.where(below, a, 0.0).astype(BF16) for a in a_mats]
        tbs = [t.astype(BF16) for t in ts]
        xs = [_dot(o, t).astype(BF16) for o, t in zip(offs, tbs)]
        ts = [t - _dot(tb, x) for t, tb, x in zip(ts, tbs, xs)]
        size *= 2
    return ts


def _split3(x):
    hi = x.astype(BF16)
    r1 = x - hi.astype(F32)
    mid = r1.astype(BF16)
    lo = (r1 - mid.astype(F32)).astype(BF16)
    return hi, mid, lo


def _gdn_kernel(h0_ref, hn_ref, wproj_ref, convw_ref, alog_ref, dtb_ref, normw_ref,
                o_ref, proj_scr, xbuf, state, *, chunk):
    nb, _, d = hn_ref.shape
    nh, dk, dv = GDN_HEADS, GDN_DK, GDN_DV
    qk_w = nh * dk
    conv_w = 2 * qk_w + nh * dv
    z_off = conv_w
    ab_off = conv_w + nh * dv
    pad = SUBLANES
    probs = [(b, hh) for b in range(nb) for hh in range(nh)]

    @pl.when(pl.program_id(0) == 0)
    def _():
        state[...] = jnp.zeros_like(state)
        xbuf[:, 0:pad, :] = jnp.zeros((nb, pad, conv_w), F32)
        proj_scr[...] = _dot(h0_ref[...].reshape(nb * chunk, d), wproj_ref[...])

    for b in range(nb):
        xbuf[b, pad:pad + chunk, :] = proj_scr[b * chunk:(b + 1) * chunk, :conv_w]
    ab = proj_scr[:, ab_off:ab_off + LANES]
    z_gates = [_silu(proj_scr[b * chunk:(b + 1) * chunk, z_off + hh * dv:z_off + (hh + 1) * dv])
               for b, hh in probs]
    h_next = hn_ref[...].reshape(nb * chunk, d)
    n_cols = proj_scr.shape[1]
    col_blocks = [(c0, min(c0 + GDN_PROJ_BLOCK, n_cols))
                  for c0 in range(0, n_cols, GDN_PROJ_BLOCK)]

    g_all = -jnp.exp(alog_ref[...]) * jax.nn.softplus(ab + dtb_ref[...])
    beta_all = jax.nn.sigmoid(ab)

    ii = lax.broadcasted_iota(jnp.int32, (chunk, chunk), 0)
    jj = lax.broadcasted_iota(jnp.int32, (chunk, chunk), 1)
    incl = ii >= jj
    strict = ii > jj
    tri = incl.astype(BF16)
    cw = convw_ref[...]

    n_pieces = nb * (conv_w // GDN_CONV_PIECE)
    qkv, gcum, gcum_t = [], [], []
    piece = 0
    for b in range(nb):
        rows = slice(b * chunk, (b + 1) * chunk)
        parts = []
        for p0 in range(0, conv_w, GDN_CONV_PIECE):
            for c0, c1 in col_blocks[piece::n_pieces]:
                proj_scr[:, c0:c1] = _dot(h_next, wproj_ref[:, c0:c1])
            piece += 1
            cols = slice(p0, p0 + GDN_CONV_PIECE)
            xe = xbuf[b, :, cols]
            conv = xe[pad:] * cw[CONV_K - 1:CONV_K, cols]
            for tap in range(1, CONV_K):
                conv = conv + (pltpu.roll(xe, tap, axis=0)[pad:]
                               * cw[CONV_K - 1 - tap:CONV_K - tap, cols])
            parts.append(_silu(conv))
        xbuf[b, 0:pad, :] = xbuf[b, chunk:chunk + pad, :]
        qkv.append(jnp.concatenate(parts, axis=1))
        g_hi, g_mid, g_lo = _split3(g_all[rows, :])
        gc = _dot(tri, g_hi) + _dot(tri, g_mid) + _dot(tri, g_lo)
        gcum.append(gc)
        gcum_t.append(gc.T)

    qs, ks, vs, kbs, g_cols, b_cols, decays = [], [], [], [], [], [], []
    for b, hh in probs:
        q = qkv[b][:, hh * dk:(hh + 1) * dk]
        k = qkv[b][:, qk_w + hh * dk:qk_w + (hh + 1) * dk]
        vs.append(qkv[b][:, 2 * qk_w + hh * dv:2 * qk_w + (hh + 1) * dv])
        q = q * (lax.rsqrt(jnp.sum(q * q, axis=-1, keepdims=True) + EPS) * (dk ** -0.5))
        k = k * lax.rsqrt(jnp.sum(k * k, axis=-1, keepdims=True) + EPS)
        qs.append(q)
        ks.append(k)
        kbs.append(k.astype(BF16))
        g_col = gcum[b][:, hh:hh + 1]
        g_row = gcum_t[b][hh:hh + 1, :]
        g_cols.append(g_col)
        b_cols.append(beta_all[b * chunk:(b + 1) * chunk, nh + hh:nh + hh + 1])
        decays.append(jnp.exp(jnp.where(incl, g_col - g_row, -jnp.inf)))

    kks = [_dot_nt(kb, kb) for kb in kbs]
    qks = [(_dot_nt(q.astype(BF16), kb) * dec).astype(BF16) for q, kb, dec in zip(qs, kbs, decays)]
    a_mats = [jnp.where(strict, bc * kk * dec, 0.0) for bc, kk, dec in zip(b_cols, kks, decays)]
    t_invs = _unit_lower_inverses(a_mats, ii, jj)

    e_gs = [jnp.exp(g) for g in g_cols]
    rhs_us = [v * bc for v, bc in zip(vs, b_cols)]
    rhs_ws = [(k * (bc * eg)).astype(BF16) for k, bc, eg in zip(ks, b_cols, e_gs)]
    g_lasts = [g[chunk - 1:chunk, :] for g in g_cols]
    q_decs = [(q * eg).astype(BF16) for q, eg in zip(qs, e_gs)]
    k_dec_ts = [(k * jnp.exp(gl - g)).T.astype(BF16) for k, gl, g in zip(ks, g_lasts, g_cols)]

    s_olds = [state[b, hh] for b, hh in probs]
    s_bs = [s.astype(BF16) for s in s_olds]
    resid = [(ru - _dot(rw, sb)).astype(BF16) for ru, rw, sb in zip(rhs_us, rhs_ws, s_bs)]
    v_nbs = [_dot(t.astype(BF16), r).astype(BF16) for t, r in zip(t_invs, resid)]
    outs = [_dot(jnp.concatenate([qd, qk], axis=1), jnp.concatenate([sb, vn], axis=0))
            for qd, sb, qk, vn in zip(q_decs, s_bs, qks, v_nbs)]
    for (b, hh), s_old, gl, kdt, vn in zip(probs, s_olds, g_lasts, k_dec_ts, v_nbs):
        state[b, hh] = s_old * jnp.exp(gl) + _dot(kdt, vn)
    for (b, hh), o, zg in zip(probs, outs, z_gates):
        o = o * _rms_scale(o, dv) * normw_ref[...] * zg
        o_ref[b, :, hh * dv:(hh + 1) * dv] = o.astype(o_ref.dtype)


def _gdn(h2, w_proj, conv_w, a_log, dt_bias, norm_w):
    bsz, s, d = h2.shape
    chunk = min(GDN_CHUNK, s)
    n_chunks = s // chunk
    nh, dv = GDN_HEADS, GDN_DV
    cw = conv_w.shape[1]
    full = lambda j: (0, 0)
    return pl.pallas_call(
        functools.partial(_gdn_kernel, chunk=chunk),
        grid=(n_chunks,),
        in_specs=[pl.BlockSpec((bsz, chunk, d), lambda j: (0, 0, 0)),
                  pl.BlockSpec((bsz, chunk, d), lambda j: (0, jnp.minimum(j + 1, n_chunks - 1), 0)),
                  pl.BlockSpec(w_proj.shape, full),
                  pl.BlockSpec(conv_w.shape, full),
                  pl.BlockSpec((1, LANES), full),
                  pl.BlockSpec((1, LANES), full),
                  pl.BlockSpec((1, dv), full)],
        out_specs=pl.BlockSpec((bsz, chunk, nh * dv), lambda j: (0, j, 0)),
        out_shape=jax.ShapeDtypeStruct((bsz, s, nh * dv), BF16),
        scratch_shapes=[pltpu.VMEM((bsz * chunk, w_proj.shape[1]), F32),
                        pltpu.VMEM((bsz, chunk + SUBLANES, cw), F32),
                        pltpu.VMEM((bsz, nh, GDN_DK, dv), F32)],
        compiler_params=pltpu.CompilerParams(
            dimension_semantics=("arbitrary",),
            vmem_limit_bytes=VMEM_LIMIT),
        name="gdn",
    )(h2, h2, w_proj, conv_w, a_log, dt_bias, norm_w)


def _mla_prep_kernel(h_ref, pos_ref, wm_ref, wuq_ref, wukv_ref, qnw_ref, kvnw_ref,
                     wqn_ref, wqr_ref, wqrs_ref, wkn_ref, wkr_ref,
                     invf_ref, sel_ref, qexte_ref, qexto_ref, kexte_ref, kexto_ref,
                     qt_ref, k_ref, vt_ref, *, n_sub):
    nh = MLA_HEADS
    scale = (MLA_NOPE + MLA_ROPE) ** -0.5 * LOG2_E
    o1 = MLA_Q_LORA
    o2 = o1 + MLA_KV_LORA
    pw = 2 * MLA_NOPE + 2 * LANES
    kw = MLA_NOPE + MLA_V
    tm = h_ref.shape[1]
    ts = tm // n_sub
    subs = [slice(r * ts, (r + 1) * ts) for r in range(n_sub)]
    lo = lax.broadcasted_iota(jnp.int32, (1, LANES), 1) < MLA_ROPE

    ang_t = invf_ref[...] * pos_ref[0].astype(F32)
    trig_t = _split3(jnp.concatenate([jnp.cos(ang_t), jnp.sin(ang_t)], axis=0))

    def half_sums(x):
        sq = x * x
        inv_lo = lax.rsqrt(jnp.sum(jnp.where(lo, sq, 0.0), axis=-1, keepdims=True)
                           * (1.0 / MLA_ROPE) + EPS)
        inv_hi = lax.rsqrt(jnp.sum(jnp.where(lo, 0.0, sq), axis=-1, keepdims=True)
                           * (1.0 / MLA_ROPE) + EPS)
        return jnp.where(lo, inv_lo, inv_hi)

    def latent(rows):
        return _dot(h_ref[0, rows], wm_ref[...])

    def up_project(lat):
        cq = lat[:, :o1]
        ckv = lat[:, o1:o2]
        cqn = (cq * _rms_scale(cq, MLA_Q_LORA) * qnw_ref[...]).astype(BF16)
        ckvn = (ckv * _rms_scale(ckv, MLA_KV_LORA) * kvnw_ref[...]).astype(BF16)
        return _dot(cqn, wuq_ref[...]), _dot(ckvn, wukv_ref[...])

    def finish(rows, lat, qf, kvf):
        parts = [_dot_tn(part[:, rows], sel_ref[...]) for part in trig_t]
        trig = (parts[0] + parts[1]) + parts[2]
        trig_r = pltpu.roll(trig, MLA_ROPE, axis=1)
        cos4 = jnp.where(lo, trig, trig_r)
        sin4 = jnp.where(lo, trig_r, trig)

        kr2 = lat[:, o2:o2 + LANES]
        a = kr2 * wkr_ref[...] * trig
        k_rope = (a + pltpu.roll(a, MLA_ROPE, axis=1)) * lax.rsqrt(
            jnp.sum(jnp.where(lo, kr2 * kr2, 0.0), axis=-1, keepdims=True) * (1.0 / MLA_ROPE) + EPS)
        k_rope_even = jnp.where(lo, k_rope, 0.0) + kexte_ref[...]
        k_rope_odd = jnp.where(lo, 0.0, k_rope) + kexto_ref[...]
        for pair in range(nh // 2):
            base = pair * pw
            qr2 = qf[:, base + 2 * MLA_NOPE:base + 2 * MLA_NOPE + LANES]
            qrs2 = qf[:, base + 2 * MLA_NOPE + LANES:base + pw]
            q_rope2 = ((qr2 * wqr_ref[...] * cos4 + qrs2 * wqrs_ref[...] * sin4)
                       * (half_sums(qr2) * scale))
            for odd in range(2):
                hh = 2 * pair + odd
                qn = qf[:, base + odd * MLA_NOPE:base + (odd + 1) * MLA_NOPE]
                qn = qn * _rms_scale(qn, MLA_NOPE) * (wqn_ref[...] * scale)
                if odd:
                    q_rope = jnp.where(lo, 0.0, q_rope2) + qexto_ref[...]
                else:
                    q_rope = jnp.where(lo, q_rope2, 0.0) + qexte_ref[...]
                qt_ref[0, hh, :, rows] = jnp.concatenate([qn, q_rope], axis=1).T.astype(BF16)
                kn = kvf[:, hh * kw:hh * kw + MLA_NOPE]
                v = kvf[:, hh * kw + MLA_NOPE:(hh + 1) * kw]
                kn = kn * _rms_scale(kn, MLA_NOPE) * wkn_ref[...]
                k_rope_h = k_rope_odd if odd else k_rope_even
                k_ref[0, hh, rows, :] = jnp.concatenate([kn, k_rope_h], axis=1).astype(BF16)
                vt_ref[0, hh, :, rows] = v.T.astype(BF16)

    lats = [latent(subs[0])]
    ups = []
    for r in range(n_sub):
        if r + 1 < n_sub:
            lats.append(latent(subs[r + 1]))
        ups.append(up_project(lats[r]))
        if r >= 1:
            finish(subs[r - 1], lats[r - 1], *ups[r - 1])
    finish(subs[n_sub - 1], lats[n_sub - 1], *ups[n_sub - 1])


def _mla_prep(h2, pos3, w_m, w_uq, w_ukv, vecs, *, tm=1024):
    bsz, s, d = h2.shape
    tm = min(tm, s)
    nh = MLA_HEADS
    qk_dim = MLA_NOPE + LANES
    full = lambda b, i: (0, 0)
    vec_specs = [pl.BlockSpec(v.shape, full) for v in vecs]
    assert tm % (MLA_SUBTILES * LANES) == 0
    return pl.pallas_call(
        functools.partial(_mla_prep_kernel, n_sub=MLA_SUBTILES),
        grid=(bsz, s // tm),
        in_specs=[pl.BlockSpec((1, tm, d), lambda b, i: (b, i, 0)),
                  pl.BlockSpec((1, 1, tm), lambda b, i: (b, 0, i)),
                  pl.BlockSpec(w_m.shape, full),
                  pl.BlockSpec(w_uq.shape, full),
                  pl.BlockSpec(w_ukv.shape, full)] + vec_specs,
        out_specs=[pl.BlockSpec((1, nh, qk_dim, tm), lambda b, i: (b, 0, 0, i)),
                   pl.BlockSpec((1, nh, tm, qk_dim), lambda b, i: (b, 0, i, 0)),
                   pl.BlockSpec((1, nh, MLA_V, tm), lambda b, i: (b, 0, 0, i))],
        out_shape=[jax.ShapeDtypeStruct((bsz, nh, qk_dim, s), BF16),
                   jax.ShapeDtypeStruct((bsz, nh, s, qk_dim), BF16),
                   jax.ShapeDtypeStruct((bsz, nh, MLA_V, s), BF16)],
        compiler_params=pltpu.CompilerParams(
            dimension_semantics=("parallel", "parallel"),
            vmem_limit_bytes=VMEM_LIMIT),
        name="mla_prep",
    )(h2, pos3, w_m, w_uq, w_ukv, *vecs)


def _attn_kernel(mode_ref, qt_ref, k_ref, vt_ref, nw_ref, o_ref, s_scr, m_scr, l_scr, acc_scr, *,
                 tq, tk):
    i = pl.program_id(2)
    per_q = tq // tk
    kc = lax.broadcasted_iota(jnp.int32, (tk, tq), 0) // ATTN_CHUNK
    qc = lax.broadcasted_iota(jnp.int32, (tk, tq), 1) // ATTN_CHUNK
    visible = qc >= kc

    def key_tile(hh, j):
        start = pl.multiple_of(j * tk, tk)
        return k_ref[0, hh, pl.ds(start, tk), :], vt_ref[0, hh, :, pl.ds(start, tk)]

    def finish():
        for hh in range(2):
            o = (acc_scr[hh] / l_scr[hh]).T
            o = o * _rms_scale(o, MLA_V) * nw_ref[...]
            o_ref[0, :, hh * MLA_V:(hh + 1) * MLA_V] = o.astype(o_ref.dtype)

    l_scr[...] = jnp.zeros(l_scr.shape, F32)
    acc_scr[...] = jnp.zeros(acc_scr.shape, F32)

    @pl.when(mode_ref[0] == 1)
    def _fixed_reference():
        def tile(hh, j, q0=0, mask=None):
            kj, vj = key_tile(hh, j)
            st = _dot(kj, qt_ref[0, hh, :, q0:])
            if mask is not None:
                st = jnp.where(mask, st, -jnp.inf)
            p = jnp.exp2(st)
            l_scr[hh, :, q0:] += jnp.sum(p, axis=0, keepdims=True)
            acc_scr[hh, :, q0:] += _dot(vj, p.astype(BF16))

        def body(jj, _):
            for r in range(per_q):
                tile(0, jj * per_q + r)
                tile(1, jj * per_q + r)
            return 0

        lax.fori_loop(0, i, body, 0)
        for r in range(per_q):
            for hh in range(2):
                tile(hh, i * per_q + r, r * tk, visible[:, :tq - r * tk])
        finish()

    @pl.when(mode_ref[0] == 0)
    def _online():
        def scores(hh, j, q0=0):
            kj, _ = key_tile(hh, j)
            s_scr[hh, :, :tq - q0] = _dot(kj, qt_ref[0, hh, :, q0:])

        def accumulate(hh, j, q0=0, mask=None):
            w = tq - q0
            if mask is not None:
                s_scr[hh, :, :w] = jnp.where(mask, s_scr[hh, :, :w], -jnp.inf)
            m = m_scr[hh, :, q0:]
            m_new = jnp.maximum(m, jnp.max(s_scr[hh, :, :w], axis=0, keepdims=True))
            alpha = jnp.exp2(m - m_new)
            p = jnp.exp2(s_scr[hh, :, :w] - m_new)
            m_scr[hh, :, q0:] = m_new
            l_scr[hh, :, q0:] = alpha * l_scr[hh, :, q0:] + jnp.sum(p, axis=0, keepdims=True)
            _, vj = key_tile(hh, j)
            acc_scr[hh, :, q0:] = alpha * acc_scr[hh, :, q0:] + _dot(vj, p.astype(BF16))

        m_scr[...] = jnp.full(m_scr.shape, -jnp.inf, F32)
        scores(0, 0)

        def body(jj, _):
            for r in range(per_q):
                j = jj * per_q + r
                scores(1, j)
                accumulate(0, j)
                scores(0, j + 1)
                accumulate(1, j)
            return 0

        lax.fori_loop(0, i, body, 0)
        for r in range(per_q):
            q0 = r * tk
            j = i * per_q + r
            scores(1, j, q0)
            accumulate(0, j, q0, visible[:, :tq - q0])
            if r < per_q - 1:
                scores(0, j + 1, q0 + tk)
            accumulate(1, j, q0, visible[:, :tq - q0])
        finish()


def _attn(mode, qt, k, vt, out_norm_w, *, tq=2048, tk=512):
    bsz, nh, qk_dim, s = qt.shape
    tq = min(tq, s)
    tk = min(tk, tq)
    heads = 2
    return pl.pallas_call(
        functools.partial(_attn_kernel, tq=tq, tk=tk),
        grid=(bsz, nh // heads, s // tq),
        in_specs=[pl.BlockSpec(memory_space=pltpu.SMEM),
                  pl.BlockSpec((1, heads, qk_dim, tq), lambda b, h, i: (b, h, 0, i)),
                  pl.BlockSpec((1, heads, s, qk_dim), lambda b, h, i: (b, h, 0, 0)),
                  pl.BlockSpec((1, heads, MLA_V, s), lambda b, h, i: (b, h, 0, 0)),
                  pl.BlockSpec((1, MLA_V), lambda b, h, i: (0, 0))],
        out_specs=pl.BlockSpec((1, tq, heads * MLA_V), lambda b, h, i: (b, i, h)),
        out_shape=jax.ShapeDtypeStruct((bsz, s, nh * MLA_V), BF16),
        scratch_shapes=[pltpu.VMEM((heads, tk, tq), F32),
                        pltpu.VMEM((heads, 1, tq), F32),
                        pltpu.VMEM((heads, 1, tq), F32),
                        pltpu.VMEM((heads, MLA_V, tq), F32)],
        compiler_params=pltpu.CompilerParams(
            dimension_semantics=("parallel", "parallel", "arbitrary"),
            vmem_limit_bytes=VMEM_LIMIT),
        name="attn",
    )(mode, qt, k, vt, out_norm_w)


def _pad_lanes(w, width=LANES):
    return jnp.pad(w, [(0, 0)] * (w.ndim - 1) + [(0, width - w.shape[-1])])


def _swap_halves(w):
    half = w.shape[-1] // 2
    return jnp.concatenate([w[..., half:], w[..., :half]], axis=-1)


def _layer(x, mods, pos3, w_ffn1_in, w_ffn1_out, w_in, conv_w, a_log, dt_bias, gdn_norm_w,
           q_norm_w, w_uq, kv_norm_w, w_ukv, qn_q_nope, qn_q_rope, qn_k_nope, qn_k_rope,
           out_norm_w, w_out, w_ffn2_in, w_ffn2_out):
    sh1, s1, g1, sh2, s2, g2, sh3, s3, g3 = mods
    nh = MLA_HEADS

    w_gdn, w_m = _split_w_in(w_in)
    per_q = MLA_NOPE + MLA_ROPE
    uq_parts = []
    for pair in range(nh // 2):
        wn, wr = [], []
        for hh in (2 * pair, 2 * pair + 1):
            wn.append(w_uq[:, hh * per_q:hh * per_q + MLA_NOPE])
            wr.append(w_uq[:, hh * per_q + MLA_NOPE:(hh + 1) * per_q])
        uq_parts += wn + wr + [_swap_halves(w) for w in wr]
    w_uq_p = jnp.concatenate(uq_parts, axis=1).astype(BF16)
    w_ukv_b = w_ukv.astype(BF16)

    half = MLA_ROPE // 2
    inv_freq = ROPE_BASE ** (-jnp.arange(half, dtype=F32) / half)
    invf = inv_freq.reshape(half, 1)
    eye = jnp.eye(half, dtype=F32)
    zero = jnp.zeros((half, 2 * half), F32)
    trig_sel = jnp.concatenate([jnp.concatenate([eye, eye, zero], axis=1),
                                jnp.concatenate([zero, -eye, eye], axis=1)], axis=0).astype(BF16)
    row = lambda v: v.reshape(1, -1)

    def sq_norm_bound(w_nope, w_rope):
        return MLA_NOPE * jnp.max(w_nope * w_nope) + MLA_ROPE * jnp.max(w_rope * w_rope)
    q_scale = (MLA_NOPE + MLA_ROPE) ** -0.5 * LOG2_E
    bound = (SCORE_BOUND_MARGIN * q_scale
             * jnp.sqrt(sq_norm_bound(qn_q_nope, qn_q_rope) * sq_norm_bound(qn_k_nope, qn_k_rope)))
    use_bound = bound < MAX_FIXED_REFERENCE
    mode = use_bound.astype(jnp.int32).reshape(1)
    lane_id = jnp.arange(LANES)
    ext_even = (lane_id == MLA_ROPE).astype(F32).reshape(1, LANES)
    ext_odd = (lane_id == 0).astype(F32).reshape(1, LANES)
    shift = jnp.where(use_bound, -bound, 0.0)
    twice = lambda v: jnp.concatenate([v, v])
    vecs = [row(q_norm_w), row(kv_norm_w),
            row(qn_q_nope), row(twice(qn_q_rope)), row(twice(_swap_halves(qn_q_rope))),
            row(qn_k_nope), row(jnp.concatenate([qn_k_rope, _swap_halves(qn_k_rope)])),
            invf, trig_sel, ext_even * shift, ext_odd * shift, ext_even, ext_odd]

    w1_in, w1_out, w2_in, w2_out = _to_bf16([w_ffn1_in, w_ffn1_out, w_ffn2_in, w_ffn2_out])
    x1, h2 = _ffn(x, sh1, s1, g1, w1_in, w1_out, nxt=(sh2, s2))

    o_a = _gdn(h2, w_gdn, conv_w, row(_pad_lanes(a_log)), row(_pad_lanes(dt_bias)), row(gdn_norm_w))
    qt, k, vt = _mla_prep(h2, pos3, w_m, w_uq_p, w_ukv_b, vecs)
    o_b = _attn(mode, qt, k, vt, row(out_norm_w))

    return _ffn(x1, sh3, s3, g3, w2_in, w2_out, mix=(o_a, o_b, w_out.astype(BF16), g2))


def kernel(x, c, positions, w_ada, b_ada, ffn1_w_in, ffn1_w_out, w_in, gdn_conv_w, gdn_a_log, gdn_dt_bias, gdn_norm_w, mla_q_norm_w, mla_w_uq, mla_kv_norm_w, mla_w_ukv, qkn_q_nope, qkn_q_rope, qkn_k_nope, qkn_k_rope, mla_out_norm_w, w_out, ffn2_w_in, ffn2_w_out):
    bsz, s, d = x.shape
    pos3 = positions.reshape(bsz, 1, s)
    for l in range(w_ada.shape[0]):
        mod = _mod(c, w_ada[l], b_ada[l])
        mods = [m.reshape(bsz, 1, d) for m in jnp.split(mod, N_MOD, axis=-1)]
        x = _layer(x, mods, pos3, ffn1_w_in[l], ffn1_w_out[l], w_in[l], gdn_conv_w[l],
                   gdn_a_log[l], gdn_dt_bias[l], gdn_norm_w[l], mla_q_norm_w[l], mla_w_uq[l],
                   mla_kv_norm_w[l], mla_w_ukv[l], qkn_q_nope[l], qkn_q_rope[l], qkn_k_nope[l],
                   qkn_k_rope[l], mla_out_norm_w[l], w_out[l], ffn2_w_in[l], ffn2_w_out[l])
    return x
```

```python
import functools

import jax
import jax.numpy as jnp
from jax import lax
from jax.experimental import pallas as pl
from jax.experimental.pallas import tpu as pltpu

F32 = jnp.float32
BF16 = jnp.bfloat16

EPS = 1e-6
ATTN_CHUNK = 64
GDN_HEADS = 4
GDN_DK = 128
GDN_DV = 128
CONV_K = 4
MLA_HEADS = 4
MLA_NOPE = 128
MLA_ROPE = 64
MLA_V = 128
MLA_Q_LORA = 384
MLA_KV_LORA = 256
ROPE_BASE = 10000.0
N_MOD = 9
LOG2_E = 1.4426950408889634
MAX_FIXED_REFERENCE = 60.0
SCORE_BOUND_MARGIN = 1.0625

LANES = 128
SUBLANES = 8
VMEM_LIMIT = 56 * 1024 * 1024

GDN_CHUNK = 128
GDN_PROJ_BLOCK = 256
GDN_CONV_PIECE = 512
FFN_SUBTILES = 4
MLA_SUBTILES = 4


def _dot(a, b):
    return jnp.dot(a, b, preferred_element_type=F32)


def _dot_nt(a, b):
    return lax.dot_general(a, b, (((1,), (1,)), ((), ())), preferred_element_type=F32)


def _dot_tn(a, b):
    return lax.dot_general(a, b, (((0,), (0,)), ((), ())), preferred_element_type=F32)


def _silu(x):
    return x * jax.nn.sigmoid(x)


def _rms_scale(x, n):
    return lax.rsqrt(jnp.sum(x * x, axis=-1, keepdims=True) * (1.0 / n) + EPS)


def _cast_kernel(*refs):
    n = len(refs) // 2
    for x_ref, o_ref in zip(refs[:n], refs[n:]):
        o_ref[...] = x_ref[...].astype(o_ref.dtype)


def _to_bf16(weights, *, row_blocks=8):
    specs = []
    for w in weights:
        rows, cols = w.shape
        tr = rows // row_blocks
        assert rows % row_blocks == 0 and tr % (2 * SUBLANES) == 0 and cols % LANES == 0
        specs.append(pl.BlockSpec((tr, cols), lambda i: (i, 0)))
    return pl.pallas_call(
        _cast_kernel,
        grid=(row_blocks,),
        in_specs=specs,
        out_specs=specs,
        out_shape=[jax.ShapeDtypeStruct(w.shape, BF16) for w in weights],
        compiler_params=pltpu.CompilerParams(dimension_semantics=("arbitrary",),
                                             vmem_limit_bytes=VMEM_LIMIT),
        name="cast",
    )(*weights)


def _split_w_in_kernel(wt_ref, gdn_ref, m_ref):
    o_gz = 2 * GDN_HEADS * GDN_DK + 2 * GDN_HEADS * GDN_DV
    o_ab = o_gz + 2 * GDN_HEADS
    o_ckv = o_ab + MLA_Q_LORA + MLA_KV_LORA
    wt = wt_ref[...]
    gdn_ref[...] = wt[:o_gz + LANES, :].T.astype(BF16)
    kr = wt[o_ckv:, :]
    half = MLA_ROPE // 2
    m_ref[...] = jnp.concatenate([wt[o_ab:o_ckv, :], kr, kr[half:, :], kr[:half, :]],
                                 axis=0).T.astype(BF16)


def _split_w_in(w_in_t, *, col_blocks=4):
    feats, d = w_in_t.shape
    tc = d // col_blocks
    n_gdn = 2 * GDN_HEADS * GDN_DK + 2 * GDN_HEADS * GDN_DV + LANES
    n_m = MLA_Q_LORA + MLA_KV_LORA + 2 * MLA_ROPE
    assert d % col_blocks == 0 and tc % LANES == 0
    return pl.pallas_call(
        _split_w_in_kernel,
        grid=(col_blocks,),
        in_specs=[pl.BlockSpec((feats, tc), lambda i: (0, i))],
        out_specs=[pl.BlockSpec((tc, n_gdn), lambda i: (i, 0)),
                   pl.BlockSpec((tc, n_m), lambda i: (i, 0))],
        out_shape=[jax.ShapeDtypeStruct((d, n_gdn), BF16),
                   jax.ShapeDtypeStruct((d, n_m), BF16)],
        compiler_params=pltpu.CompilerParams(dimension_semantics=("arbitrary",),
                                             vmem_limit_bytes=VMEM_LIMIT),
        name="split_w_in",
    )(w_in_t)


def _mod_kernel(c_ref, w_ref, b_ref, o_ref):
    sc = _silu(c_ref[...])
    o_ref[...] = jnp.dot(sc, w_ref[...], precision=lax.Precision.HIGHEST,
                         preferred_element_type=F32) + b_ref[...]


def _mod(c, w_ada, b_ada):
    bsz, d = c.shape
    n = w_ada.shape[1]
    tn = 1024
    return pl.pallas_call(
        _mod_kernel,
        grid=(n // tn,),
        in_specs=[pl.BlockSpec((bsz, d), lambda j: (0, 0)),
                  pl.BlockSpec((d, tn), lambda j: (0, j)),
                  pl.BlockSpec((1, tn), lambda j: (0, j))],
        out_specs=pl.BlockSpec((bsz, tn), lambda j: (0, j)),
        out_shape=jax.ShapeDtypeStruct((bsz, n), F32),
        compiler_params=pltpu.CompilerParams(dimension_semantics=("arbitrary",),
                                             vmem_limit_bytes=VMEM_LIMIT),
        name="mod",
    )(c, w_ada, b_ada.reshape(1, n))


def _ffn_kernel(*refs, has_mix, emit_next, d_model, d_ff, tf, n_sub):
    it = iter(refs)
    x_ref = next(it)
    if has_mix:
        oa_ref, ob_ref, wmix_ref, gmix_ref = next(it), next(it), next(it), next(it)
    shift_ref, scale_ref, gate_ref = next(it), next(it), next(it)
    win_ref, wout_ref = next(it), next(it)
    if emit_next:
        nshift_ref, nscale_ref = next(it), next(it)
    out_ref = next(it)
    if emit_next:
        hn_ref = next(it)
    h_sc, a_sc = next(it), next(it)

    tm = x_ref.shape[1]
    ts = tm // n_sub
    subs = [slice(r * ts, (r + 1) * ts) for r in range(n_sub)]

    def prologue(rows):
        x = x_ref[0, rows]
        if has_mix:
            half = oa_ref.shape[-1]
            y = (_dot(oa_ref[0, rows], wmix_ref[:half, :])
                 + _dot(ob_ref[0, rows], wmix_ref[half:, :]))
            x = x + gmix_ref[0] * y
            out_ref[0, rows] = x
        h_sc[rows] = (x * _rms_scale(x, d_model) * (1.0 + scale_ref[0])
                      + shift_ref[0]).astype(BF16)

    def hidden(rows):
        h = h_sc[rows]
        for c in range(d_ff // tf):
            g = _dot(h, win_ref[:, c * tf:(c + 1) * tf])
            u = _dot(h, win_ref[:, d_ff + c * tf:d_ff + (c + 1) * tf])
            a_sc[rows, c * tf:(c + 1) * tf] = (_silu(g) * u).astype(BF16)

    def epilogue(rows):
        x = out_ref[0, rows] if has_mix else x_ref[0, rows]
        out = x + 0.5 * gate_ref[0] * _dot(a_sc[rows], wout_ref[...])
        out_ref[0, rows] = out
        if emit_next:
            hn = out * _rms_scale(out, d_model) * (1.0 + nscale_ref[0]) + nshift_ref[0]
            hn_ref[0, rows] = hn.astype(BF16)

    prologue(subs[0])
    for r in range(n_sub):
        if r + 1 < n_sub:
            prologue(subs[r + 1])
        hidden(subs[r])
        epilogue(subs[r])


def _resident(shape, index_map):
    return pl.BlockSpec(shape, index_map, pipeline_mode=pl.Buffered(1))


def _ffn(x, shift, scale, gate, w_in_bf, w_out_bf, *, mix=None, nxt=None, tm=1024, tf=256):
    bsz, s, d = x.shape
    ff = w_out_bf.shape[0]
    tm = min(tm, s)
    assert ff % tf == 0 and s % tm == 0 and tm % (FFN_SUBTILES * 2 * SUBLANES) == 0
    has_mix = mix is not None
    emit_next = nxt is not None

    row = lambda b, i: (b, i, 0)
    vec = lambda b, i: (b, 0, 0)
    const = lambda b, i: (0, 0)
    args = [x]
    in_specs = [pl.BlockSpec((1, tm, d), row)]
    if has_mix:
        o_a, o_b, w_mix, g_mix = mix
        args += [o_a, o_b, w_mix, g_mix]
        in_specs += [pl.BlockSpec((1, tm, o_a.shape[-1]), row),
                     pl.BlockSpec((1, tm, o_b.shape[-1]), row),
                     _resident(w_mix.shape, const),
                     pl.BlockSpec((1, 1, d), vec)]
    args += [shift, scale, gate, w_in_bf, w_out_bf]
    in_specs += [pl.BlockSpec((1, 1, d), vec)] * 3
    in_specs += [_resident(w_in_bf.shape, const), _resident(w_out_bf.shape, const)]
    if emit_next:
        args += list(nxt)
        in_specs += [pl.BlockSpec((1, 1, d), vec)] * 2
    out_shape = [jax.ShapeDtypeStruct((bsz, s, d), F32)]
    out_specs = [pl.BlockSpec((1, tm, d), row)]
    if emit_next:
        out_shape.append(jax.ShapeDtypeStruct((bsz, s, d), BF16))
        out_specs.append(pl.BlockSpec((1, tm, d), row))

    res = pl.pallas_call(
        functools.partial(_ffn_kernel, has_mix=has_mix, emit_next=emit_next, d_model=d,
                          d_ff=ff, tf=tf, n_sub=FFN_SUBTILES),
        grid=(bsz, s // tm),
        in_specs=in_specs,
        out_specs=out_specs,
        out_shape=out_shape,
        scratch_shapes=[pltpu.VMEM((tm, d), BF16), pltpu.VMEM((tm, ff), BF16)],
        compiler_params=pltpu.CompilerParams(
            dimension_semantics=("parallel", "parallel"),
            vmem_limit_bytes=VMEM_LIMIT),
        name="ffn_mix" if has_mix else "ffn",
    )(*args)
    return res if emit_next else res[0]


def _unit_lower_inverses(a_mats, ii, jj):
    n = a_mats[0].shape[0]
    eye = (ii == jj).astype(F32)

    def same_block(size):
        return (ii // size) == (jj // size)

    base = same_block(SUBLANES)
    d1 = [jnp.where(base, a, 0.0) for a in a_mats]
    d1b = [d.astype(BF16) for d in d1]
    d2 = [_dot(d, d) for d in d1b]
    d2b = [d.astype(BF16) for d in d2]
    d3 = [_dot(a, b) for a, b in zip(d1b, d2b)]
    d4b = [_dot(d, d).astype(BF16) for d in d2b]
    p1 = [eye - a + b - c for a, b, c in zip(d1, d2, d3)]
    ts = [p + _dot(p.astype(BF16), d) for p, d in zip(p1, d4b)]
    size = SUBLANES
    while size < n:
        below = same_block(2 * size) & jnp.logical_not(same_block(size))
        offs = [jnp.where(below, a, 0.0).astype(BF16) for a in a_mats]
        tbs = [t.astype(BF16) for t in ts]
        xs = [_dot(o, t).astype(BF16) for o, t in zip(offs, tbs)]
        ts = [t - _dot(tb, x) for t, tb, x in zip(ts, tbs, xs)]
        size *= 2
    return ts


def _split3(x):
    hi = x.astype(BF16)
    r1 = x - hi.astype(F32)
    mid = r1.astype(BF16)
    lo = (r1 - mid.astype(F32)).astype(BF16)
    return hi, mid, lo


def _gdn_kernel(h0_ref, hn_ref, wproj_ref, convw_ref, alog_ref, dtb_ref, normw_ref,
                o_ref, proj_scr, xbuf, state, *, chunk):
    nb, _, d = hn_ref.shape
    nh, dk, dv = GDN_HEADS, GDN_DK, GDN_DV
    qk_w = nh * dk
    conv_w = 2 * qk_w + nh * dv
    z_off = conv_w
    ab_off = conv_w + nh * dv
    pad = SUBLANES
    probs = [(b, hh) for b in range(nb) for hh in range(nh)]

    @pl.when(pl.program_id(0) == 0)
    def _():
        state[...] = jnp.zeros_like(state)
        xbuf[:, 0:pad, :] = jnp.zeros((nb, pad, conv_w), F32)
        proj_scr[...] = _dot(h0_ref[...].reshape(nb * chunk, d), wproj_ref[...])

    for b in range(nb):
        xbuf[b, pad:pad + chunk, :] = proj_scr[b * chunk:(b + 1) * chunk, :conv_w]
    ab = proj_scr[:, ab_off:ab_off + LANES]
    z_gates = [_silu(proj_scr[b * chunk:(b + 1) * chunk, z_off + hh * dv:z_off + (hh + 1) * dv])
               for b, hh in probs]
    h_next = hn_ref[...].reshape(nb * chunk, d)
    n_cols = proj_scr.shape[1]
    col_blocks = [(c0, min(c0 + GDN_PROJ_BLOCK, n_cols))
                  for c0 in range(0, n_cols, GDN_PROJ_BLOCK)]

    g_all = -jnp.exp(alog_ref[...]) * jax.nn.softplus(ab + dtb_ref[...])
    beta_all = jax.nn.sigmoid(ab)

    ii = lax.broadcasted_iota(jnp.int32, (chunk, chunk), 0)
    jj = lax.broadcasted_iota(jnp.int32, (chunk, chunk), 1)
    incl = ii >= jj
    strict = ii > jj
    tri = incl.astype(BF16)
    cw = convw_ref[...]

    n_pieces = nb * (conv_w // GDN_CONV_PIECE)
    qkv, gcum, gcum_t = [], [], []
    piece = 0
    for b in range(nb):
        rows = slice(b * chunk, (b + 1) * chunk)
        parts = []
        for p0 in range(0, conv_w, GDN_CONV_PIECE):
            for c0, c1 in col_blocks[piece::n_pieces]:
                proj_scr[:, c0:c1] = _dot(h_next, wproj_ref[:, c0:c1])
            piece += 1
            cols = slice(p0, p0 + GDN_CONV_PIECE)
            xe = xbuf[b, :, cols]
            conv = xe[pad:] * cw[CONV_K - 1:CONV_K, cols]
            for tap in range(1, CONV_K):
                conv = conv + (pltpu.roll(xe, tap, axis=0)[pad:]
                               * cw[CONV_K - 1 - tap:CONV_K - tap, cols])
            parts.append(_silu(conv))
        xbuf[b, 0:pad, :] = xbuf[b, chunk:chunk + pad, :]
        qkv.append(jnp.concatenate(parts, axis=1))
        g_hi, g_mid, g_lo = _split3(g_all[rows, :])
        gc = _dot(tri, g_hi) + _dot(tri, g_mid) + _dot(tri, g_lo)
        gcum.append(gc)
        gcum_t.append(gc.T)

    qs, ks, vs, kbs, g_cols, b_cols, decays = [], [], [], [], [], [], []
    for b, hh in probs:
        q = qkv[b][:, hh * dk:(hh + 1) * dk]
        k = qkv[b][:, qk_w + hh * dk:qk_w + (hh + 1) * dk]
        vs.append(qkv[b][:, 2 * qk_w + hh * dv:2 * qk_w + (hh + 1) * dv])
        q = q * (lax.rsqrt(jnp.sum(q * q, axis=-1, keepdims=True) + EPS) * (dk ** -0.5))
        k = k * lax.rsqrt(jnp.sum(k * k, axis=-1, keepdims=True) + EPS)
        qs.append(q)
        ks.append(k)
        kbs.append(k.astype(BF16))
        g_col = gcum[b][:, hh:hh + 1]
        g_row = gcum_t[b][hh:hh + 1, :]
        g_cols.append(g_col)
        b_cols.append(beta_all[b * chunk:(b + 1) * chunk, nh + hh:nh + hh + 1])
        decays.append(jnp.exp(jnp.where(incl, g_col - g_row, -jnp.inf)))

    kks = [_dot_nt(kb, kb) for kb in kbs]
    qks = [(_dot_nt(q.astype(BF16), kb) * dec).astype(BF16) for q, kb, dec in zip(qs, kbs, decays)]
    a_mats = [jnp.where(strict, bc * kk * dec, 0.0) for bc, kk, dec in zip(b_cols, kks, decays)]
    t_invs = _unit_lower_inverses(a_mats, ii, jj)

    e_gs = [jnp.exp(g) for g in g_cols]
    rhs_us = [v * bc for v, bc in zip(vs, b_cols)]
    rhs_ws = [(k * (bc * eg)).astype(BF16) for k, bc, eg in zip(ks, b_cols, e_gs)]
    g_lasts = [g[chunk - 1:chunk, :] for g in g_cols]
    q_decs = [(q * eg).astype(BF16) for q, eg in zip(qs, e_gs)]
    k_dec_ts = [(k * jnp.exp(gl - g)).T.astype(BF16) for k, gl, g in zip(ks, g_lasts, g_cols)]

    s_olds = [state[b, hh] for b, hh in probs]
    s_bs = [s.astype(BF16) for s in s_olds]
    resid = [(ru - _dot(rw, sb)).astype(BF16) for ru, rw, sb in zip(rhs_us, rhs_ws, s_bs)]
    v_nbs = [_dot(t.astype(BF16), r).astype(BF16) for t, r in zip(t_invs, resid)]
    outs = [_dot(jnp.concatenate([qd, qk], axis=1), jnp.concatenate([sb, vn], axis=0))
            for qd, sb, qk, vn in zip(q_decs, s_bs, qks, v_nbs)]
    for (b, hh), s_old, gl, kdt, vn in zip(probs, s_olds, g_lasts, k_dec_ts, v_nbs):
        state[b, hh] = s_old * jnp.exp(gl) + _dot(kdt, vn)
    for (b, hh), o, zg in zip(probs, outs, z_gates):
        o = o * _rms_scale(o, dv) * normw_ref[...] * zg
        o_ref[b, :, hh * dv:(hh + 1) * dv] = o.astype(o_ref.dtype)


def _gdn(h2, w_proj, conv_w, a_log, dt_bias, norm_w):
    bsz, s, d = h2.shape
    chunk = min(GDN_CHUNK, s)
    n_chunks = s // chunk
    nh, dv = GDN_HEADS, GDN_DV
    cw = conv_w.shape[1]
    full = lambda j: (0, 0)
    return pl.pallas_call(
        functools.partial(_gdn_kernel, chunk=chunk),
        grid=(n_chunks,),
        in_specs=[pl.BlockSpec((bsz, chunk, d), lambda j: (0, 0, 0)),
                  pl.BlockSpec((bsz, chunk, d), lambda j: (0, jnp.minimum(j + 1, n_chunks - 1), 0)),
                  pl.BlockSpec(w_proj.shape, full),
                  pl.BlockSpec(conv_w.shape, full),
                  pl.BlockSpec((1, LANES), full),
                  pl.BlockSpec((1, LANES), full),
                  pl.BlockSpec((1, dv), full)],
        out_specs=pl.BlockSpec((bsz, chunk, nh * dv), lambda j: (0, j, 0)),
        out_shape=jax.ShapeDtypeStruct((bsz, s, nh * dv), BF16),
        scratch_shapes=[pltpu.VMEM((bsz * chunk, w_proj.shape[1]), F32),
                        pltpu.VMEM((bsz, chunk + SUBLANES, cw), F32),
                        pltpu.VMEM((bsz, nh, GDN_DK, dv), F32)],
        compiler_params=pltpu.CompilerParams(
            dimension_semantics=("arbitrary",),
            vmem_limit_bytes=VMEM_LIMIT),
        name="gdn",
    )(h2, h2, w_proj, conv_w, a_log, dt_bias, norm_w)


def _mla_prep_kernel(h_ref, pos_ref, wm_ref, wuq_ref, wukv_ref, qnw_ref, kvnw_ref,
                     wqn_ref, wqr_ref, wqrs_ref, wkn_ref, wkr_ref,
                     invf_ref, sel_ref, qexte_ref, qexto_ref, kexte_ref, kexto_ref,
                     qt_ref, k_ref, vt_ref, *, n_sub):
    nh = MLA_HEADS
    scale = (MLA_NOPE + MLA_ROPE) ** -0.5 * LOG2_E
    o1 = MLA_Q_LORA
    o2 = o1 + MLA_KV_LORA
    pw = 2 * MLA_NOPE + 2 * LANES
    kw = MLA_NOPE + MLA_V
    tm = h_ref.shape[1]
    ts = tm // n_sub
    subs = [slice(r * ts, (r + 1) * ts) for r in range(n_sub)]
    lo = lax.broadcasted_iota(jnp.int32, (1, LANES), 1) < MLA_ROPE

    ang_t = invf_ref[...] * pos_ref[0].astype(F32)
    trig_t = _split3(jnp.concatenate([jnp.cos(ang_t), jnp.sin(ang_t)], axis=0))

    def half_sums(x):
        sq = x * x
        inv_lo = lax.rsqrt(jnp.sum(jnp.where(lo, sq, 0.0), axis=-1, keepdims=True)
                           * (1.0 / MLA_ROPE) + EPS)
        inv_hi = lax.rsqrt(jnp.sum(jnp.where(lo, 0.0, sq), axis=-1, keepdims=True)
                           * (1.0 / MLA_ROPE) + EPS)
        return jnp.where(lo, inv_lo, inv_hi)

    def latent(rows):
        return _dot(h_ref[0, rows], wm_ref[...])

    def up_project(lat):
        cq = lat[:, :o1]
        ckv = lat[:, o1:o2]
        cqn = (cq * _rms_scale(cq, MLA_Q_LORA) * qnw_ref[...]).astype(BF16)
        ckvn = (ckv * _rms_scale(ckv, MLA_KV_LORA) * kvnw_ref[...]).astype(BF16)
        return _dot(cqn, wuq_ref[...]), _dot(ckvn, wukv_ref[...])

    def finish(rows, lat, qf, kvf):
        parts = [_dot_tn(part[:, rows], sel_ref[...]) for part in trig_t]
        trig = (parts[0] + parts[1]) + parts[2]
        trig_r = pltpu.roll(trig, MLA_ROPE, axis=1)
        cos4 = jnp.where(lo, trig, trig_r)
        sin4 = jnp.where(lo, trig_r, trig)

        kr2 = lat[:, o2:o2 + LANES]
        a = kr2 * wkr_ref[...] * trig
        k_rope = (a + pltpu.roll(a, MLA_ROPE, axis=1)) * lax.rsqrt(
            jnp.sum(jnp.where(lo, kr2 * kr2, 0.0), axis=-1, keepdims=True) * (1.0 / MLA_ROPE) + EPS)
        k_rope_even = jnp.where(lo, k_rope, 0.0) + kexte_ref[...]
        k_rope_odd = jnp.where(lo, 0.0, k_rope) + kexto_ref[...]
        for pair in range(nh // 2):
            base = pair * pw
            qr2 = qf[:, base + 2 * MLA_NOPE:base + 2 * MLA_NOPE + LANES]
            qrs2 = qf[:, base + 2 * MLA_NOPE + LANES:base + pw]
            q_rope2 = ((qr2 * wqr_ref[...] * cos4 + qrs2 * wqrs_ref[...] * sin4)
                       * (half_sums(qr2) * scale))
            for odd in range(2):
                hh = 2 * pair + odd
                qn = qf[:, base + odd * MLA_NOPE:base + (odd + 1) * MLA_NOPE]
                qn = qn * _rms_scale(qn, MLA_NOPE) * (wqn_ref[...] * scale)
                if odd:
                    q_rope = jnp.where(lo, 0.0, q_rope2) + qexto_ref[...]
                else:
                    q_rope = jnp.where(lo, q_rope2, 0.0) + qexte_ref[...]
                qt_ref[0, hh, :, rows] = jnp.concatenate([qn, q_rope], axis=1).T.astype(BF16)
                kn = kvf[:, hh * kw:hh * kw + MLA_NOPE]
                v = kvf[:, hh * kw + MLA_NOPE:(hh + 1) * kw]
                kn = kn * _rms_scale(kn, MLA_NOPE) * wkn_ref[...]
                k_rope_h = k_rope_odd if odd else k_rope_even
                k_ref[0, hh, rows, :] = jnp.concatenate([kn, k_rope_h], axis=1).astype(BF16)
                vt_ref[0, hh, :, rows] = v.T.astype(BF16)

    lats = [latent(subs[0])]
    ups = []
    for r in range(n_sub):
        if r + 1 < n_sub:
            lats.append(latent(subs[r + 1]))
        ups.append(up_project(lats[r]))
        if r >= 1:
            finish(subs[r - 1], lats[r - 1], *ups[r - 1])
    finish(subs[n_sub - 1], lats[n_sub - 1], *ups[n_sub - 1])


def _mla_prep(h2, pos3, w_m, w_uq, w_ukv, vecs, *, tm=1024):
    bsz, s, d = h2.shape
    tm = min(tm, s)
    nh = MLA_HEADS
    qk_dim = MLA_NOPE + LANES
    full = lambda b, i: (0, 0)
    vec_specs = [pl.BlockSpec(v.shape, full) for v in vecs]
    assert tm % (MLA_SUBTILES * LANES) == 0
    return pl.pallas_call(
        functools.partial(_mla_prep_kernel, n_sub=MLA_SUBTILES),
        grid=(bsz, s // tm),
        in_specs=[pl.BlockSpec((1, tm, d), lambda b, i: (b, i, 0)),
                  pl.BlockSpec((1, 1, tm), lambda b, i: (b, 0, i)),
                  pl.BlockSpec(w_m.shape, full),
                  pl.BlockSpec(w_uq.shape, full),
                  pl.BlockSpec(w_ukv.shape, full)] + vec_specs,
        out_specs=[pl.BlockSpec((1, nh, qk_dim, tm), lambda b, i: (b, 0, 0, i)),
                   pl.BlockSpec((1, nh, tm, qk_dim), lambda b, i: (b, 0, i, 0)),
                   pl.BlockSpec((1, nh, MLA_V, tm), lambda b, i: (b, 0, 0, i))],
        out_shape=[jax.ShapeDtypeStruct((bsz, nh, qk_dim, s), BF16),
                   jax.ShapeDtypeStruct((bsz, nh, s, qk_dim), BF16),
                   jax.ShapeDtypeStruct((bsz, nh, MLA_V, s), BF16)],
        compiler_params=pltpu.CompilerParams(
            dimension_semantics=("parallel", "parallel"),
            vmem_limit_bytes=VMEM_LIMIT),
        name="mla_prep",
    )(h2, pos3, w_m, w_uq, w_ukv, *vecs)


def _attn_kernel(mode_ref, qt_ref, k_ref, vt_ref, nw_ref, o_ref, s_scr, m_scr, l_scr, acc_scr, *,
                 tq, tk):
    i = pl.program_id(2)
    per_q = tq // tk
    kc = lax.broadcasted_iota(jnp.int32, (tk, tq), 0) // ATTN_CHUNK
    qc = lax.broadcasted_iota(jnp.int32, (tk, tq), 1) // ATTN_CHUNK
    visible = qc >= kc

    def key_tile(hh, j):
        start = pl.multiple_of(j * tk, tk)
        return k_ref[0, hh, pl.ds(start, tk), :], vt_ref[0, hh, :, pl.ds(start, tk)]

    def finish():
        for hh in range(2):
            o = (acc_scr[hh] / l_scr[hh]).T
            o = o * _rms_scale(o, MLA_V) * nw_ref[...]
            o_ref[0, :, hh * MLA_V:(hh + 1) * MLA_V] = o.astype(o_ref.dtype)

    l_scr[...] = jnp.zeros(l_scr.shape, F32)
    acc_scr[...] = jnp.zeros(acc_scr.shape, F32)

    @pl.when(mode_ref[0] == 1)
    def _fixed_reference():
        def tile(hh, j, q0=0, mask=None):
            kj, vj = key_tile(hh, j)
            st = _dot(kj, qt_ref[0, hh, :, q0:])
            if mask is not None:
                st = jnp.where(mask, st, -jnp.inf)
            p = jnp.exp2(st)
            l_scr[hh, :, q0:] += jnp.sum(p, axis=0, keepdims=True)
            acc_scr[hh, :, q0:] += _dot(vj, p.astype(BF16))

        def body(jj, _):
            for r in range(per_q):
                tile(0, jj * per_q + r)
                tile(1, jj * per_q + r)
            return 0

        lax.fori_loop(0, i, body, 0)
        for r in range(per_q):
            for hh in range(2):
                tile(hh, i * per_q + r, r * tk, visible[:, :tq - r * tk])
        finish()

    @pl.when(mode_ref[0] == 0)
    def _online():
        def scores(hh, j, q0=0):
            kj, _ = key_tile(hh, j)
            s_scr[hh, :, :tq - q0] = _dot(kj, qt_ref[0, hh, :, q0:])

        def accumulate(hh, j, q0=0, mask=None):
            w = tq - q0
            if mask is not None:
                s_scr[hh, :, :w] = jnp.where(mask, s_scr[hh, :, :w], -jnp.inf)
            m = m_scr[hh, :, q0:]
            m_new = jnp.maximum(m, jnp.max(s_scr[hh, :, :w], axis=0, keepdims=True))
            alpha = jnp.exp2(m - m_new)
            p = jnp.exp2(s_scr[hh, :, :w] - m_new)
            m_scr[hh, :, q0:] = m_new
            l_scr[hh, :, q0:] = alpha * l_scr[hh, :, q0:] + jnp.sum(p, axis=0, keepdims=True)
            _, vj = key_tile(hh, j)
            acc_scr[hh, :, q0:] = alpha * acc_scr[hh, :, q0:] + _dot(vj, p.astype(BF16))

        m_scr[...] = jnp.full(m_scr.shape, -jnp.inf, F32)
        scores(0, 0)

        def body(jj, _):
            for r in range(per_q):
                j = jj * per_q + r
                scores(1, j)
                accumulate(0, j)
                scores(0, j + 1)
                accumulate(1, j)
            return 0

        lax.fori_loop(0, i, body, 0)
        for r in range(per_q):
            q0 = r * tk
            j = i * per_q + r
            scores(1, j, q0)
            accumulate(0, j, q0, visible[:, :tq - q0])
            if r < per_q - 1:
                scores(0, j + 1, q0 + tk)
            accumulate(1, j, q0, visible[:, :tq - q0])
        finish()


def _attn(mode, qt, k, vt, out_norm_w, *, tq=2048, tk=512):
    bsz, nh, qk_dim, s = qt.shape
    tq = min(tq, s)
    tk = min(tk, tq)
    heads = 2
    return pl.pallas_call(
        functools.partial(_attn_kernel, tq=tq, tk=tk),
        grid=(bsz, nh // heads, s // tq),
        in_specs=[pl.BlockSpec(memory_space=pltpu.SMEM),
                  pl.BlockSpec((1, heads, qk_dim, tq), lambda b, h, i: (b, h, 0, i)),
                  pl.BlockSpec((1, heads, s, qk_dim), lambda b, h, i: (b, h, 0, 0)),
                  pl.BlockSpec((1, heads, MLA_V, s), lambda b, h, i: (b, h, 0, 0)),
                  pl.BlockSpec((1, MLA_V), lambda b, h, i: (0, 0))],
        out_specs=pl.BlockSpec((1, tq, heads * MLA_V), lambda b, h, i: (b, i, h)),
        out_shape=jax.ShapeDtypeStruct((bsz, s, nh * MLA_V), BF16),
        scratch_shapes=[pltpu.VMEM((heads, tk, tq), F32),
                        pltpu.VMEM((heads, 1, tq), F32),
                        pltpu.VMEM((heads, 1, tq), F32),
                        pltpu.VMEM((heads, MLA_V, tq), F32)],
        compiler_params=pltpu.CompilerParams(
            dimension_semantics=("parallel", "parallel", "arbitrary"),
            vmem_limit_bytes=VMEM_LIMIT),
        name="attn",
    )(mode, qt, k, vt, out_norm_w)


def _pad_lanes(w, width=LANES):
    return jnp.pad(w, [(0, 0)] * (w.ndim - 1) + [(0, width - w.shape[-1])])


def _swap_halves(w):
    half = w.shape[-1] // 2
    return jnp.concatenate([w[..., half:], w[..., :half]], axis=-1)


def _layer(x, mods, pos3, w_ffn1_in, w_ffn1_out, w_in, conv_w, a_log, dt_bias, gdn_norm_w,
           q_norm_w, w_uq, kv_norm_w, w_ukv, qn_q_nope, qn_q_rope, qn_k_nope, qn_k_rope,
           out_norm_w, w_out, w_ffn2_in, w_ffn2_out):
    sh1, s1, g1, sh2, s2, g2, sh3, s3, g3 = mods
    nh = MLA_HEADS

    w_gdn, w_m = _split_w_in(w_in.T)
    per_q = MLA_NOPE + MLA_ROPE
    uq_parts = []
    for pair in range(nh // 2):
        wn, wr = [], []
        for hh in (2 * pair, 2 * pair + 1):
            wn.append(w_uq[:, hh * per_q:hh * per_q + MLA_NOPE])
            wr.append(w_uq[:, hh * per_q + MLA_NOPE:(hh + 1) * per_q])
        uq_parts += wn + wr + [_swap_halves(w) for w in wr]
    w_uq_p = jnp.concatenate(uq_parts, axis=1).astype(BF16)
    w_ukv_b = w_ukv.astype(BF16)

    half = MLA_ROPE // 2
    inv_freq = ROPE_BASE ** (-jnp.arange(half, dtype=F32) / half)
    invf = inv_freq.reshape(half, 1)
    eye = jnp.eye(half, dtype=F32)
    zero = jnp.zeros((half, 2 * half), F32)
    trig_sel = jnp.concatenate([jnp.concatenate([eye, eye, zero], axis=1),
                                jnp.concatenate([zero, -eye, eye], axis=1)], axis=0).astype(BF16)
    row = lambda v: v.reshape(1, -1)

    def sq_norm_bound(w_nope, w_rope):
        return MLA_NOPE * jnp.max(w_nope * w_nope) + MLA_ROPE * jnp.max(w_rope * w_rope)
    q_scale = (MLA_NOPE + MLA_ROPE) ** -0.5 * LOG2_E
    bound = (SCORE_BOUND_MARGIN * q_scale
             * jnp.sqrt(sq_norm_bound(qn_q_nope, qn_q_rope) * sq_norm_bound(qn_k_nope, qn_k_rope)))
    use_bound = bound < MAX_FIXED_REFERENCE
    mode = use_bound.astype(jnp.int32).reshape(1)
    lane_id = jnp.arange(LANES)
    ext_even = (lane_id == MLA_ROPE).astype(F32).reshape(1, LANES)
    ext_odd = (lane_id == 0).astype(F32).reshape(1, LANES)
    shift = jnp.where(use_bound, -bound, 0.0)
    twice = lambda v: jnp.concatenate([v, v])
    vecs = [row(q_norm_w), row(kv_norm_w),
            row(qn_q_nope), row(twice(qn_q_rope)), row(twice(_swap_halves(qn_q_rope))),
            row(qn_k_nope), row(jnp.concatenate([qn_k_rope, _swap_halves(qn_k_rope)])),
            invf, trig_sel, ext_even * shift, ext_odd * shift, ext_even, ext_odd]

    w1_in, w1_out, w2_in, w2_out = _to_bf16([w_ffn1_in, w_ffn1_out, w_ffn2_in, w_ffn2_out])
    x1, h2 = _ffn(x, sh1, s1, g1, w1_in, w1_out, nxt=(sh2, s2))

    o_a = _gdn(h2, w_gdn, conv_w, row(_pad_lanes(a_log)), row(_pad_lanes(dt_bias)), row(gdn_norm_w))
    qt, k, vt = _mla_prep(h2, pos3, w_m, w_uq_p, w_ukv_b, vecs)
    o_b = _attn(mode, qt, k, vt, row(out_norm_w))

    return _ffn(x1, sh3, s3, g3, w2_in, w2_out, mix=(o_a, o_b, w_out.astype(BF16), g2))


def kernel(x, c, positions, w_ada, b_ada, ffn1_w_in, ffn1_w_out, w_in, gdn_conv_w, gdn_a_log, gdn_dt_bias, gdn_norm_w, mla_q_norm_w, mla_w_uq, mla_kv_norm_w, mla_w_ukv, qkn_q_nope, qkn_q_rope, qkn_k_nope, qkn_k_rope, mla_out_norm_w, w_out, ffn2_w_in, ffn2_w_out):
    bsz, s, d = x.shape
    pos3 = positions.reshape(bsz, 1, s)
    for l in range(w_ada.shape[0]):
        mod = _mod(c, w_ada[l], b_ada[l])
        mods = [m.reshape(bsz, 1, d) for m in jnp.split(mod, N_MOD, axis=-1)]
        x = _layer(x, mods, pos3, ffn1_w_in[l], ffn1_w_out[l], w_in[l], gdn_conv_w[l],
                   gdn_a_log[l], gdn_dt_bias[l], gdn_norm_w[l], mla_q_norm_w[l], mla_w_uq[l],
                   mla_kv_norm_w[l], mla_w_ukv[l], qkn_q_nope[l], qkn_q_rope[l], qkn_k_nope[l],
                   qkn_k_rope[l], mla_out_norm_w[l], w_out[l], ffn2_w_in[l], ffn2_w_out[l])
    return x
```

```python
import functools

import jax
import jax.numpy as jnp
from jax import lax
from jax.experimental import pallas as pl
from jax.experimental.pallas import tpu as pltpu

F32 = jnp.float32
BF16 = jnp.bfloat16

EPS = 1e-6
ATTN_CHUNK = 64
GDN_HEADS = 4
GDN_DK = 128
GDN_DV = 128
CONV_K = 4
MLA_HEADS = 4
MLA_NOPE = 128
MLA_ROPE = 64
MLA_V = 128
MLA_Q_LORA = 384
MLA_KV_LORA = 256
ROPE_BASE = 10000.0
N_MOD = 9
LOG2_E = 1.4426950408889634
MAX_FIXED_REFERENCE = 60.0
SCORE_BOUND_MARGIN = 1.0625

LANES = 128
SUBLANES = 8
VMEM_LIMIT = 56 * 1024 * 1024

GDN_CHUNK = 128
GDN_PROJ_BLOCK = 256
GDN_CONV_PIECE = 512
FFN_SUBTILES = 4
MLA_SUBTILES = 4


def _dot(a, b):
    return jnp.dot(a, b, preferred_element_type=F32)


def _dot_nt(a, b):
    return lax.dot_general(a, b, (((1,), (1,)), ((), ())), preferred_element_type=F32)


def _dot_tn(a, b):
    return lax.dot_general(a, b, (((0,), (0,)), ((), ())), preferred_element_type=F32)


def _silu(x):
    return x * jax.nn.sigmoid(x)


def _rms_scale(x, n):
    return lax.rsqrt(jnp.sum(x * x, axis=-1, keepdims=True) * (1.0 / n) + EPS)


def _cast_kernel(*refs):
    n = len(refs) // 2
    for x_ref, o_ref in zip(refs[:n], refs[n:]):
        o_ref[...] = x_ref[...].astype(o_ref.dtype)


def _to_bf16(weights, *, row_blocks=16):
    specs = []
    for w in weights:
        rows, cols = w.shape
        tr = rows // row_blocks
        assert rows % row_blocks == 0 and tr % (2 * SUBLANES) == 0 and cols % LANES == 0
        specs.append(pl.BlockSpec((tr, cols), lambda i: (i, 0)))
    return pl.pallas_call(
        _cast_kernel,
        grid=(row_blocks,),
        in_specs=specs,
        out_specs=specs,
        out_shape=[jax.ShapeDtypeStruct(w.shape, BF16) for w in weights],
        compiler_params=pltpu.CompilerParams(dimension_semantics=("arbitrary",),
                                             vmem_limit_bytes=VMEM_LIMIT),
        name="cast",
    )(*weights)


def _split_w_in_kernel(wt_ref, gdn_ref, m_ref):
    o_gz = 2 * GDN_HEADS * GDN_DK + 2 * GDN_HEADS * GDN_DV
    o_ab = o_gz + 2 * GDN_HEADS
    o_ckv = o_ab + MLA_Q_LORA + MLA_KV_LORA
    wt = wt_ref[...]
    gdn_ref[...] = wt[:o_gz + LANES, :].T.astype(BF16)
    kr = wt[o_ckv:, :]
    half = MLA_ROPE // 2
    m_ref[...] = jnp.concatenate([wt[o_ab:o_ckv, :], kr, kr[half:, :], kr[:half, :]],
                                 axis=0).T.astype(BF16)


def _split_w_in(w_in_t, *, col_blocks=4):
    feats, d = w_in_t.shape
    tc = d // col_blocks
    n_gdn = 2 * GDN_HEADS * GDN_DK + 2 * GDN_HEADS * GDN_DV + LANES
    n_m = MLA_Q_LORA + MLA_KV_LORA + 2 * MLA_ROPE
    assert d % col_blocks == 0 and tc % LANES == 0
    return pl.pallas_call(
        _split_w_in_kernel,
        grid=(col_blocks,),
        in_specs=[pl.BlockSpec((feats, tc), lambda i: (0, i))],
        out_specs=[pl.BlockSpec((tc, n_gdn), lambda i: (i, 0)),
                   pl.BlockSpec((tc, n_m), lambda i: (i, 0))],
        out_shape=[jax.ShapeDtypeStruct((d, n_gdn), BF16),
                   jax.ShapeDtypeStruct((d, n_m), BF16)],
        compiler_params=pltpu.CompilerParams(dimension_semantics=("arbitrary",),
                                             vmem_limit_bytes=VMEM_LIMIT),
        name="split_w_in",
    )(w_in_t)


def _mod_kernel(c_ref, w_ref, b_ref, o_ref):
    sc = _silu(c_ref[...])
    o_ref[...] = jnp.dot(sc, w_ref[...], precision=lax.Precision.HIGHEST,
                         preferred_element_type=F32) + b_ref[...]


def _mod(c, w_ada, b_ada):
    bsz, d = c.shape
    n = w_ada.shape[1]
    tn = 1536
    assert n % tn == 0
    return pl.pallas_call(
        _mod_kernel,
        grid=(n // tn,),
        in_specs=[pl.BlockSpec((bsz, d), lambda j: (0, 0)),
                  pl.BlockSpec((d, tn), lambda j: (0, j)),
                  pl.BlockSpec((1, tn), lambda j: (0, j))],
        out_specs=pl.BlockSpec((bsz, tn), lambda j: (0, j)),
        out_shape=jax.ShapeDtypeStruct((bsz, n), F32),
        compiler_params=pltpu.CompilerParams(dimension_semantics=("arbitrary",),
                                             vmem_limit_bytes=VMEM_LIMIT),
        name="mod",
    )(c, w_ada, b_ada.reshape(1, n))


def _ffn_kernel(*refs, has_mix, emit_next, d_model, d_ff, tf, n_sub):
    it = iter(refs)
    x_ref = next(it)
    if has_mix:
        oa_ref, ob_ref, wmix_ref, gmix_ref = next(it), next(it), next(it), next(it)
    shift_ref, scale_ref, gate_ref = next(it), next(it), next(it)
    win_ref, wout_ref = next(it), next(it)
    if emit_next:
        nshift_ref, nscale_ref = next(it), next(it)
    out_ref = next(it)
    if emit_next:
        hn_ref = next(it)
    h_sc, a_sc = next(it), next(it)

    tm = x_ref.shape[1]
    ts = tm // n_sub
    subs = [slice(r * ts, (r + 1) * ts) for r in range(n_sub)]

    def prologue(rows):
        x = x_ref[0, rows]
        if has_mix:
            half = oa_ref.shape[-1]
            y = (_dot(oa_ref[0, rows], wmix_ref[:half, :])
                 + _dot(ob_ref[0, rows], wmix_ref[half:, :]))
            x = x + gmix_ref[0] * y
            out_ref[0, rows] = x
        h_sc[rows] = (x * _rms_scale(x, d_model) * (1.0 + scale_ref[0])
                      + shift_ref[0]).astype(BF16)

    def hidden(rows):
        h = h_sc[rows]
        for c in range(d_ff // tf):
            g = _dot(h, win_ref[:, c * tf:(c + 1) * tf])
            u = _dot(h, win_ref[:, d_ff + c * tf:d_ff + (c + 1) * tf])
            a_sc[rows, c * tf:(c + 1) * tf] = (_silu(g) * u).astype(BF16)

    def epilogue(rows):
        x = out_ref[0, rows] if has_mix else x_ref[0, rows]
        out = x + 0.5 * gate_ref[0] * _dot(a_sc[rows], wout_ref[...])
        out_ref[0, rows] = out
        if emit_next:
            hn = out * _rms_scale(out, d_model) * (1.0 + nscale_ref[0]) + nshift_ref[0]
            hn_ref[0, rows] = hn.astype(BF16)

    prologue(subs[0])
    for r in range(n_sub):
        if r + 1 < n_sub:
            prologue(subs[r + 1])
        hidden(subs[r])
        epilogue(subs[r])


def _resident(shape, index_map):
    return pl.BlockSpec(shape, index_map, pipeline_mode=pl.Buffered(1))


def _ffn(x, shift, scale, gate, w_in_bf, w_out_bf, *, mix=None, nxt=None, tm=1024, tf=256):
    bsz, s, d = x.shape
    ff = w_out_bf.shape[0]
    tm = min(tm, s)
    assert ff % tf == 0 and s % tm == 0 and tm % (FFN_SUBTILES * 2 * SUBLANES) == 0
    has_mix = mix is not None
    emit_next = nxt is not None

    row = lambda b, i: (b, i, 0)
    vec = lambda b, i: (b, 0, 0)
    const = lambda b, i: (0, 0)
    args = [x]
    in_specs = [pl.BlockSpec((1, tm, d), row)]
    if has_mix:
        o_a, o_b, w_mix, g_mix = mix
        args += [o_a, o_b, w_mix, g_mix]
        in_specs += [pl.BlockSpec((1, tm, o_a.shape[-1]), row),
                     pl.BlockSpec((1, tm, o_b.shape[-1]), row),
                     _resident(w_mix.shape, const),
                     pl.BlockSpec((1, 1, d), vec)]
    args += [shift, scale, gate, w_in_bf, w_out_bf]
    in_specs += [pl.BlockSpec((1, 1, d), vec)] * 3
    in_specs += [_resident(w_in_bf.shape, const), _resident(w_out_bf.shape, const)]
    if emit_next:
        args += list(nxt)
        in_specs += [pl.BlockSpec((1, 1, d), vec)] * 2
    out_shape = [jax.ShapeDtypeStruct((bsz, s, d), F32)]
    out_specs = [pl.BlockSpec((1, tm, d), row)]
    if emit_next:
        out_shape.append(jax.ShapeDtypeStruct((bsz, s, d), BF16))
        out_specs.append(pl.BlockSpec((1, tm, d), row))

    res = pl.pallas_call(
        functools.partial(_ffn_kernel, has_mix=has_mix, emit_next=emit_next, d_model=d,
                          d_ff=ff, tf=tf, n_sub=FFN_SUBTILES),
        grid=(bsz, s // tm),
        in_specs=in_specs,
        out_specs=out_specs,
        out_shape=out_shape,
        scratch_shapes=[pltpu.VMEM((tm, d), BF16), pltpu.VMEM((tm, ff), BF16)],
        compiler_params=pltpu.CompilerParams(
            dimension_semantics=("parallel", "parallel"),
            vmem_limit_bytes=VMEM_LIMIT),
        name="ffn_mix" if has_mix else "ffn",
    )(*args)
    return res if emit_next else res[0]


def _unit_lower_inverses(a_mats, ii, jj):
    n = a_mats[0].shape[0]
    eye = (ii == jj).astype(F32)

    def same_block(size):
        return (ii // size) == (jj // size)

    base = same_block(SUBLANES)
    d1 = [jnp.where(base, a, 0.0) for a in a_mats]
    d1b = [d.astype(BF16) for d in d1]
    d2 = [_dot(d, d) for d in d1b]
    d2b = [d.astype(BF16) for d in d2]
    d3 = [_dot(a, b) for a, b in zip(d1b, d2b)]
    d4b = [_dot(d, d).astype(BF16) for d in d2b]
    p1 = [eye - a + b - c for a, b, c in zip(d1, d2, d3)]
    ts = [p + _dot(p.astype(BF16), d) for p, d in zip(p1, d4b)]
    size = SUBLANES
    while size < n:
        below = same_block(2 * size) & jnp.logical_not(same_block(size))
        offs = [jnp.where(below, a, 0.0).astype(BF16) for a in a_mats]
        tbs = [t.astype(BF16) for t in ts]
        xs = [_dot(o, t).astype(BF16) for o, t in zip(offs, tbs)]
        ts = [t - _dot(tb, x) for t, tb, x in zip(ts, tbs, xs)]
        size *= 2
    return ts


def _split3(x):
    hi = x.astype(BF16)
    r1 = x - hi.astype(F32)
    mid = r1.astype(BF16)
    lo = (r1 - mid.astype(F32)).astype(BF16)
    return hi, mid, lo


def _gdn_kernel(h0_ref, hn_ref, wproj_ref, convw_ref, alog_ref, dtb_ref, normw_ref,
                o_ref, proj_scr, xbuf, state, *, chunk):
    nb, _, d = hn_ref.shape
    nh, dk, dv = GDN_HEADS, GDN_DK, GDN_DV
    qk_w = nh * dk
    conv_w = 2 * qk_w + nh * dv
    z_off = conv_w
    ab_off = conv_w + nh * dv
    pad = SUBLANES
    probs = [(b, hh) for b in range(nb) for hh in range(nh)]

    @pl.when(pl.program_id(0) == 0)
    def _():
        state[...] = jnp.zeros_like(state)
        xbuf[:, 0:pad, :] = jnp.zeros((nb, pad, conv_w), F32)
        proj_scr[...] = _dot(h0_ref[...].reshape(nb * chunk, d), wproj_ref[...])

    for b in range(nb):
        xbuf[b, pad:pad + chunk, :] = proj_scr[b * chunk:(b + 1) * chunk, :conv_w]
    ab = proj_scr[:, ab_off:ab_off + LANES]
    z_gates = [_silu(proj_scr[b * chunk:(b + 1) * chunk, z_off + hh * dv:z_off + (hh + 1) * dv])
               for b, hh in probs]
    h_next = hn_ref[...].reshape(nb * chunk, d)
    n_cols = proj_scr.shape[1]
    col_blocks = [(c0, min(c0 + GDN_PROJ_BLOCK, n_cols))
                  for c0 in range(0, n_cols, GDN_PROJ_BLOCK)]

    g_all = -jnp.exp(alog_ref[...]) * jax.nn.softplus(ab + dtb_ref[...])
    beta_all = jax.nn.sigmoid(ab)

    ii = lax.broadcasted_iota(jnp.int32, (chunk, chunk), 0)
    jj = lax.broadcasted_iota(jnp.int32, (chunk, chunk), 1)
    incl = ii >= jj
    strict = ii > jj
    tri = incl.astype(BF16)
    cw = convw_ref[...]

    n_pieces = nb * (conv_w // GDN_CONV_PIECE)
    qkv, gcum, gcum_t = [], [], []
    piece = 0
    for b in range(nb):
        rows = slice(b * chunk, (b + 1) * chunk)
        parts = []
        for p0 in range(0, conv_w, GDN_CONV_PIECE):
            for c0, c1 in col_blocks[piece::n_pieces]:
                proj_scr[:, c0:c1] = _dot(h_next, wproj_ref[:, c0:c1])
            piece += 1
            cols = slice(p0, p0 + GDN_CONV_PIECE)
            xe = xbuf[b, :, cols]
            conv = xe[pad:] * cw[CONV_K - 1:CONV_K, cols]
            for tap in range(1, CONV_K):
                conv = conv + (pltpu.roll(xe, tap, axis=0)[pad:]
                               * cw[CONV_K - 1 - tap:CONV_K - tap, cols])
            parts.append(_silu(conv))
        xbuf[b, 0:pad, :] = xbuf[b, chunk:chunk + pad, :]
        qkv.append(jnp.concatenate(parts, axis=1))
        g_hi, g_mid, g_lo = _split3(g_all[rows, :])
        gc = _dot(tri, g_hi) + _dot(tri, g_mid) + _dot(tri, g_lo)
        gcum.append(gc)
        gcum_t.append(gc.T)

    qs, ks, vs, kbs, g_cols, b_cols, decays = [], [], [], [], [], [], []
    for b, hh in probs:
        q = qkv[b][:, hh * dk:(hh + 1) * dk]
        k = qkv[b][:, qk_w + hh * dk:qk_w + (hh + 1) * dk]
        vs.append(qkv[b][:, 2 * qk_w + hh * dv:2 * qk_w + (hh + 1) * dv])
        q = q * (lax.rsqrt(jnp.sum(q * q, axis=-1, keepdims=True) + EPS) * (dk ** -0.5))
        k = k * lax.rsqrt(jnp.sum(k * k, axis=-1, keepdims=True) + EPS)
        qs.append(q)
        ks.append(k)
        kbs.append(k.astype(BF16))
        g_col = gcum[b][:, hh:hh + 1]
        g_row = gcum_t[b][hh:hh + 1, :]
        g_cols.append(g_col)
        b_cols.append(beta_all[b * chunk:(b + 1) * chunk, nh + hh:nh + hh + 1])
        decays.append(jnp.exp(jnp.where(incl, g_col - g_row, -jnp.inf)))

    kks = [_dot_nt(kb, kb) for kb in kbs]
    qks = [(_dot_nt(q.astype(BF16), kb) * dec).astype(BF16) for q, kb, dec in zip(qs, kbs, decays)]
    a_mats = [jnp.where(strict, bc * kk * dec, 0.0) for bc, kk, dec in zip(b_cols, kks, decays)]
    t_invs = _unit_lower_inverses(a_mats, ii, jj)

    e_gs = [jnp.exp(g) for g in g_cols]
    rhs_us = [v * bc for v, bc in zip(vs, b_cols)]
    rhs_ws = [(k * (bc * eg)).astype(BF16) for k, bc, eg in zip(ks, b_cols, e_gs)]
    g_lasts = [g[chunk - 1:chunk, :] for g in g_cols]
    q_decs = [(q * eg).astype(BF16) for q, eg in zip(qs, e_gs)]
    k_dec_ts = [(k * jnp.exp(gl - g)).T.astype(BF16) for k, gl, g in zip(ks, g_lasts, g_cols)]

    s_olds = [state[b, hh] for b, hh in probs]
    s_bs = [s.astype(BF16) for s in s_olds]
    resid = [(ru - _dot(rw, sb)).astype(BF16) for ru, rw, sb in zip(rhs_us, rhs_ws, s_bs)]
    v_nbs = [_dot(t.astype(BF16), r).astype(BF16) for t, r in zip(t_invs, resid)]
    outs = [_dot(jnp.concatenate([qd, qk], axis=1), jnp.concatenate([sb, vn], axis=0))
            for qd, sb, qk, vn in zip(q_decs, s_bs, qks, v_nbs)]
    for (b, hh), s_old, gl, kdt, vn in zip(probs, s_olds, g_lasts, k_dec_ts, v_nbs):
        state[b, hh] = s_old * jnp.exp(gl) + _dot(kdt, vn)
    for (b, hh), o, zg in zip(probs, outs, z_gates):
        o = o * _rms_scale(o, dv) * normw_ref[...] * zg
        o_ref[b, :, hh * dv:(hh + 1) * dv] = o.astype(o_ref.dtype)


def _gdn(h2, w_proj, conv_w, a_log, dt_bias, norm_w):
    bsz, s, d = h2.shape
    chunk = min(GDN_CHUNK, s)
    n_chunks = s // chunk
    nh, dv = GDN_HEADS, GDN_DV
    cw = conv_w.shape[1]
    full = lambda j: (0, 0)
    return pl.pallas_call(
        functools.partial(_gdn_kernel, chunk=chunk),
        grid=(n_chunks,),
        in_specs=[pl.BlockSpec((bsz, chunk, d), lambda j: (0, 0, 0)),
                  pl.BlockSpec((bsz, chunk, d), lambda j: (0, jnp.minimum(j + 1, n_chunks - 1), 0)),
                  pl.BlockSpec(w_proj.shape, full),
                  pl.BlockSpec(conv_w.shape, full),
                  pl.BlockSpec((1, LANES), full),
                  pl.BlockSpec((1, LANES), full),
                  pl.BlockSpec((1, dv), full)],
        out_specs=pl.BlockSpec((bsz, chunk, nh * dv), lambda j: (0, j, 0)),
        out_shape=jax.ShapeDtypeStruct((bsz, s, nh * dv), BF16),
        scratch_shapes=[pltpu.VMEM((bsz * chunk, w_proj.shape[1]), F32),
                        pltpu.VMEM((bsz, chunk + SUBLANES, cw), F32),
                        pltpu.VMEM((bsz, nh, GDN_DK, dv), F32)],
        compiler_params=pltpu.CompilerParams(
            dimension_semantics=("arbitrary",),
            vmem_limit_bytes=VMEM_LIMIT),
        name="gdn",
    )(h2, h2, w_proj, conv_w, a_log, dt_bias, norm_w)


def _mla_prep_kernel(h_ref, pos_ref, wm_ref, wuq_ref, wukv_ref, qnw_ref, kvnw_ref,
                     wqn_ref, wqr_ref, wqrs_ref, wkn_ref, wkr_ref,
                     invf_ref, sel_ref, qexte_ref, qexto_ref, kexte_ref, kexto_ref,
                     qt_ref, k_ref, vt_ref, *, n_sub):
    nh = MLA_HEADS
    scale = (MLA_NOPE + MLA_ROPE) ** -0.5 * LOG2_E
    o1 = MLA_Q_LORA
    o2 = o1 + MLA_KV_LORA
    pw = 2 * MLA_NOPE + 2 * LANES
    kw = MLA_NOPE + MLA_V
    tm = h_ref.shape[1]
    ts = tm // n_sub
    subs = [slice(r * ts, (r + 1) * ts) for r in range(n_sub)]
    lo = lax.broadcasted_iota(jnp.int32, (1, LANES), 1) < MLA_ROPE

    ang_t = invf_ref[...] * pos_ref[0].astype(F32)
    trig_t = _split3(jnp.concatenate([jnp.cos(ang_t), jnp.sin(ang_t)], axis=0))

    def half_sums(x):
        sq = x * x
        inv_lo = lax.rsqrt(jnp.sum(jnp.where(lo, sq, 0.0), axis=-1, keepdims=True)
                           * (1.0 / MLA_ROPE) + EPS)
        inv_hi = lax.rsqrt(jnp.sum(jnp.where(lo, 0.0, sq), axis=-1, keepdims=True)
                           * (1.0 / MLA_ROPE) + EPS)
        return jnp.where(lo, inv_lo, inv_hi)

    def latent(rows):
        return _dot(h_ref[0, rows], wm_ref[...])

    def up_project(lat):
        cq = lat[:, :o1]
        ckv = lat[:, o1:o2]
        cqn = (cq * _rms_scale(cq, MLA_Q_LORA) * qnw_ref[...]).astype(BF16)
        ckvn = (ckv * _rms_scale(ckv, MLA_KV_LORA) * kvnw_ref[...]).astype(BF16)
        return _dot(cqn, wuq_ref[...]), _dot(ckvn, wukv_ref[...])

    def finish(rows, lat, qf, kvf):
        parts = [_dot_tn(part[:, rows], sel_ref[...]) for part in trig_t]
        trig = (parts[0] + parts[1]) + parts[2]
        trig_r = pltpu.roll(trig, MLA_ROPE, axis=1)
        cos4 = jnp.where(lo, trig, trig_r)
        sin4 = jnp.where(lo, trig_r, trig)

        kr2 = lat[:, o2:o2 + LANES]
        a = kr2 * wkr_ref[...] * trig
        k_rope = (a + pltpu.roll(a, MLA_ROPE, axis=1)) * lax.rsqrt(
            jnp.sum(jnp.where(lo, kr2 * kr2, 0.0), axis=-1, keepdims=True) * (1.0 / MLA_ROPE) + EPS)
        k_rope_even = jnp.where(lo, k_rope, 0.0) + kexte_ref[...]
        k_rope_odd = jnp.where(lo, 0.0, k_rope) + kexto_ref[...]
        for pair in range(nh // 2):
            base = pair * pw
            qr2 = qf[:, base + 2 * MLA_NOPE:base + 2 * MLA_NOPE + LANES]
            qrs2 = qf[:, base + 2 * MLA_NOPE + LANES:base + pw]
            q_rope2 = ((qr2 * wqr_ref[...] * cos4 + qrs2 * wqrs_ref[...] * sin4)
                       * (half_sums(qr2) * scale))
            for odd in range(2):
                hh = 2 * pair + odd
                qn = qf[:, base + odd * MLA_NOPE:base + (odd + 1) * MLA_NOPE]
                qn = qn * _rms_scale(qn, MLA_NOPE) * (wqn_ref[...] * scale)
                if odd:
                    q_rope = jnp.where(lo, 0.0, q_rope2) + qexto_ref[...]
                else:
                    q_rope = jnp.where(lo, q_rope2, 0.0) + qexte_ref[...]
                qt_ref[0, hh, :, rows] = jnp.concatenate([qn, q_rope], axis=1).T.astype(BF16)
                kn = kvf[:, hh * kw:hh * kw + MLA_NOPE]
                v = kvf[:, hh * kw + MLA_NOPE:(hh + 1) * kw]
                kn = kn * _rms_scale(kn, MLA_NOPE) * wkn_ref[...]
                k_rope_h = k_rope_odd if odd else k_rope_even
                k_ref[0, hh, rows, :] = jnp.concatenate([kn, k_rope_h], axis=1).astype(BF16)
                vt_ref[0, hh, :, rows] = v.T.astype(BF16)

    lats = [latent(subs[0])]
    ups = []
    for r in range(n_sub):
        if r + 1 < n_sub:
            lats.append(latent(subs[r + 1]))
        ups.append(up_project(lats[r]))
        if r >= 1:
            finish(subs[r - 1], lats[r - 1], *ups[r - 1])
    finish(subs[n_sub - 1], lats[n_sub - 1], *ups[n_sub - 1])


def _mla_prep(h2, pos3, w_m, w_uq, w_ukv, vecs, *, tm=1024):
    bsz, s, d = h2.shape
    tm = min(tm, s)
    nh = MLA_HEADS
    qk_dim = MLA_NOPE + LANES
    full = lambda b, i: (0, 0)
    vec_specs = [pl.BlockSpec(v.shape, full) for v in vecs]
    assert tm % (MLA_SUBTILES * LANES) == 0
    return pl.pallas_call(
        functools.partial(_mla_prep_kernel, n_sub=MLA_SUBTILES),
        grid=(bsz, s // tm),
        in_specs=[pl.BlockSpec((1, tm, d), lambda b, i: (b, i, 0)),
                  pl.BlockSpec((1, 1, tm), lambda b, i: (b, 0, i)),
                  pl.BlockSpec(w_m.shape, full),
                  pl.BlockSpec(w_uq.shape, full),
                  pl.BlockSpec(w_ukv.shape, full)] + vec_specs,
        out_specs=[pl.BlockSpec((1, nh, qk_dim, tm), lambda b, i: (b, 0, 0, i)),
                   pl.BlockSpec((1, nh, tm, qk_dim), lambda b, i: (b, 0, i, 0)),
                   pl.BlockSpec((1, nh, MLA_V, tm), lambda b, i: (b, 0, 0, i))],
        out_shape=[jax.ShapeDtypeStruct((bsz, nh, qk_dim, s), BF16),
                   jax.ShapeDtypeStruct((bsz, nh, s, qk_dim), BF16),
                   jax.ShapeDtypeStruct((bsz, nh, MLA_V, s), BF16)],
        compiler_params=pltpu.CompilerParams(
            dimension_semantics=("parallel", "parallel"),
            vmem_limit_bytes=VMEM_LIMIT),
        name="mla_prep",
    )(h2, pos3, w_m, w_uq, w_ukv, *vecs)


def _attn_kernel(mode_ref, qt_ref, k_ref, vt_ref, nw_ref, o_ref, s_scr, m_scr, l_scr, acc_scr, *,
                 tq, tk):
    i = pl.program_id(2)
    per_q = tq // tk
    kc = lax.broadcasted_iota(jnp.int32, (tk, tq), 0) // ATTN_CHUNK
    qc = lax.broadcasted_iota(jnp.int32, (tk, tq), 1) // ATTN_CHUNK
    visible = qc >= kc

    def key_tile(hh, j):
        start = pl.multiple_of(j * tk, tk)
        return k_ref[0, hh, pl.ds(start, tk), :], vt_ref[0, hh, :, pl.ds(start, tk)]

    def finish():
        for hh in range(2):
            o = (acc_scr[hh] / l_scr[hh]).T
            o = o * _rms_scale(o, MLA_V) * nw_ref[...]
            o_ref[0, :, hh * MLA_V:(hh + 1) * MLA_V] = o.astype(o_ref.dtype)

    l_scr[...] = jnp.zeros(l_scr.shape, F32)
    acc_scr[...] = jnp.zeros(acc_scr.shape, F32)

    @pl.when(mode_ref[0] == 1)
    def _fixed_reference():
        def tile(hh, j, q0=0, mask=None):
            kj, vj = key_tile(hh, j)
            st = _dot(kj, qt_ref[0, hh, :, q0:])
            if mask is not None:
                st = jnp.where(mask, st, -jnp.inf)
            p = jnp.exp2(st)
            l_scr[hh, :, q0:] += jnp.sum(p, axis=0, keepdims=True)
            acc_scr[hh, :, q0:] += _dot(vj, p.astype(BF16))

        def body(jj, _):
            for r in range(per_q):
                tile(0, jj * per_q + r)
                tile(1, jj * per_q + r)
            return 0

        lax.fori_loop(0, i, body, 0)
        for r in range(per_q):
            for hh in range(2):
                tile(hh, i * per_q + r, r * tk, visible[:, :tq - r * tk])
        finish()

    @pl.when(mode_ref[0] == 0)
    def _online():
        def scores(hh, j, q0=0):
            kj, _ = key_tile(hh, j)
            s_scr[hh, :, :tq - q0] = _dot(kj, qt_ref[0, hh, :, q0:])

        def accumulate(hh, j, q0=0, mask=None):
            w = tq - q0
            if mask is not None:
                s_scr[hh, :, :w] = jnp.where(mask, s_scr[hh, :, :w], -jnp.inf)
            m = m_scr[hh, :, q0:]
            m_new = jnp.maximum(m, jnp.max(s_scr[hh, :, :w], axis=0, keepdims=True))
            alpha = jnp.exp2(m - m_new)
            p = jnp.exp2(s_scr[hh, :, :w] - m_new)
            m_scr[hh, :, q0:] = m_new
            l_scr[hh, :, q0:] = alpha * l_scr[hh, :, q0:] + jnp.sum(p, axis=0, keepdims=True)
            _, vj = key_tile(hh, j)
            acc_scr[hh, :, q0:] = alpha * acc_scr[hh, :, q0:] + _dot(vj, p.astype(BF16))

        m_scr[...] = jnp.full(m_scr.shape, -jnp.inf, F32)
        scores(0, 0)

        def body(jj, _):
            for r in range(per_q):
                j = jj * per_q + r
                scores(1, j)
                accumulate(0, j)
                scores(0, j + 1)
                accumulate(1, j)
            return 0

        lax.fori_loop(0, i, body, 0)
        for r in range(per_q):
            q0 = r * tk
            j = i * per_q + r
            scores(1, j, q0)
            accumulate(0, j, q0, visible[:, :tq - q0])
            if r < per_q - 1:
                scores(0, j + 1, q0 + tk)
            accumulate(1, j, q0, visible[:, :tq - q0])
        finish()


def _attn(mode, qt, k, vt, out_norm_w, *, tq=2048, tk=512):
    bsz, nh, qk_dim, s = qt.shape
    tq = min(tq, s)
    tk = min(tk, tq)
    heads = 2
    return pl.pallas_call(
        functools.partial(_attn_kernel, tq=tq, tk=tk),
        grid=(bsz, nh // heads, s // tq),
        in_specs=[pl.BlockSpec(memory_space=pltpu.SMEM),
                  pl.BlockSpec((1, heads, qk_dim, tq), lambda b, h, i: (b, h, 0, i)),
                  pl.BlockSpec((1, heads, s, qk_dim), lambda b, h, i: (b, h, 0, 0)),
                  pl.BlockSpec((1, heads, MLA_V, s), lambda b, h, i: (b, h, 0, 0)),
                  pl.BlockSpec((1, MLA_V), lambda b, h, i: (0, 0))],
        out_specs=pl.BlockSpec((1, tq, heads * MLA_V), lambda b, h, i: (b, i, h)),
        out_shape=jax.ShapeDtypeStruct((bsz, s, nh * MLA_V), BF16),
        scratch_shapes=[pltpu.VMEM((heads, tk, tq), F32),
                        pltpu.VMEM((heads, 1, tq), F32),
                        pltpu.VMEM((heads, 1, tq), F32),
                        pltpu.VMEM((heads, MLA_V, tq), F32)],
        compiler_params=pltpu.CompilerParams(
            dimension_semantics=("parallel", "parallel", "arbitrary"),
            vmem_limit_bytes=VMEM_LIMIT),
        name="attn",
    )(mode, qt, k, vt, out_norm_w)


def _pad_lanes(w, width=LANES):
    return jnp.pad(w, [(0, 0)] * (w.ndim - 1) + [(0, width - w.shape[-1])])


def _swap_halves(w):
    half = w.shape[-1] // 2
    return jnp.concatenate([w[..., half:], w[..., :half]], axis=-1)


def _layer(x, mods, pos3, w_ffn1_in, w_ffn1_out, w_in, conv_w, a_log, dt_bias, gdn_norm_w,
           q_norm_w, w_uq, kv_norm_w, w_ukv, qn_q_nope, qn_q_rope, qn_k_nope, qn_k_rope,
           out_norm_w, w_out, w_ffn2_in, w_ffn2_out):
    sh1, s1, g1, sh2, s2, g2, sh3, s3, g3 = mods
    nh = MLA_HEADS

    w_gdn, w_m = _split_w_in(w_in.T)
    per_q = MLA_NOPE + MLA_ROPE
    uq_parts = []
    for pair in range(nh // 2):
        wn, wr = [], []
        for hh in (2 * pair, 2 * pair + 1):
            wn.append(w_uq[:, hh * per_q:hh * per_q + MLA_NOPE])
            wr.append(w_uq[:, hh * per_q + MLA_NOPE:(hh + 1) * per_q])
        uq_parts += wn + wr + [_swap_halves(w) for w in wr]
    w_uq_p = jnp.concatenate(uq_parts, axis=1).astype(BF16)
    w_ukv_b = w_ukv.astype(BF16)

    half = MLA_ROPE // 2
    inv_freq = ROPE_BASE ** (-jnp.arange(half, dtype=F32) / half)
    invf = inv_freq.reshape(half, 1)
    eye = jnp.eye(half, dtype=F32)
    zero = jnp.zeros((half, 2 * half), F32)
    trig_sel = jnp.concatenate([jnp.concatenate([eye, eye, zero], axis=1),
                                jnp.concatenate([zero, -eye, eye], axis=1)], axis=0).astype(BF16)
    row = lambda v: v.reshape(1, -1)

    def sq_norm_bound(w_nope, w_rope):
        return MLA_NOPE * jnp.max(w_nope * w_nope) + MLA_ROPE * jnp.max(w_rope * w_rope)
    q_scale = (MLA_NOPE + MLA_ROPE) ** -0.5 * LOG2_E
    bound = (SCORE_BOUND_MARGIN * q_scale
             * jnp.sqrt(sq_norm_bound(qn_q_nope, qn_q_rope) * sq_norm_bound(qn_k_nope, qn_k_rope)))
    use_bound = bound < MAX_FIXED_REFERENCE
    mode = use_bound.astype(jnp.int32).reshape(1)
    lane_id = jnp.arange(LANES)
    ext_even = (lane_id == MLA_ROPE).astype(F32).reshape(1, LANES)
    ext_odd = (lane_id == 0).astype(F32).reshape(1, LANES)
    shift = jnp.where(use_bound, -bound, 0.0)
    twice = lambda v: jnp.concatenate([v, v])
    vecs = [row(q_norm_w), row(kv_norm_w),
            row(qn_q_nope), row(twice(qn_q_rope)), row(twice(_swap_halves(qn_q_rope))),
            row(qn_k_nope), row(jnp.concatenate([qn_k_rope, _swap_halves(qn_k_rope)])),
            invf, trig_sel, ext_even * shift, ext_odd * shift, ext_even, ext_odd]

    w1_in, w1_out, w2_in, w2_out = _to_bf16([w_ffn1_in, w_ffn1_out, w_ffn2_in, w_ffn2_out])
    x1, h2 = _ffn(x, sh1, s1, g1, w1_in, w1_out, nxt=(sh2, s2))

    o_a = _gdn(h2, w_gdn, conv_w, row(_pad_lanes(a_log)), row(_pad_lanes(dt_bias)), row(gdn_norm_w))
    qt, k, vt = _mla_prep(h2, pos3, w_m, w_uq_p, w_ukv_b, vecs)
    o_b = _attn(mode, qt, k, vt, row(out_norm_w))

    return _ffn(x1, sh3, s3, g3, w2_in, w2_out, mix=(o_a, o_b, w_out.astype(BF16), g2))


def kernel(x, c, positions, w_ada, b_ada, ffn1_w_in, ffn1_w_out, w_in, gdn_conv_w, gdn_a_log, gdn_dt_bias, gdn_norm_w, mla_q_norm_w, mla_w_uq, mla_kv_norm_w, mla_w_ukv, qkn_q_nope, qkn_q_rope, qkn_k_nope, qkn_k_rope, mla_out_norm_w, w_out, ffn2_w_in, ffn2_w_out):
    bsz, s, d = x.shape
    pos3 = positions.reshape(bsz, 1, s)
    for l in range(w_ada.shape[0]):
        mod = _mod(c, w_ada[l], b_ada[l])
        mods = [m.reshape(bsz, 1, d) for m in jnp.split(mod, N_MOD, axis=-1)]
        x = _layer(x, mods, pos3, ffn1_w_in[l], ffn1_w_out[l], w_in[l], gdn_conv_w[l],
                   gdn_a_log[l], gdn_dt_bias[l], gdn_norm_w[l], mla_q_norm_w[l], mla_w_uq[l],
                   mla_kv_norm_w[l], mla_w_ukv[l], qkn_q_nope[l], qkn_q_rope[l], qkn_k_nope[l],
                   qkn_k_rope[l], mla_out_norm_w[l], w_out[l], ffn2_w_in[l], ffn2_w_out[l])
    return x
```

```python
import functools

import jax
import jax.numpy as jnp
from jax import lax
from jax.experimental import pallas as pl
from jax.experimental.pallas import tpu as pltpu

F32 = jnp.float32
BF16 = jnp.bfloat16

EPS = 1e-6
ATTN_CHUNK = 64
GDN_HEADS = 4
GDN_DK = 128
GDN_DV = 128
CONV_K = 4
MLA_HEADS = 4
MLA_NOPE = 128
MLA_ROPE = 64
MLA_V = 128
MLA_Q_LORA = 384
MLA_KV_LORA = 256
ROPE_BASE = 10000.0
N_MOD = 9
LOG2_E = 1.4426950408889634
MAX_FIXED_REFERENCE = 60.0
SCORE_BOUND_MARGIN = 1.0625

LANES = 128
SUBLANES = 8
VMEM_LIMIT = 56 * 1024 * 1024

GDN_CHUNK = 128
GDN_PROJ_BLOCK = 256
GDN_CONV_PIECE = 512
FFN_SUBTILES = 4
MLA_SUBTILES = 4


def _dot(a, b):
    return jnp.dot(a, b, preferred_element_type=F32)


def _dot_nt(a, b):
    return lax.dot_general(a, b, (((1,), (1,)), ((), ())), preferred_element_type=F32)


def _dot_tn(a, b):
    return lax.dot_general(a, b, (((0,), (0,)), ((), ())), preferred_element_type=F32)


def _silu(x):
    return x * jax.nn.sigmoid(x)


def _rms_scale(x, n):
    return lax.rsqrt(jnp.sum(x * x, axis=-1, keepdims=True) * (1.0 / n) + EPS)


def _cast_kernel(*refs):
    n = len(refs) // 2
    for x_ref, o_ref in zip(refs[:n], refs[n:]):
        o_ref[...] = x_ref[...].astype(o_ref.dtype)


def _to_bf16(weights, *, row_blocks=16):
    specs = []
    for w in weights:
        rows, cols = w.shape
        tr = rows // row_blocks
        assert rows % row_blocks == 0 and tr % (2 * SUBLANES) == 0 and cols % LANES == 0
        specs.append(pl.BlockSpec((tr, cols), lambda i: (i, 0)))
    return pl.pallas_call(
        _cast_kernel,
        grid=(row_blocks,),
        in_specs=specs,
        out_specs=specs,
        out_shape=[jax.ShapeDtypeStruct(w.shape, BF16) for w in weights],
        compiler_params=pltpu.CompilerParams(dimension_semantics=("arbitrary",),
                                             vmem_limit_bytes=VMEM_LIMIT),
        name="cast",
    )(*weights)


def _split_w_in_kernel(wt_ref, gdn_ref, m_ref):
    o_gz = 2 * GDN_HEADS * GDN_DK + 2 * GDN_HEADS * GDN_DV
    o_ab = o_gz + 2 * GDN_HEADS
    o_ckv = o_ab + MLA_Q_LORA + MLA_KV_LORA
    wt = wt_ref[...]
    gdn_ref[...] = wt[:o_gz + LANES, :].T.astype(BF16)
    kr = wt[o_ckv:, :]
    half = MLA_ROPE // 2
    m_ref[...] = jnp.concatenate([wt[o_ab:o_ckv, :], kr, kr[half:, :], kr[:half, :]],
                                 axis=0).T.astype(BF16)


def _split_w_in(w_in_t, *, col_blocks=4):
    feats, d = w_in_t.shape
    tc = d // col_blocks
    n_gdn = 2 * GDN_HEADS * GDN_DK + 2 * GDN_HEADS * GDN_DV + LANES
    n_m = MLA_Q_LORA + MLA_KV_LORA + 2 * MLA_ROPE
    assert d % col_blocks == 0 and tc % LANES == 0
    return pl.pallas_call(
        _split_w_in_kernel,
        grid=(col_blocks,),
        in_specs=[pl.BlockSpec((feats, tc), lambda i: (0, i))],
        out_specs=[pl.BlockSpec((tc, n_gdn), lambda i: (i, 0)),
                   pl.BlockSpec((tc, n_m), lambda i: (i, 0))],
        out_shape=[jax.ShapeDtypeStruct((d, n_gdn), BF16),
                   jax.ShapeDtypeStruct((d, n_m), BF16)],
        compiler_params=pltpu.CompilerParams(dimension_semantics=("arbitrary",),
                                             vmem_limit_bytes=VMEM_LIMIT),
        name="split_w_in",
    )(w_in_t)


def _mod_kernel(c_ref, w_ref, b_ref, o_ref):
    sc = _silu(c_ref[...])
    o_ref[...] = jnp.dot(sc, w_ref[...], precision=lax.Precision.HIGHEST,
                         preferred_element_type=F32) + b_ref[...]


def _mod(c, w_ada, b_ada):
    bsz, d = c.shape
    n = w_ada.shape[1]
    tn = 1536
    assert n % tn == 0
    return pl.pallas_call(
        _mod_kernel,
        grid=(n // tn,),
        in_specs=[pl.BlockSpec((bsz, d), lambda j: (0, 0)),
                  pl.BlockSpec((d, tn), lambda j: (0, j)),
                  pl.BlockSpec((1, tn), lambda j: (0, j))],
        out_specs=pl.BlockSpec((bsz, tn), lambda j: (0, j)),
        out_shape=jax.ShapeDtypeStruct((bsz, n), F32),
        compiler_params=pltpu.CompilerParams(dimension_semantics=("arbitrary",),
                                             vmem_limit_bytes=VMEM_LIMIT),
        name="mod",
    )(c, w_ada, b_ada.reshape(1, n))


def _ffn_kernel(*refs, has_mix, emit_next, d_model, d_ff, tf, n_sub):
    it = iter(refs)
    x_ref = next(it)
    if has_mix:
        oa_ref, ob_ref, wmix_ref, gmix_ref = next(it), next(it), next(it), next(it)
    shift_ref, scale_ref, gate_ref = next(it), next(it), next(it)
    win_ref, wout_ref = next(it), next(it)
    if emit_next:
        nshift_ref, nscale_ref = next(it), next(it)
    out_ref = next(it)
    if emit_next:
        hn_ref = next(it)
    h_sc, a_sc = next(it), next(it)

    tm = x_ref.shape[1]
    ts = tm // n_sub
    subs = [slice(r * ts, (r + 1) * ts) for r in range(n_sub)]

    def prologue(rows):
        x = x_ref[0, rows]
        if has_mix:
            half = oa_ref.shape[-1]
            y = (_dot(oa_ref[0, rows], wmix_ref[:half, :])
                 + _dot(ob_ref[0, rows], wmix_ref[half:, :]))
            x = x + gmix_ref[0] * y
            out_ref[0, rows] = x
        h_sc[rows] = (x * _rms_scale(x, d_model) * (1.0 + scale_ref[0])
                      + shift_ref[0]).astype(BF16)

    def hidden(rows):
        h = h_sc[rows]
        for c in range(d_ff // tf):
            g = _dot(h, win_ref[:, c * tf:(c + 1) * tf])
            u = _dot(h, win_ref[:, d_ff + c * tf:d_ff + (c + 1) * tf])
            a_sc[rows, c * tf:(c + 1) * tf] = (_silu(g) * u).astype(BF16)

    def epilogue(rows):
        x = out_ref[0, rows] if has_mix else x_ref[0, rows]
        out = x + 0.5 * gate_ref[0] * _dot(a_sc[rows], wout_ref[...])
        out_ref[0, rows] = out
        if emit_next:
            hn = out * _rms_scale(out, d_model) * (1.0 + nscale_ref[0]) + nshift_ref[0]
            hn_ref[0, rows] = hn.astype(BF16)

    prologue(subs[0])
    for r in range(n_sub):
        if r + 1 < n_sub:
            prologue(subs[r + 1])
        hidden(subs[r])
        epilogue(subs[r])


def _resident(shape, index_map):
    return pl.BlockSpec(shape, index_map, pipeline_mode=pl.Buffered(1))


def _ffn(x, shift, scale, gate, w_in_bf, w_out_bf, *, mix=None, nxt=None, tm=1024, tf=256):
    bsz, s, d = x.shape
    ff = w_out_bf.shape[0]
    tm = min(tm, s)
    assert ff % tf == 0 and s % tm == 0 and tm % (FFN_SUBTILES * 2 * SUBLANES) == 0
    has_mix = mix is not None
    emit_next = nxt is not None

    row = lambda b, i: (b, i, 0)
    vec = lambda b, i: (b, 0, 0)
    const = lambda b, i: (0, 0)
    args = [x]
    in_specs = [pl.BlockSpec((1, tm, d), row)]
    if has_mix:
        o_a, o_b, w_mix, g_mix = mix
        args += [o_a, o_b, w_mix, g_mix]
        in_specs += [pl.BlockSpec((1, tm, o_a.shape[-1]), row),
                     pl.BlockSpec((1, tm, o_b.shape[-1]), row),
                     _resident(w_mix.shape, const),
                     pl.BlockSpec((1, 1, d), vec)]
    args += [shift, scale, gate, w_in_bf, w_out_bf]
    in_specs += [pl.BlockSpec((1, 1, d), vec)] * 3
    in_specs += [_resident(w_in_bf.shape, const), _resident(w_out_bf.shape, const)]
    if emit_next:
        args += list(nxt)
        in_specs += [pl.BlockSpec((1, 1, d), vec)] * 2
    out_shape = [jax.ShapeDtypeStruct((bsz, s, d), F32)]
    out_specs = [pl.BlockSpec((1, tm, d), row)]
    if emit_next:
        out_shape.append(jax.ShapeDtypeStruct((bsz, s, d), BF16))
        out_specs.append(pl.BlockSpec((1, tm, d), row))

    res = pl.pallas_call(
        functools.partial(_ffn_kernel, has_mix=has_mix, emit_next=emit_next, d_model=d,
                          d_ff=ff, tf=tf, n_sub=FFN_SUBTILES),
        grid=(bsz, s // tm),
        in_specs=in_specs,
        out_specs=out_specs,
        out_shape=out_shape,
        scratch_shapes=[pltpu.VMEM((tm, d), BF16), pltpu.VMEM((tm, ff), BF16)],
        compiler_params=pltpu.CompilerParams(
            dimension_semantics=("parallel", "parallel"),
            vmem_limit_bytes=VMEM_LIMIT),
        name="ffn_mix" if has_mix else "ffn",
    )(*args)
    return res if emit_next else res[0]


def _unit_lower_inverses(a_mats, ii, jj):
    n = a_mats[0].shape[0]
    eye = (ii == jj).astype(F32)

    def same_block(size):
        return (ii // size) == (jj // size)

    base = same_block(SUBLANES)
    d1 = [jnp.where(base, a, 0.0) for a in a_mats]
    d1b = [d.astype(BF16) for d in d1]
    d2 = [_dot(d, d) for d in d1b]
    d2b = [d.astype(BF16) for d in d2]
    d3 = [_dot(a, b) for a, b in zip(d1b, d2b)]
    d4b = [_dot(d, d).astype(BF16) for d in d2b]
    p1 = [eye - a + b - c for a, b, c in zip(d1, d2, d3)]
    ts = [p + _dot(p.astype(BF16), d) for p, d in zip(p1, d4b)]
    size = SUBLANES
    while size < n:
        below = same_block(2 * size) & jnp.logical_not(same_block(size))
        offs = [jnp.where(below, a, 0.0).astype(BF16) for a in a_mats]
        tbs = [t.astype(BF16) for t in ts]
        xs = [_dot(o, t).astype(BF16) for o, t in zip(offs, tbs)]
        ts = [t - _dot(tb, x) for t, tb, x in zip(ts, tbs, xs)]
        size *= 2
    return ts


def _split3(x):
    hi = x.astype(BF16)
    r1 = x - hi.astype(F32)
    mid = r1.astype(BF16)
    lo = (r1 - mid.astype(F32)).astype(BF16)
    return hi, mid, lo


def _gdn_kernel(h0_ref, hn_ref, wproj_ref, convw_ref, alog_ref, dtb_ref, normw_ref,
                o_ref, proj_scr, xbuf, state, *, chunk):
    nb, _, d = hn_ref.shape
    nh, dk, dv = GDN_HEADS, GDN_DK, GDN_DV
    qk_w = nh * dk
    conv_w = 2 * qk_w + nh * dv
    z_off = conv_w
    ab_off = conv_w + nh * dv
    pad = SUBLANES
    probs = [(b, hh) for b in range(nb) for hh in range(nh)]

    @pl.when(pl.program_id(0) == 0)
    def _():
        state[...] = jnp.zeros_like(state)
        xbuf[:, 0:pad, :] = jnp.zeros((nb, pad, conv_w), F32)
        proj_scr[...] = _dot(h0_ref[...].reshape(nb * chunk, d), wproj_ref[...])

    for b in range(nb):
        xbuf[b, pad:pad + chunk, :] = proj_scr[b * chunk:(b + 1) * chunk, :conv_w]
    ab = proj_scr[:, ab_off:ab_off + LANES]
    z_gates = [_silu(proj_scr[b * chunk:(b + 1) * chunk, z_off + hh * dv:z_off + (hh + 1) * dv])
               for b, hh in probs]
    h_next = hn_ref[...].reshape(nb * chunk, d)
    n_cols = proj_scr.shape[1]
    col_blocks = [(c0, min(c0 + GDN_PROJ_BLOCK, n_cols))
                  for c0 in range(0, n_cols, GDN_PROJ_BLOCK)]

    g_all = -jnp.exp(alog_ref[...]) * jax.nn.softplus(ab + dtb_ref[...])
    beta_all = jax.nn.sigmoid(ab)

    ii = lax.broadcasted_iota(jnp.int32, (chunk, chunk), 0)
    jj = lax.broadcasted_iota(jnp.int32, (chunk, chunk), 1)
    incl = ii >= jj
    strict = ii > jj
    tri = incl.astype(BF16)
    cw = convw_ref[...]

    n_pieces = nb * (conv_w // GDN_CONV_PIECE)
    qkv, gcum, gcum_t = [], [], []
    piece = 0
    for b in range(nb):
        rows = slice(b * chunk, (b + 1) * chunk)
        parts = []
        for p0 in range(0, conv_w, GDN_CONV_PIECE):
            for c0, c1 in col_blocks[piece::n_pieces]:
                proj_scr[:, c0:c1] = _dot(h_next, wproj_ref[:, c0:c1])
            piece += 1
            cols = slice(p0, p0 + GDN_CONV_PIECE)
            xe = xbuf[b, :, cols]
            conv = xe[pad:] * cw[CONV_K - 1:CONV_K, cols]
            for tap in range(1, CONV_K):
                conv = conv + (pltpu.roll(xe, tap, axis=0)[pad:]
                               * cw[CONV_K - 1 - tap:CONV_K - tap, cols])
            parts.append(_silu(conv))
        xbuf[b, 0:pad, :] = xbuf[b, chunk:chunk + pad, :]
        qkv.append(jnp.concatenate(parts, axis=1))
        g_hi, g_mid, g_lo = _split3(g_all[rows, :])
        gc = _dot(tri, g_hi) + _dot(tri, g_mid) + _dot(tri, g_lo)
        gcum.append(gc)
        gcum_t.append(gc.T)

    qs, ks, vs, kbs, g_cols, b_cols, decays = [], [], [], [], [], [], []
    for b, hh in probs:
        q = qkv[b][:, hh * dk:(hh + 1) * dk]
        k = qkv[b][:, qk_w + hh * dk:qk_w + (hh + 1) * dk]
        vs.append(qkv[b][:, 2 * qk_w + hh * dv:2 * qk_w + (hh + 1) * dv])
        q = q * (lax.rsqrt(jnp.sum(q * q, axis=-1, keepdims=True) + EPS) * (dk ** -0.5))
        k = k * lax.rsqrt(jnp.sum(k * k, axis=-1, keepdims=True) + EPS)
        qs.append(q)
        ks.append(k)
        kbs.append(k.astype(BF16))
        g_col = gcum[b][:, hh:hh + 1]
        g_row = gcum_t[b][hh:hh + 1, :]
        g_cols.append(g_col)
        b_cols.append(beta_all[b * chunk:(b + 1) * chunk, nh + hh:nh + hh + 1])
        decays.append(jnp.exp(jnp.where(incl, g_col - g_row, -jnp.inf)))

    kks = [_dot_nt(kb, kb) for kb in kbs]
    qks = [(_dot_nt(q.astype(BF16), kb) * dec).astype(BF16) for q, kb, dec in zip(qs, kbs, decays)]
    a_mats = [jnp.where(strict, bc * kk * dec, 0.0) for bc, kk, dec in zip(b_cols, kks, decays)]
    t_invs = _unit_lower_inverses(a_mats, ii, jj)

    e_gs = [jnp.exp(g) for g in g_cols]
    rhs_us = [v * bc for v, bc in zip(vs, b_cols)]
    rhs_ws = [(k * (bc * eg)).astype(BF16) for k, bc, eg in zip(ks, b_cols, e_gs)]
    g_lasts = [g[chunk - 1:chunk, :] for g in g_cols]
    q_decs = [(q * eg).astype(BF16) for q, eg in zip(qs, e_gs)]
    k_dec_ts = [(k * jnp.exp(gl - g)).T.astype(BF16) for k, gl, g in zip(ks, g_lasts, g_cols)]

    s_olds = [state[b, hh] for b, hh in probs]
    s_bs = [s.astype(BF16) for s in s_olds]
    resid = [(ru - _dot(rw, sb)).astype(BF16) for ru, rw, sb in zip(rhs_us, rhs_ws, s_bs)]
    v_nbs = [_dot(t.astype(BF16), r).astype(BF16) for t, r in zip(t_invs, resid)]
    outs = [_dot(jnp.concatenate([qd, qk], axis=1), jnp.concatenate([sb, vn], axis=0))
            for qd, sb, qk, vn in zip(q_decs, s_bs, qks, v_nbs)]
    for (b, hh), s_old, gl, kdt, vn in zip(probs, s_olds, g_lasts, k_dec_ts, v_nbs):
        state[b, hh] = s_old * jnp.exp(gl) + _dot(kdt, vn)
    for (b, hh), o, zg in zip(probs, outs, z_gates):
        o = o * _rms_scale(o, dv) * normw_ref[...] * zg
        o_ref[b, :, hh * dv:(hh + 1) * dv] = o.astype(o_ref.dtype)


def _gdn(h2, w_proj, conv_w, a_log, dt_bias, norm_w):
    bsz, s, d = h2.shape
    chunk = min(GDN_CHUNK, s)
    n_chunks = s // chunk
    nh, dv = GDN_HEADS, GDN_DV
    cw = conv_w.shape[1]
    full = lambda j: (0, 0)
    return pl.pallas_call(
        functools.partial(_gdn_kernel, chunk=chunk),
        grid=(n_chunks,),
        in_specs=[pl.BlockSpec((bsz, chunk, d), lambda j: (0, 0, 0)),
                  pl.BlockSpec((bsz, chunk, d), lambda j: (0, jnp.minimum(j + 1, n_chunks - 1), 0)),
                  pl.BlockSpec(w_proj.shape, full),
                  pl.BlockSpec(conv_w.shape, full),
                  pl.BlockSpec((1, LANES), full),
                  pl.BlockSpec((1, LANES), full),
                  pl.BlockSpec((1, dv), full)],
        out_specs=pl.BlockSpec((bsz, chunk, nh * dv), lambda j: (0, j, 0)),
        out_shape=jax.ShapeDtypeStruct((bsz, s, nh * dv), BF16),
        scratch_shapes=[pltpu.VMEM((bsz * chunk, w_proj.shape[1]), F32),
                        pltpu.VMEM((bsz, chunk + SUBLANES, cw), F32),
                        pltpu.VMEM((bsz, nh, GDN_DK, dv), F32)],
        compiler_params=pltpu.CompilerParams(
            dimension_semantics=("arbitrary",),
            vmem_limit_bytes=VMEM_LIMIT),
        name="gdn",
    )(h2, h2, w_proj, conv_w, a_log, dt_bias, norm_w)


def _mla_prep_kernel(h_ref, pos_ref, wm_ref, wuq_ref, wukv_ref, qnw_ref, kvnw_ref,
                     wqn_ref, wqr_ref, wqrs_ref, wkn_ref, wkr_ref,
                     invf_ref, sel_ref, qexte_ref, qexto_ref, kexte_ref, kexto_ref,
                     qt_ref, k_ref, vt_ref, *, n_sub):
    nh = MLA_HEADS
    scale = (MLA_NOPE + MLA_ROPE) ** -0.5 * LOG2_E
    o1 = MLA_Q_LORA
    o2 = o1 + MLA_KV_LORA
    pw = 2 * MLA_NOPE + 2 * LANES
    kw = MLA_NOPE + MLA_V
    tm = h_ref.shape[1]
    ts = tm // n_sub
    subs = [slice(r * ts, (r + 1) * ts) for r in range(n_sub)]
    lo = lax.broadcasted_iota(jnp.int32, (1, LANES), 1) < MLA_ROPE

    ang_t = invf_ref[...] * pos_ref[0].astype(F32)
    trig_t = _split3(jnp.concatenate([jnp.cos(ang_t), jnp.sin(ang_t)], axis=0))

    def half_sums(x):
        sq = x * x
        inv_lo = lax.rsqrt(jnp.sum(jnp.where(lo, sq, 0.0), axis=-1, keepdims=True)
                           * (1.0 / MLA_ROPE) + EPS)
        inv_hi = lax.rsqrt(jnp.sum(jnp.where(lo, 0.0, sq), axis=-1, keepdims=True)
                           * (1.0 / MLA_ROPE) + EPS)
        return jnp.where(lo, inv_lo, inv_hi)

    def latent(rows):
        return _dot(h_ref[0, rows], wm_ref[...])

    def up_project(lat):
        cq = lat[:, :o1]
        ckv = lat[:, o1:o2]
        cqn = (cq * _rms_scale(cq, MLA_Q_LORA) * qnw_ref[...]).astype(BF16)
        ckvn = (ckv * _rms_scale(ckv, MLA_KV_LORA) * kvnw_ref[...]).astype(BF16)
        return _dot(cqn, wuq_ref[...]), _dot(ckvn, wukv_ref[...])

    def finish(rows, lat, qf, kvf):
        parts = [_dot_tn(part[:, rows], sel_ref[...]) for part in trig_t]
        trig = (parts[0] + parts[1]) + parts[2]
        trig_r = pltpu.roll(trig, MLA_ROPE, axis=1)
        cos4 = jnp.where(lo, trig, trig_r)
        sin4 = jnp.where(lo, trig_r, trig)

        kr2 = lat[:, o2:o2 + LANES]
        a = kr2 * wkr_ref[...] * trig
        k_rope = (a + pltpu.roll(a, MLA_ROPE, axis=1)) * lax.rsqrt(
            jnp.sum(jnp.where(lo, kr2 * kr2, 0.0), axis=-1, keepdims=True) * (1.0 / MLA_ROPE) + EPS)
        k_rope_even = jnp.where(lo, k_rope, 0.0) + kexte_ref[...]
        k_rope_odd = jnp.where(lo, 0.0, k_rope) + kexto_ref[...]
        for pair in range(nh // 2):
            base = pair * pw
            qr2 = qf[:, base + 2 * MLA_NOPE:base + 2 * MLA_NOPE + LANES]
            qrs2 = qf[:, base + 2 * MLA_NOPE + LANES:base + pw]
            q_rope2 = ((qr2 * wqr_ref[...] * cos4 + qrs2 * wqrs_ref[...] * sin4)
                       * (half_sums(qr2) * scale))
            for odd in range(2):
                hh = 2 * pair + odd
                qn = qf[:, base + odd * MLA_NOPE:base + (odd + 1) * MLA_NOPE]
                qn = qn * _rms_scale(qn, MLA_NOPE) * (wqn_ref[...] * scale)
                if odd:
                    q_rope = jnp.where(lo, 0.0, q_rope2) + qexto_ref[...]
                else:
                    q_rope = jnp.where(lo, q_rope2, 0.0) + qexte_ref[...]
                qt_ref[0, hh, :, rows] = jnp.concatenate([qn, q_rope], axis=1).T.astype(BF16)
                kn = kvf[:, hh * kw:hh * kw + MLA_NOPE]
                v = kvf[:, hh * kw + MLA_NOPE:(hh + 1) * kw]
                kn = kn * _rms_scale(kn, MLA_NOPE) * wkn_ref[...]
                k_rope_h = k_rope_odd if odd else k_rope_even
                k_ref[0, hh, rows, :] = jnp.concatenate([kn, k_rope_h], axis=1).astype(BF16)
                vt_ref[0, hh, :, rows] = v.T.astype(BF16)

    lats = [latent(subs[0])]
    ups = []
    for r in range(n_sub):
        if r + 1 < n_sub:
            lats.append(latent(subs[r + 1]))
        ups.append(up_project(lats[r]))
        if r >= 1:
            finish(subs[r - 1], lats[r - 1], *ups[r - 1])
    finish(subs[n_sub - 1], lats[n_sub - 1], *ups[n_sub - 1])


def _mla_prep(h2, pos3, w_m, w_uq, w_ukv, vecs, *, tm=1024):
    bsz, s, d = h2.shape
    tm = min(tm, s)
    nh = MLA_HEADS
    qk_dim = MLA_NOPE + LANES
    full = lambda b, i: (0, 0)
    vec_specs = [pl.BlockSpec(v.shape, full) for v in vecs]
    assert tm % (MLA_SUBTILES * LANES) == 0
    return pl.pallas_call(
        functools.partial(_mla_prep_kernel, n_sub=MLA_SUBTILES),
        grid=(bsz, s // tm),
        in_specs=[pl.BlockSpec((1, tm, d), lambda b, i: (b, i, 0)),
                  pl.BlockSpec((1, 1, tm), lambda b, i: (b, 0, i)),
                  pl.BlockSpec(w_m.shape, full),
                  pl.BlockSpec(w_uq.shape, full),
                  pl.BlockSpec(w_ukv.shape, full)] + vec_specs,
        out_specs=[pl.BlockSpec((1, nh, qk_dim, tm), lambda b, i: (b, 0, 0, i)),
                   pl.BlockSpec((1, nh, tm, qk_dim), lambda b, i: (b, 0, i, 0)),
                   pl.BlockSpec((1, nh, MLA_V, tm), lambda b, i: (b, 0, 0, i))],
        out_shape=[jax.ShapeDtypeStruct((bsz, nh, qk_dim, s), BF16),
                   jax.ShapeDtypeStruct((bsz, nh, s, qk_dim), BF16),
                   jax.ShapeDtypeStruct((bsz, nh, MLA_V, s), BF16)],
        compiler_params=pltpu.CompilerParams(
            dimension_semantics=("parallel", "parallel"),
            vmem_limit_bytes=VMEM_LIMIT),
        name="mla_prep",
    )(h2, pos3, w_m, w_uq, w_ukv, *vecs)


def _attn_kernel(mode_ref, qt_ref, k_ref, vt_ref, nw_ref, o_ref, s_scr, m_scr, l_scr, acc_scr, *,
                 tq, tk):
    i = pl.program_id(2)
    per_q = tq // tk
    kc = lax.broadcasted_iota(jnp.int32, (tk, tq), 0) // ATTN_CHUNK
    qc = lax.broadcasted_iota(jnp.int32, (tk, tq), 1) // ATTN_CHUNK
    visible = qc >= kc

    def key_tile(hh, j):
        start = pl.multiple_of(j * tk, tk)
        return k_ref[0, hh, pl.ds(start, tk), :], vt_ref[0, hh, :, pl.ds(start, tk)]

    def finish():
        for hh in range(2):
            o = (acc_scr[hh] / l_scr[hh]).T
            o = o * _rms_scale(o, MLA_V) * nw_ref[...]
            o_ref[0, :, hh * MLA_V:(hh + 1) * MLA_V] = o.astype(o_ref.dtype)

    l_scr[...] = jnp.zeros(l_scr.shape, F32)
    acc_scr[...] = jnp.zeros(acc_scr.shape, F32)

    @pl.when(mode_ref[0] == 1)
    def _fixed_reference():
        def tile(hh, start, size, q0=0, mask=None):
            kj = k_ref[0, hh, pl.ds(start, size), :]
            vj = vt_ref[0, hh, :, pl.ds(start, size)]
            st = _dot(kj, qt_ref[0, hh, :, q0:])
            if mask is not None:
                st = jnp.where(mask, st, -jnp.inf)
            p = jnp.exp2(st)
            l_scr[hh, :, q0:] += jnp.sum(p, axis=0, keepdims=True)
            acc_scr[hh, :, q0:] += _dot(vj, p.astype(BF16))

        def body(jj, _):
            for r in range(per_q):
                start = pl.multiple_of((jj * per_q + r) * tk, tk)
                tile(0, start, tk)
                tile(1, start, tk)
            return 0

        lax.fori_loop(0, i, body, 0)
        tkd = tk // 2
        for r in range(tq // tkd):
            start = pl.multiple_of(i * tq + r * tkd, tkd)
            for hh in range(2):
                tile(hh, start, tkd, r * tkd, visible[:tkd, :tq - r * tkd])
        finish()

    @pl.when(mode_ref[0] == 0)
    def _online():
        def scores(hh, j, q0=0):
            kj, _ = key_tile(hh, j)
            s_scr[hh, :, :tq - q0] = _dot(kj, qt_ref[0, hh, :, q0:])

        def accumulate(hh, j, q0=0, mask=None):
            w = tq - q0
            if mask is not None:
                s_scr[hh, :, :w] = jnp.where(mask, s_scr[hh, :, :w], -jnp.inf)
            m = m_scr[hh, :, q0:]
            m_new = jnp.maximum(m, jnp.max(s_scr[hh, :, :w], axis=0, keepdims=True))
            alpha = jnp.exp2(m - m_new)
            p = jnp.exp2(s_scr[hh, :, :w] - m_new)
            m_scr[hh, :, q0:] = m_new
            l_scr[hh, :, q0:] = alpha * l_scr[hh, :, q0:] + jnp.sum(p, axis=0, keepdims=True)
            _, vj = key_tile(hh, j)
            acc_scr[hh, :, q0:] = alpha * acc_scr[hh, :, q0:] + _dot(vj, p.astype(BF16))

        m_scr[...] = jnp.full(m_scr.shape, -jnp.inf, F32)
        scores(0, 0)

        def body(jj, _):
            for r in range(per_q):
                j = jj * per_q + r
                scores(1, j)
                accumulate(0, j)
                scores(0, j + 1)
                accumulate(1, j)
            return 0

        lax.fori_loop(0, i, body, 0)
        for r in range(per_q):
            q0 = r * tk
            j = i * per_q + r
            scores(1, j, q0)
            accumulate(0, j, q0, visible[:, :tq - q0])
            if r < per_q - 1:
                scores(0, j + 1, q0 + tk)
            accumulate(1, j, q0, visible[:, :tq - q0])
        finish()


def _attn(mode, qt, k, vt, out_norm_w, *, tq=2048, tk=512):
    bsz, nh, qk_dim, s = qt.shape
    tq = min(tq, s)
    tk = min(tk, tq)
    heads = 2
    return pl.pallas_call(
        functools.partial(_attn_kernel, tq=tq, tk=tk),
        grid=(bsz, nh // heads, s // tq),
        in_specs=[pl.BlockSpec(memory_space=pltpu.SMEM),
                  pl.BlockSpec((1, heads, qk_dim, tq), lambda b, h, i: (b, h, 0, i)),
                  pl.BlockSpec((1, heads, s, qk_dim), lambda b, h, i: (b, h, 0, 0)),
                  pl.BlockSpec((1, heads, MLA_V, s), lambda b, h, i: (b, h, 0, 0)),
                  pl.BlockSpec((1, MLA_V), lambda b, h, i: (0, 0))],
        out_specs=pl.BlockSpec((1, tq, heads * MLA_V), lambda b, h, i: (b, i, h)),
        out_shape=jax.ShapeDtypeStruct((bsz, s, nh * MLA_V), BF16),
        scratch_shapes=[pltpu.VMEM((heads, tk, tq), F32),
                        pltpu.VMEM((heads, 1, tq), F32),
                        pltpu.VMEM((heads, 1, tq), F32),
                        pltpu.VMEM((heads, MLA_V, tq), F32)],
        compiler_params=pltpu.CompilerParams(
            dimension_semantics=("parallel", "parallel", "arbitrary"),
            vmem_limit_bytes=VMEM_LIMIT),
        name="attn",
    )(mode, qt, k, vt, out_norm_w)


def _pad_lanes(w, width=LANES):
    return jnp.pad(w, [(0, 0)] * (w.ndim - 1) + [(0, width - w.shape[-1])])


def _swap_halves(w):
    half = w.shape[-1] // 2
    return jnp.concatenate([w[..., half:], w[..., :half]], axis=-1)


def _layer(x, mods, pos3, w_ffn1_in, w_ffn1_out, w_in, conv_w, a_log, dt_bias, gdn_norm_w,
           q_norm_w, w_uq, kv_norm_w, w_ukv, qn_q_nope, qn_q_rope, qn_k_nope, qn_k_rope,
           out_norm_w, w_out, w_ffn2_in, w_ffn2_out):
    sh1, s1, g1, sh2, s2, g2, sh3, s3, g3 = mods
    nh = MLA_HEADS

    w_gdn, w_m = _split_w_in(w_in.T)
    per_q = MLA_NOPE + MLA_ROPE
    uq_parts = []
    for pair in range(nh // 2):
        wn, wr = [], []
        for hh in (2 * pair, 2 * pair + 1):
            wn.append(w_uq[:, hh * per_q:hh * per_q + MLA_NOPE])
            wr.append(w_uq[:, hh * per_q + MLA_NOPE:(hh + 1) * per_q])
        uq_parts += wn + wr + [_swap_halves(w) for w in wr]
    w_uq_p = jnp.concatenate(uq_parts, axis=1).astype(BF16)
    w_ukv_b = w_ukv.astype(BF16)

    half = MLA_ROPE // 2
    inv_freq = ROPE_BASE ** (-jnp.arange(half, dtype=F32) / half)
    invf = inv_freq.reshape(half, 1)
    eye = jnp.eye(half, dtype=F32)
    zero = jnp.zeros((half, 2 * half), F32)
    trig_sel = jnp.concatenate([jnp.concatenate([eye, eye, zero], axis=1),
                                jnp.concatenate([zero, -eye, eye], axis=1)], axis=0).astype(BF16)
    row = lambda v: v.reshape(1, -1)

    def sq_norm_bound(w_nope, w_rope):
        return MLA_NOPE * jnp.max(w_nope * w_nope) + MLA_ROPE * jnp.max(w_rope * w_rope)
    q_scale = (MLA_NOPE + MLA_ROPE) ** -0.5 * LOG2_E
    bound = (SCORE_BOUND_MARGIN * q_scale
             * jnp.sqrt(sq_norm_bound(qn_q_nope, qn_q_rope) * sq_norm_bound(qn_k_nope, qn_k_rope)))
    use_bound = bound < MAX_FIXED_REFERENCE
    mode = use_bound.astype(jnp.int32).reshape(1)
    lane_id = jnp.arange(LANES)
    ext_even = (lane_id == MLA_ROPE).astype(F32).reshape(1, LANES)
    ext_odd = (lane_id == 0).astype(F32).reshape(1, LANES)
    shift = jnp.where(use_bound, -bound, 0.0)
    twice = lambda v: jnp.concatenate([v, v])
    vecs = [row(q_norm_w), row(kv_norm_w),
            row(qn_q_nope), row(twice(qn_q_rope)), row(twice(_swap_halves(qn_q_rope))),
            row(qn_k_nope), row(jnp.concatenate([qn_k_rope, _swap_halves(qn_k_rope)])),
            invf, trig_sel, ext_even * shift, ext_odd * shift, ext_even, ext_odd]

    w1_in, w1_out, w2_in, w2_out = _to_bf16([w_ffn1_in, w_ffn1_out, w_ffn2_in, w_ffn2_out])
    x1, h2 = _ffn(x, sh1, s1, g1, w1_in, w1_out, nxt=(sh2, s2))

    o_a = _gdn(h2, w_gdn, conv_w, row(_pad_lanes(a_log)), row(_pad_lanes(dt_bias)), row(gdn_norm_w))
    qt, k, vt = _mla_prep(h2, pos3, w_m, w_uq_p, w_ukv_b, vecs)
    o_b = _attn(mode, qt, k, vt, row(out_norm_w))

    return _ffn(x1, sh3, s3, g3, w2_in, w2_out, mix=(o_a, o_b, w_out.astype(BF16), g2))


def kernel(x, c, positions, w_ada, b_ada, ffn1_w_in, ffn1_w_out, w_in, gdn_conv_w, gdn_a_log, gdn_dt_bias, gdn_norm_w, mla_q_norm_w, mla_w_uq, mla_kv_norm_w, mla_w_ukv, qkn_q_nope, qkn_q_rope, qkn_k_nope, qkn_k_rope, mla_out_norm_w, w_out, ffn2_w_in, ffn2_w_out):
    bsz, s, d = x.shape
    pos3 = positions.reshape(bsz, 1, s)
    for l in range(w_ada.shape[0]):
        mod = _mod(c, w_ada[l], b_ada[l])
        mods = [m.reshape(bsz, 1, d) for m in jnp.split(mod, N_MOD, axis=-1)]
        x = _layer(x, mods, pos3, ffn1_w_in[l], ffn1_w_out[l], w_in[l], gdn_conv_w[l],
                   gdn_a_log[l], gdn_dt_bias[l], gdn_norm_w[l], mla_q_norm_w[l], mla_w_uq[l],
                   mla_kv_norm_w[l], mla_w_ukv[l], qkn_q_nope[l], qkn_q_rope[l], qkn_k_nope[l],
                   qkn_k_rope[l], mla_out_norm_w[l], w_out[l], ffn2_w_in[l], ffn2_w_out[l])
    return x
```
